```python
import jax
import jax.numpy as jnp
from jax import lax
import numpy as np

D_MODEL = 1024
BATCH = 2
SEQ = 16384
DEPTH = 1

GRID_W = 64
CTX_LEN = 256
HEAD_DIM = 64
N_Q_HEADS = 8
N_KV_HEADS = 2
GQA_GROUP = N_Q_HEADS // N_KV_HEADS
ATT_WIDTH = N_Q_HEADS * HEAD_DIM
KV_WIDTH = N_KV_HEADS * HEAD_DIM
ROPE_FREQS = HEAD_DIM // 4
ROPE_BASE = 10000.0
Q_BLOCK = 128
RWKV_HEAD = 64
RWKV_HEADS = 8
RWKV_WIDTH = RWKV_HEADS * RWKV_HEAD
W_LORA = 64
A_LORA = 64
SHIFT_WIDTH = 3 * RWKV_WIDTH + 2 * W_LORA + 2 * A_LORA
IN_WIDTH = 2 * ATT_WIDTH + 2 * KV_WIDTH + SHIFT_WIDTH + RWKV_WIDTH
LN_EPS = 1e-5
QK_EPS = 1e-6
GN_EPS = 64e-5
ALPHA = (2.0 * DEPTH) ** 0.25
BETA = (8.0 * DEPTH) ** -0.25

kernel_name = 'hybrid_gqa_rwkv7_deepnorm_prefix'


def _layer_norm(x):
    xf = x.astype(jnp.float32)
    mu = jnp.mean(xf, axis=-1, keepdims=True)
    var = jnp.mean(jnp.square(xf - mu), axis=-1, keepdims=True)
    return ((xf - mu) * lax.rsqrt(var + LN_EPS)).astype(x.dtype)


def _rms_norm(x, g):
    xf = x.astype(jnp.float32)
    y = xf * lax.rsqrt(jnp.mean(jnp.square(xf), axis=-1, keepdims=True) + QK_EPS)
    return y.astype(x.dtype) * g


def _axial_rope_tables(rows):
    row = jnp.repeat(jnp.arange(rows, dtype=jnp.float32), GRID_W)
    col = jnp.tile(jnp.arange(GRID_W, dtype=jnp.float32), rows)
    inv = ROPE_BASE ** (-jnp.arange(ROPE_FREQS, dtype=jnp.float32) / ROPE_FREQS)
    ang = jnp.stack([row[:, None] * inv, col[:, None] * inv], axis=1)
    return jnp.cos(ang), jnp.sin(ang)


def _apply_axial_rope(x, cos, sin):
    xr = x.reshape(x.shape[:-1] + (2, 2, ROPE_FREQS))
    x1, x2 = xr[..., 0, :], xr[..., 1, :]
    cs, sn = cos[None, :, None], sin[None, :, None]
    out = jnp.stack([x1 * cs - x2 * sn, x2 * cs + x1 * sn], axis=-2)
    return out.reshape(x.shape).astype(x.dtype)


def _split_in(z):
    a_end = ATT_WIDTH + 2 * KV_WIDTH
    ga_end = a_end + ATT_WIDTH
    mix_end = ga_end + SHIFT_WIDTH
    return z[..., :a_end], z[..., a_end:ga_end], z[..., ga_end:mix_end], z[..., mix_end:]


def _attn_qkv(z_att, q_norm, k_norm, rope):
    b, t = z_att.shape[:2]
    zq, zk, zv = jnp.split(z_att, [ATT_WIDTH, ATT_WIDTH + KV_WIDTH], axis=-1)
    q = _rms_norm(zq.reshape(b, t, N_Q_HEADS, HEAD_DIM), q_norm)
    k = _rms_norm(zk.reshape(b, t, N_KV_HEADS, HEAD_DIM), k_norm)
    v = zv.reshape(b, t, N_KV_HEADS, HEAD_DIM)
    if rope is not None:
        q = _apply_axial_rope(q, *rope)
        k = _apply_axial_rope(k, *rope)
    return q, k, v


def _block_attention(q, k, v):
    b, t = q.shape[:2]
    nblk = t // Q_BLOCK
    qb = q.reshape(b, nblk, Q_BLOCK, N_KV_HEADS, GQA_GROUP, HEAD_DIM).transpose(1, 0, 2, 3, 4, 5)
    scale = HEAD_DIM ** -0.5

    def one_block(qblk):
        sc = jnp.einsum('bqkgd,blkd->bkgql', qblk, k).astype(jnp.float32) * scale
        p = jax.nn.softmax(sc, axis=-1).astype(v.dtype)
        return jnp.einsum('bkgql,blkd->bqkgd', p, v)

    o = lax.map(one_block, qb)
    return o.transpose(1, 0, 2, 3, 4, 5).reshape(b, t, ATT_WIDTH)


def _token_shift(z, mu_prev, mu_next):
    zero = jnp.zeros_like(z[:, :1])
    z_prev = jnp.concatenate([zero, z[:, :-1]], axis=1)
    z_next = jnp.concatenate([z[:, 1:], zero], axis=1)
    return z + mu_prev * (z_prev - z) + mu_next * (z_next - z)


def _rwkv_inputs(z_mix, mu_prev, mu_next, w0, w_up, a0, a_up, k_k, k_a):
    b, t = z_mix.shape[:2]
    z_mix = _token_shift(z_mix, mu_prev, mu_next)
    offs = [RWKV_WIDTH, 2 * RWKV_WIDTH, 3 * RWKV_WIDTH, 3 * RWKV_WIDTH + W_LORA,
            3 * RWKV_WIDTH + 2 * W_LORA, 3 * RWKV_WIDTH + 2 * W_LORA + A_LORA]
    r, k, v, wl_f, wl_b, al_f, al_b = jnp.split(z_mix, offs, axis=-1)
    heads = lambda u: u.reshape(b, t, RWKV_HEADS, RWKV_HEAD)
    kk = heads(k * k_k).astype(jnp.float32)
    kk = kk / jnp.maximum(jnp.linalg.norm(kk, axis=-1, keepdims=True), 1e-12)
    dirs = []
    for d, (wl, al) in enumerate(((wl_f, al_f), (wl_b, al_b))):
        w_log = -jax.nn.softplus(-(w0[d] + jnp.tanh(wl) @ w_up[d])) - 0.5
        decay = jnp.exp(-jnp.exp(w_log.astype(jnp.float32)))
        a = jax.nn.sigmoid(a0[d] + al @ a_up[d])
        kd = k * (1.0 + (a - 1.0) * k_a)
        dirs.append((heads(decay), heads(kd), heads(a)))
    return heads(r), heads(v), kk, dirs


def _wkv_scan(state0, r, v, kk, decay, k, a, reverse):
    f32 = jnp.float32
    xs = tuple(jnp.moveaxis(u.astype(f32), 1, 0) for u in (r, decay, k, v, kk, kk * a))

    def step(S, inp):
        r_t, d_t, k_t, v_t, kk_t, b_t = inp
        sk = jnp.einsum('bhvk,bhk->bhv', S, kk_t)
        S = S * d_t[:, :, None, :] - sk[..., None] * b_t[:, :, None, :] + v_t[..., None] * k_t[:, :, None, :]
        return S, jnp.einsum('bhvk,bhk->bhv', S, r_t)

    s_fin, o = lax.scan(step, state0, xs, reverse=reverse)
    return s_fin, jnp.moveaxis(o, 0, 1)


def _rwkv_readout(o_f, o_b, r, v, dirs, r_k, gn_w, gn_b):
    b, t = o_f.shape[:2]
    o = o_f + o_b
    mu = jnp.mean(o, axis=-1, keepdims=True)
    var = jnp.mean(jnp.square(o - mu), axis=-1, keepdims=True)
    y = ((o - mu) * lax.rsqrt(var + GN_EPS)).reshape(b, t, RWKV_WIDTH) * gn_w + gn_b
    rf, vf = r.astype(jnp.float32), v.astype(jnp.float32)
    bonus = sum(jnp.sum(rf * dirs[d][1].astype(jnp.float32) * r_k[d], axis=-1, keepdims=True) * vf
                for d in range(2))
    return (y + bonus.reshape(b, t, RWKV_WIDTH)).astype(r.dtype)


def _merge(h, y_att, y_rwkv, g_att, g_rwkv, w_pa, w_pb, w_mg, b_mg, w_o):
    ya = (y_att * jax.nn.silu(g_att)) @ w_pa
    yb = (y_rwkv * jax.nn.silu(g_rwkv)) @ w_pb
    ga, gb = jnp.split(jax.nn.sigmoid(h @ w_mg + b_mg), 2, axis=-1)
    return (ga * ya + gb * yb) @ w_o


def setup_inputs(seed: int = 0) -> dict:
    key = jax.random.key(seed)
    ks = jax.random.split(key, 32)
    f32 = jnp.float32
    nrm = lambda k, shape, s: jax.random.normal(k, shape, f32) * s
    L = DEPTH
    return {
        'x': nrm(ks[0], (BATCH, SEQ, D_MODEL), 1.0),
        'c': nrm(ks[1], (BATCH, D_MODEL), 1.0),
        'ctx': nrm(ks[2], (BATCH, CTX_LEN, D_MODEL), 1.0),
        'c_ctx': nrm(ks[3], (D_MODEL,), 1.0),
        'w_ada': nrm(ks[4], (L, D_MODEL, 3 * D_MODEL), 0.5 * D_MODEL ** -0.5),
        'b_ada': nrm(ks[5], (L, 3 * D_MODEL), 0.02),
        'w_in': nrm(ks[6], (L, D_MODEL, IN_WIDTH), D_MODEL ** -0.5),
        'q_norm': 1.0 + nrm(ks[7], (L, HEAD_DIM), 0.02),
        'k_norm': 1.0 + nrm(ks[8], (L, HEAD_DIM), 0.02),
        'mu_prev': jax.random.uniform(ks[9], (L, SHIFT_WIDTH), f32, 0.0, 0.5),
        'mu_next': jax.random.uniform(ks[10], (L, SHIFT_WIDTH), f32, 0.0, 0.5),
        'w0': jax.random.uniform(ks[11], (L, 2, RWKV_WIDTH), f32, -6.0, -1.0),
        'w_up': nrm(ks[12], (L, 2, W_LORA, RWKV_WIDTH), 0.5 * W_LORA ** -0.5),
        'a0': nrm(ks[13], (L, 2, RWKV_WIDTH), 0.1),
        'a_up': nrm(ks[14], (L, 2, A_LORA, RWKV_WIDTH), 0.5 * A_LORA ** -0.5),
        'k_k': 0.85 + nrm(ks[15], (L, RWKV_WIDTH), 0.02),
        'k_a': 1.0 + nrm(ks[16], (L, RWKV_WIDTH), 0.02),
        'r_k': nrm(ks[17], (L, 2, RWKV_HEADS, RWKV_HEAD), 0.1),
        'gn_w': 1.0 + nrm(ks[18], (L, RWKV_WIDTH), 0.02),
        'gn_b': nrm(ks[19], (L, RWKV_WIDTH), 0.02),
        'w_pa': nrm(ks[20], (L, ATT_WIDTH, D_MODEL), BETA * ATT_WIDTH ** -0.5),
        'w_pb': nrm(ks[21], (L, RWKV_WIDTH, D_MODEL), BETA * RWKV_WIDTH ** -0.5),
        'w_mg': nrm(ks[22], (L, D_MODEL, 2 * D_MODEL), D_MODEL ** -0.5),
        'b_mg': nrm(ks[23], (L, 2 * D_MODEL), 0.02),
        'w_o': nrm(ks[24], (L, D_MODEL, D_MODEL), BETA * D_MODEL ** -0.5),
        'ln_g': 1.0 + nrm(ks[25], (L, D_MODEL), 0.02),
        'ln_b': nrm(ks[26], (L, D_MODEL), 0.02),
    }


def reference(x, c, ctx, c_ctx, w_ada, b_ada, w_in, q_norm, k_norm, mu_prev, mu_next, w0, w_up,
              a0, a_up, k_k, k_a, r_k, gn_w, gn_b, w_pa, w_pb, w_mg, b_mg, w_o, ln_g, ln_b):
    b, s, _ = x.shape
    rows = s // GRID_W
    rope = _axial_rope_tables(rows)
    zero_state = jnp.zeros((b, RWKV_HEADS, RWKV_HEAD, RWKV_HEAD), jnp.float32)
    for l in range(DEPTH):
        update_ctx = l < DEPTH - 1
        sh_x, sc_x, gt_x = jnp.split(jax.nn.silu(c) @ w_ada[l] + b_ada[l], 3, axis=-1)
        sh_c, sc_c, gt_c = jnp.split(jax.nn.silu(c_ctx) @ w_ada[l] + b_ada[l], 3, axis=-1)
        h_x = _layer_norm(x) * (1.0 + sc_x[:, None]) + sh_x[:, None]
        h_c = _layer_norm(ctx) * (1.0 + sc_c) + sh_c
        att_x, ga_x, mix_x, gb_x = _split_in(h_x @ w_in[l])
        att_c, ga_c, mix_c, gb_c = _split_in(h_c @ w_in[l])
        q_x, k_x, v_x = _attn_qkv(att_x, q_norm[l], k_norm[l], rope)
        q_c, k_c, v_c = _attn_qkv(att_c, q_norm[l], k_norm[l], None)
        y_att_x = _block_attention(q_x, jnp.concatenate([k_c, k_x], axis=1),
                                   jnp.concatenate([v_c, v_x], axis=1))
        rp = (mu_prev[l], mu_next[l], w0[l], w_up[l], a0[l], a_up[l], k_k[l], k_a[l])
        r_c, vr_c, kk_c, dirs_c = _rwkv_inputs(mix_c, *rp)
        r_x, vr_x, kk_x, dirs_x = _rwkv_inputs(mix_x, *rp)
        s_cf, o_cf = _wkv_scan(zero_state, r_c, vr_c, kk_c, *dirs_c[0], reverse=False)
        s_cb, o_cb = _wkv_scan(zero_state, r_c, vr_c, kk_c, *dirs_c[1], reverse=True)
        _, o_xf = _wkv_scan(s_cf, r_x, vr_x, kk_x, *dirs_x[0], reverse=False)
        _, o_xb = _wkv_scan(s_cb, r_x, vr_x, kk_x, *dirs_x[1], reverse=True)
        y_rwkv_x = _rwkv_readout(o_xf, o_xb, r_x, vr_x, dirs_x, r_k[l], gn_w[l], gn_b[l])
        out_x = _merge(h_x, y_att_x, y_rwkv_x, ga_x, gb_x, w_pa[l], w_pb[l], w_mg[l], b_mg[l], w_o[l])
        x_new = _layer_norm(ALPHA * x + gt_x[:, None] * out_x) * ln_g[l] + ln_b[l]
        if update_ctx:
            y_att_c = _block_attention(q_c, k_c, v_c)
            y_rwkv_c = _rwkv_readout(o_cf, o_cb, r_c, vr_c, dirs_c, r_k[l], gn_w[l], gn_b[l])
            out_c = _merge(h_c, y_att_c, y_rwkv_c, ga_c, gb_c, w_pa[l], w_pb[l], w_mg[l], b_mg[l], w_o[l])
            ctx = _layer_norm(ALPHA * ctx + gt_c * out_c) * ln_g[l] + ln_b[l]
        x = x_new
    return x
```

```python
import functools
import math

import jax
import jax.numpy as jnp
from jax import lax
from jax.experimental import pallas as pl
from jax.experimental.pallas import tpu as pltpu

F32 = jnp.float32
BF16 = jnp.bfloat16
HIGHEST = lax.Precision.HIGHEST

D_MODEL = 1024
GRID_W = 64
HEAD_DIM = 64
N_Q_HEADS = 8
N_KV_HEADS = 2
GQA_GROUP = N_Q_HEADS // N_KV_HEADS
ATT_WIDTH = N_Q_HEADS * HEAD_DIM
KV_WIDTH = N_KV_HEADS * HEAD_DIM
QKV_WIDTH = ATT_WIDTH + 2 * KV_WIDTH
ROPE_FREQS = HEAD_DIM // 4
ROPE_BASE = 10000.0
RWKV_HEAD = 64
RWKV_HEADS = 8
RWKV_WIDTH = RWKV_HEADS * RWKV_HEAD
W_LORA = 64
A_LORA = 64
SHIFT_WIDTH = 3 * RWKV_WIDTH + 2 * W_LORA + 2 * A_LORA
IN_WIDTH = 2 * ATT_WIDTH + 2 * KV_WIDTH + SHIFT_WIDTH + RWKV_WIDTH
NAT_WIDTH = IN_WIDTH - QKV_WIDTH
LN_EPS = 1e-5
QK_EPS = 1e-6
GN_EPS = 64e-5
DEPTH = 1
ALPHA = (2.0 * DEPTH) ** 0.25

CHUNK = 64
KEY_BLOCK = 256
V_ROWS = HEAD_DIM + 8
TOKEN_TILE = 256
Q_TILE = 512
VMEM_LIMIT = 48 * 1024 * 1024
NEG_BIG = -1e30


def _cparams(sem):
    return pltpu.CompilerParams(dimension_semantics=sem, vmem_limit_bytes=VMEM_LIMIT)


def _silu(x):
    return x * jax.nn.sigmoid(x)


def _layer_norm(x):
    mu = jnp.mean(x, axis=-1, keepdims=True)
    xc = x - mu
    var = jnp.mean(xc * xc, axis=-1, keepdims=True)
    return xc * lax.rsqrt(var + LN_EPS)


def _split_bf16(x, parts):
    out = []
    rem = x
    for _ in range(parts):
        p = rem.astype(BF16)
        out.append(p)
        rem = rem - p.astype(F32)
    return out


def _dot_split_lhs(x, m_bf16, parts):
    acc = None
    for p in _split_bf16(x, parts):
        t = jnp.dot(p, m_bf16, preferred_element_type=F32)
        acc = t if acc is None else acc + t
    return acc


def _dot_split_rhs(m_bf16, x, parts):
    acc = None
    for p in _split_bf16(x, parts):
        t = jnp.dot(m_bf16, p, preferred_element_type=F32)
        acc = t if acc is None else acc + t
    return acc


def _ada_kernel(c_ref, w_ref, b_ref, o_ref):
    s = _silu(c_ref[...])
    o_ref[...] = jnp.dot(s, w_ref[...], precision=HIGHEST, preferred_element_type=F32) + b_ref[...]


def _ada(cc, w_ada, b_ada):
    rows = cc.shape[0]
    nblk = (3 * D_MODEL) // D_MODEL
    return pl.pallas_call(
        _ada_kernel,
        grid=(nblk,),
        in_specs=[
            pl.BlockSpec((rows, D_MODEL), lambda j: (0, 0)),
            pl.BlockSpec((D_MODEL, D_MODEL), lambda j: (0, j)),
            pl.BlockSpec((1, D_MODEL), lambda j: (0, j)),
        ],
        out_specs=pl.BlockSpec((rows, D_MODEL), lambda j: (0, j)),
        out_shape=jax.ShapeDtypeStruct((rows, 3 * D_MODEL), F32),
        compiler_params=_cparams(("arbitrary",)),
        name="ada",
    )(cc, w_ada, b_ada)


def _rope_t(x, cos, sin):
    xs = jnp.concatenate([x[:, 16:32], x[:, 0:16], x[:, 48:64], x[:, 32:48]], axis=1)
    return x * cos + xs * sin


def _inproj_kernel(x_ref, mod_ref, cos_ref, sin_ref, wt_ref, wn_ref, qg_ref, kg_ref,
                   q_ref, k_ref, v_ref, ga_ref, mix_ref, gb_ref):
    tm = x_ref.shape[1]
    x = x_ref[0]
    shift = mod_ref[0, 0:1, :]
    scale = mod_ref[0, 1:2, :]
    h = _layer_norm(x) * (1.0 + scale) + shift
    hb = h.astype(BF16)
    zt = lax.dot_general(wt_ref[...], hb, (((1,), (1,)), ((), ())), preferred_element_type=F32)
    zn = jnp.dot(hb, wn_ref[...], preferred_element_type=F32)
    cos = cos_ref[...][None]
    sin = sin_ref[...][None]

    qz = zt[0:ATT_WIDTH].reshape(N_Q_HEADS, HEAD_DIM, tm)
    qn = qz * lax.rsqrt(jnp.mean(qz * qz, axis=1, keepdims=True) + QK_EPS) * qg_ref[...][None]
    qr = _rope_t(qn, cos, sin) * (HEAD_DIM ** -0.5 * math.log2(math.e))
    q_ref[0] = qr.astype(BF16)

    kz = zt[ATT_WIDTH:ATT_WIDTH + KV_WIDTH].reshape(N_KV_HEADS, HEAD_DIM, tm)
    kn = kz * lax.rsqrt(jnp.mean(kz * kz, axis=1, keepdims=True) + QK_EPS) * kg_ref[...][None]
    kr = _rope_t(kn, cos, sin)
    vz = zt[ATT_WIDTH + KV_WIDTH:QKV_WIDTH].reshape(N_KV_HEADS, HEAD_DIM, tm)
    pad_row = lax.broadcasted_iota(jnp.int32, (V_ROWS - HEAD_DIM, tm), 0)
    ones_pad = jnp.where(pad_row == 0, 1.0, 0.0).astype(BF16)
    for j in range(N_KV_HEADS):
        k_ref[0, j, 0] = kr[j].T.astype(BF16)
        v_ref[0, j, 0, 0:HEAD_DIM, :] = vz[j].astype(BF16)
        v_ref[0, j, 0, HEAD_DIM:V_ROWS, :] = ones_pad

    ga_ref[0] = _silu(zn[:, 0:ATT_WIDTH]).astype(BF16)
    mix_ref[0] = zn[:, ATT_WIDTH:ATT_WIDTH + SHIFT_WIDTH]
    gb_ref[0] = _silu(zn[:, ATT_WIDTH + SHIFT_WIDTH:NAT_WIDTH]).astype(BF16)


def _inproj(x, mod, cos_t, sin_t, w_t, w_n, q_gain, k_gain):
    b, t, _ = x.shape
    tm = TOKEN_TILE
    assert tm == KEY_BLOCK and t % tm == 0
    nt = t // tm
    const = lambda shape: pl.BlockSpec(shape, lambda bi, i: (0,) * len(shape))
    return pl.pallas_call(
        _inproj_kernel,
        grid=(b, nt),
        in_specs=[
            pl.BlockSpec((1, tm, D_MODEL), lambda bi, i: (bi, i, 0)),
            pl.BlockSpec((1, 3, D_MODEL), lambda bi, i: (bi, 0, 0)),
            pl.BlockSpec((HEAD_DIM, tm), lambda bi, i: (0, i)),
            pl.BlockSpec((HEAD_DIM, tm), lambda bi, i: (0, i)),
            const((QKV_WIDTH, D_MODEL)),
            const((D_MODEL, NAT_WIDTH)),
            const((HEAD_DIM, 1)),
            const((HEAD_DIM, 1)),
        ],
        out_specs=[
            pl.BlockSpec((1, N_Q_HEADS, HEAD_DIM, tm), lambda bi, i: (bi, 0, 0, i)),
            pl.BlockSpec((1, N_KV_HEADS, 1, tm, HEAD_DIM), lambda bi, i: (bi, 0, i, 0, 0)),
            pl.BlockSpec((1, N_KV_HEADS, 1, V_ROWS, tm), lambda bi, i: (bi, 0, i, 0, 0)),
            pl.BlockSpec((1, tm, ATT_WIDTH), lambda bi, i: (bi, i, 0)),
            pl.BlockSpec((1, tm, SHIFT_WIDTH), lambda bi, i: (bi, i, 0)),
            pl.BlockSpec((1, tm, RWKV_WIDTH), lambda bi, i: (bi, i, 0)),
        ],
        out_shape=[
            jax.ShapeDtypeStruct((b, N_Q_HEADS, HEAD_DIM, t), BF16),
            jax.ShapeDtypeStruct((b, N_KV_HEADS, nt, tm, HEAD_DIM), BF16),
            jax.ShapeDtypeStruct((b, N_KV_HEADS, nt, V_ROWS, tm), BF16),
            jax.ShapeDtypeStruct((b, t, ATT_WIDTH), BF16),
            jax.ShapeDtypeStruct((b, t, SHIFT_WIDTH), F32),
            jax.ShapeDtypeStruct((b, t, RWKV_WIDTH), BF16),
        ],
        compiler_params=_cparams(("parallel", "parallel")),
        name="inproj",
    )(x, mod, cos_t, sin_t, w_t, w_n, q_gain, k_gain)


def _attn_kernel(q_ref, k_ref, v_ref, o_ref, m_sc, acc_sc):
    nkb = k_ref.shape[2]
    m_sc[...] = jnp.full(m_sc.shape, NEG_BIG, F32)
    acc_sc[...] = jnp.zeros(acc_sc.shape, F32)

    def body(kb, carry):
        kblk = k_ref[0, 0, kb]
        vblk = v_ref[0, 0, kb]
        for g in range(GQA_GROUP):
            s = jnp.dot(kblk, q_ref[0, g], preferred_element_type=F32)
            m_old = m_sc[g]
            m_new = jnp.maximum(m_old, jnp.max(s, axis=0, keepdims=True))
            alpha = jnp.exp2(m_old - m_new)
            p = jnp.exp2(s - m_new).astype(BF16)
            acc_sc[g] = acc_sc[g] * alpha + jnp.dot(vblk, p, preferred_element_type=F32)
            m_sc[g] = m_new
        return carry

    lax.fori_loop(0, nkb, body, 0)
    for g in range(GQA_GROUP):
        acc = acc_sc[g]
        o = acc[0:HEAD_DIM] / acc[HEAD_DIM:HEAD_DIM + 1]
        o_ref[0, :, g * HEAD_DIM:(g + 1) * HEAD_DIM] = o.T


def _attention(q_t, k_blk, v_blk):
    b, _, _, t = q_t.shape
    nkb = k_blk.shape[2]
    tq = min(Q_TILE, t)
    assert t % tq == 0
    return pl.pallas_call(
        _attn_kernel,
        grid=(b, N_KV_HEADS, t // tq),
        in_specs=[
            pl.BlockSpec((1, GQA_GROUP, HEAD_DIM, tq), lambda bi, j, i: (bi, j, 0, i)),
            pl.BlockSpec((1, 1, nkb, KEY_BLOCK, HEAD_DIM), lambda bi, j, i: (bi, j, 0, 0, 0)),
            pl.BlockSpec((1, 1, nkb, V_ROWS, KEY_BLOCK), lambda bi, j, i: (bi, j, 0, 0, 0)),
        ],
        out_specs=pl.BlockSpec((1, tq, GQA_GROUP * HEAD_DIM), lambda bi, j, i: (bi, i, j)),
        out_shape=jax.ShapeDtypeStruct((b, t, ATT_WIDTH), F32),
        scratch_shapes=[
            pltpu.VMEM((GQA_GROUP, 1, tq), F32),
            pltpu.VMEM((GQA_GROUP, V_ROWS, tq), F32),
        ],
        compiler_params=_cparams(("parallel", "parallel", "parallel")),
        name="attn",
    )(q_t, k_blk, v_blk)


def _heads(x):
    return jnp.stack([x[:, h * RWKV_HEAD:(h + 1) * RWKV_HEAD] for h in range(RWKV_HEADS)], axis=0)


def _bmm(a, b):
    return jnp.einsum('hij,hjk->hik', a.astype(BF16), b.astype(BF16), preferred_element_type=F32)


def _bmm_nt(a, b):
    return jnp.einsum('hik,hjk->hij', a.astype(BF16), b.astype(BF16), preferred_element_type=F32)


def _softplus(y):
    return jnp.maximum(y, 0.0) + jnp.log(1.0 + jnp.exp(-jnp.abs(y)))


def _wkvprep_kernel(mix_ref, prev_ref, next_ref, mup_ref, mun_ref, w0_ref, wup_ref, a0_ref, aup_ref,
                    kk_ref, ka_ref, rk_ref, seg_ref,
                    bonus_ref, qeff_ref, o0_ref, g_ref, h_ref,
                    r_sc, v_sc, kn_sc, ld_sc, kd_sc, b_sc):
    tm = mix_ref.shape[1]
    i = pl.program_id(1)
    n = pl.num_programs(1)
    z = mix_ref[0]
    prev_row = jnp.where(i > 0, prev_ref[0, 7:8, :], 0.0)
    next_row = jnp.where(i < n - 1, next_ref[0, 0:1, :], 0.0)
    rows = lax.broadcasted_iota(jnp.int32, (tm, 1), 0)
    zp = jnp.where(rows == 0, prev_row, pltpu.roll(z, 1, 0))
    zn = jnp.where(rows == tm - 1, next_row, pltpu.roll(z, tm - 1, 0))
    zs = z + mup_ref[...] * (zp - z) + mun_ref[...] * (zn - z)

    w = RWKV_WIDTH
    r = zs[:, 0:w]
    k = zs[:, w:2 * w]
    v = zs[:, 2 * w:3 * w]
    seg = seg_ref[...]
    kkr = k * kk_ref[...]
    norm = jnp.sqrt(_dot_split_lhs(kkr * kkr, seg, 2))
    kn = kkr / jnp.maximum(norm, 1e-12)
    r_sc[...] = r
    v_sc[...] = v
    kn_sc[...] = kn
    coef = jnp.zeros((tm, w), F32)
    for d in range(2):
        wl = zs[:, 3 * w + d * W_LORA:3 * w + (d + 1) * W_LORA]
        al = zs[:, 3 * w + 2 * W_LORA + d * A_LORA:3 * w + 2 * W_LORA + (d + 1) * A_LORA]
        w_pre = w0_ref[d:d + 1, :] + jnp.dot(jnp.tanh(wl).astype(BF16), wup_ref[d], preferred_element_type=F32)
        w_log = -_softplus(-w_pre) - 0.5
        ld_sc[d] = -jnp.exp(w_log)
        a = jax.nn.sigmoid(a0_ref[d:d + 1, :] + jnp.dot(al.astype(BF16), aup_ref[d], preferred_element_type=F32))
        kd = k * (1.0 + (a - 1.0) * ka_ref[...])
        kd_sc[d] = kd
        b_sc[d] = kn * a
        coef = coef + _dot_split_lhs(r * kd * rk_ref[d:d + 1, :], seg, 2)
    bonus_ref[0] = coef * v

    ri = lax.broadcasted_iota(jnp.int32, (CHUNK, CHUNK), 0)
    ci = lax.broadcasted_iota(jnp.int32, (CHUNK, CHUNK), 1)
    eye = (ri == ci)[None]

    def chunk(c, carry):
        sl = pl.ds(pl.multiple_of(c * CHUNK, CHUNK), CHUNK)
        r_c = r_sc[sl, :]
        v_c = v_sc[sl, :]
        kn_c = kn_sc[sl, :]
        vh = _heads(v_c)
        for d in range(2):
            before = (ci < ri) if d == 0 else (ci > ri)
            strict = before[None]
            incl = (before | (ri == ci))[None]
            ld = ld_sc[d, sl, :]
            cum = _dot_split_rhs(jnp.where(before | (ri == ci), 1.0, 0.0).astype(BF16), ld, 3)
            tot = cum[CHUNK - 1:CHUNK, :] if d == 0 else cum[0:1, :]
            e_pos = jnp.exp(cum)
            e_prev = jnp.exp(cum - ld)
            e_neg = jnp.exp(-cum)
            e_rest = jnp.exp(tot - cum)
            kd_c = kd_sc[d, sl, :]
            b_c = b_sc[d, sl, :]
            rh = _heads(r_c * e_pos)
            kkh = _heads(kn_c * e_prev)
            kt = _heads(kd_c * e_neg)
            bt = _heads(b_c * e_neg)
            kv = _heads(kd_c * e_rest)
            bv = _heads(b_c * e_rest)
            gam = _heads(jnp.exp(tot))

            lb = jnp.where(strict, _bmm_nt(kkh, bt), 0.0)
            lk = jnp.where(strict, _bmm_nt(kkh, kt), 0.0)
            ark = jnp.where(incl, _bmm_nt(rh, kt), 0.0)
            arb = jnp.where(incl, _bmm_nt(rh, bt), 0.0)
            xp = -lb
            tinv = jnp.where(eye, 1.0, 0.0) + xp
            span = 2
            while span < CHUNK:
                xp = _bmm(xp, xp)
                tinv = tinv + _bmm(tinv, xp)
                span *= 2
            wmat = _bmm(tinv, kkh)
            u0 = _bmm(tinv, _bmm(lk, vh))
            qeff_ref[d, 0, :, sl, :] = rh - _bmm(arb, wmat)
            o0_ref[d, 0, :, sl, :] = _bmm(ark, vh) - _bmm(arb, u0)
            bvt = jnp.swapaxes(bv, 1, 2)
            kvt = jnp.swapaxes(kv, 1, 2)
            g_ref[d, 0, c] = jnp.where(eye, jnp.broadcast_to(gam, (RWKV_HEADS, CHUNK, RWKV_HEAD)), 0.0) - _bmm(bvt, wmat)
            h_ref[d, 0, c] = _bmm(kvt, vh) - _bmm(bvt, u0)
        return carry

    lax.fori_loop(0, tm // CHUNK, chunk, 0)


def _wkvprep(mix, mu_prev, mu_next, w0, w_up, a0, a_up, k_k, k_a, r_k, seg):
    b, t, _ = mix.shape
    tm = min(TOKEN_TILE, t)
    assert t % tm == 0 and tm % CHUNK == 0
    nt = t // tm
    nc = tm // CHUNK
    hb = tm // 8
    last8 = t // 8 - 1
    const = lambda shape: pl.BlockSpec(shape, lambda bi, i: (0,) * len(shape))
    hd = (2, 1, RWKV_HEADS, tm, RWKV_HEAD)
    st = (2, 1, nc, RWKV_HEADS, RWKV_HEAD, RWKV_HEAD)
    return pl.pallas_call(
        _wkvprep_kernel,
        grid=(b, nt),
        in_specs=[
            pl.BlockSpec((1, tm, SHIFT_WIDTH), lambda bi, i: (bi, i, 0)),
            pl.BlockSpec((1, 8, SHIFT_WIDTH), lambda bi, i: (bi, jnp.maximum(i * hb - 1, 0), 0)),
            pl.BlockSpec((1, 8, SHIFT_WIDTH), lambda bi, i: (bi, jnp.minimum((i + 1) * hb, last8), 0)),
            const((1, SHIFT_WIDTH)),
            const((1, SHIFT_WIDTH)),
            const((2, RWKV_WIDTH)),
            const((2, W_LORA, RWKV_WIDTH)),
            const((2, RWKV_WIDTH)),
            const((2, A_LORA, RWKV_WIDTH)),
            const((1, RWKV_WIDTH)),
            const((1, RWKV_WIDTH)),
            const((2, RWKV_WIDTH)),
            const((RWKV_WIDTH, RWKV_WIDTH)),
        ],
        out_specs=[
            pl.BlockSpec((1, tm, RWKV_WIDTH), lambda bi, i: (bi, i, 0)),
            pl.BlockSpec(hd, lambda bi, i: (0, bi, 0, i, 0)),
            pl.BlockSpec(hd, lambda bi, i: (0, bi, 0, i, 0)),
            pl.BlockSpec(st, lambda bi, i: (0, bi, i, 0, 0, 0)),
            pl.BlockSpec(st, lambda bi, i: (0, bi, i, 0, 0, 0)),
        ],
        out_shape=[
            jax.ShapeDtypeStruct((b, t, RWKV_WIDTH), F32),
            jax.ShapeDtypeStruct((2, b, RWKV_HEADS, t, RWKV_HEAD), F32),
            jax.ShapeDtypeStruct((2, b, RWKV_HEADS, t, RWKV_HEAD), F32),
            jax.ShapeDtypeStruct((2, b, t // CHUNK, RWKV_HEADS, RWKV_HEAD, RWKV_HEAD), F32),
            jax.ShapeDtypeStruct((2, b, t // CHUNK, RWKV_HEADS, RWKV_HEAD, RWKV_HEAD), F32),
        ],
        scratch_shapes=[
            pltpu.VMEM((tm, RWKV_WIDTH), F32),
            pltpu.VMEM((tm, RWKV_WIDTH), F32),
            pltpu.VMEM((tm, RWKV_WIDTH), F32),
            pltpu.VMEM((2, tm, RWKV_WIDTH), F32),
            pltpu.VMEM((2, tm, RWKV_WIDTH), F32),
            pltpu.VMEM((2, tm, RWKV_WIDTH), F32),
        ],
        compiler_params=_cparams(("parallel", "parallel")),
        name="wkvprep",
    )(mix, mix, mix, mu_prev, mu_next, w0, w_up, a0, a_up, k_k, k_a, r_k, seg)


def _wkvscan_kernel(qeff_ref, o0_ref, g_ref, h_ref, s0_ref, o_ref, sfin_ref, s_sc):
    d = pl.program_id(0)
    c = pl.program_id(2)
    nc = g_ref.shape[2]

    @pl.when(c == 0)
    def _():
        s_sc[...] = s0_ref[0, 0]

    for j in range(nc):
        jj = j + d * (nc - 1 - 2 * j)
        sl = pl.ds(pl.multiple_of(jj * CHUNK, CHUNK), CHUNK)
        s = s_sc[...]
        o = jnp.einsum('hij,hjk->hik', qeff_ref[0, 0, :, sl, :], s, precision=HIGHEST,
                       preferred_element_type=F32) + o0_ref[0, 0, :, sl, :]
        s_sc[...] = jnp.einsum('hij,hjk->hik', g_ref[0, 0, jj], s, precision=HIGHEST,
                               preferred_element_type=F32) + h_ref[0, 0, jj]
        for hh in range(RWKV_HEADS):
            o_ref[0, 0, sl, hh * RWKV_HEAD:(hh + 1) * RWKV_HEAD] = o[hh]

    @pl.when(c == pl.num_programs(2) - 1)
    def _():
        sfin_ref[0, 0] = s_sc[...]


def _wkvscan(qeff, o0, g, h, s0):
    _, b, _, t, _ = qeff.shape
    tc = min(TOKEN_TILE, t)
    nc = tc // CHUNK
    nt = t // tc
    visit = lambda d, c: c + d * (nt - 1 - 2 * c)
    hd = (1, 1, RWKV_HEADS, tc, RWKV_HEAD)
    st = (1, 1, nc, RWKV_HEADS, RWKV_HEAD, RWKV_HEAD)
    s_spec = pl.BlockSpec((1, 1, RWKV_HEADS, RWKV_HEAD, RWKV_HEAD), lambda d, bi, c: (d, bi, 0, 0, 0))
    return pl.pallas_call(
        _wkvscan_kernel,
        grid=(2, b, nt),
        in_specs=[
            pl.BlockSpec(hd, lambda d, bi, c: (d, bi, 0, visit(d, c), 0)),
            pl.BlockSpec(hd, lambda d, bi, c: (d, bi, 0, visit(d, c), 0)),
            pl.BlockSpec(st, lambda d, bi, c: (d, bi, visit(d, c), 0, 0, 0)),
            pl.BlockSpec(st, lambda d, bi, c: (d, bi, visit(d, c), 0, 0, 0)),
            s_spec,
        ],
        out_specs=[
            pl.BlockSpec((1, 1, tc, RWKV_WIDTH), lambda d, bi, c: (d, bi, visit(d, c), 0)),
            s_spec,
        ],
        out_shape=[
            jax.ShapeDtypeStruct((2, b, t, RWKV_WIDTH), F32),
            jax.ShapeDtypeStruct((2, b, RWKV_HEADS, RWKV_HEAD, RWKV_HEAD), F32),
        ],
        scratch_shapes=[pltpu.VMEM((RWKV_HEADS, RWKV_HEAD, RWKV_HEAD), F32)],
        compiler_params=_cparams(("arbitrary", "arbitrary", "arbitrary")),
        name="wkvscan",
    )(qeff, o0, g, h, s0)


def _merge_kernel(x_ref, mod_ref, ya_ref, ga_ref, o_ref, bonus_ref, gb_ref,
                  wpa_ref, wpb_ref, wmg_ref, bmg_ref, wo_ref, gnw_ref, gnb_ref, lng_ref, lnb_ref, seg_ref,
                  out_ref):
    x = x_ref[0]
    shift = mod_ref[0, 0:1, :]
    scale = mod_ref[0, 1:2, :]
    gate = mod_ref[0, 2:3, :]
    h = _layer_norm(x) * (1.0 + scale) + shift
    gates = jax.nn.sigmoid(jnp.dot(h.astype(BF16), wmg_ref[...], preferred_element_type=F32) + bmg_ref[...])
    ya = jnp.dot((ya_ref[0] * ga_ref[0].astype(F32)).astype(BF16), wpa_ref[...], preferred_element_type=F32)

    o = o_ref[0, 0] + o_ref[1, 0]
    seg = seg_ref[...]
    mu = _dot_split_lhs(o, seg, 2) * (1.0 / RWKV_HEAD)
    oc = o - mu
    var = _dot_split_lhs(oc * oc, seg, 2) * (1.0 / RWKV_HEAD)
    y = oc * lax.rsqrt(var + GN_EPS) * gnw_ref[...] + gnb_ref[...] + bonus_ref[0]
    yb = jnp.dot((y * gb_ref[0].astype(F32)).astype(BF16), wpb_ref[...], preferred_element_type=F32)

    mixed = gates[:, 0:D_MODEL] * ya + gates[:, D_MODEL:2 * D_MODEL] * yb
    out = jnp.dot(mixed.astype(BF16), wo_ref[...], preferred_element_type=F32)
    out_ref[0] = _layer_norm(ALPHA * x + gate * out) * lng_ref[...] + lnb_ref[...]


def _merge(x, mod, y_att, ga, o, bonus, gb, w_pa, w_pb, w_mg, b_mg, w_o, gn_w, gn_b, ln_g, ln_b, seg):
    b, t, _ = x.shape
    tm = min(TOKEN_TILE, t)
    const = lambda shape: pl.BlockSpec(shape, lambda bi, i: (0,) * len(shape))
    tok = lambda width: pl.BlockSpec((1, tm, width), lambda bi, i: (bi, i, 0))
    return pl.pallas_call(
        _merge_kernel,
        grid=(b, t // tm),
        in_specs=[
            tok(D_MODEL),
            pl.BlockSpec((1, 3, D_MODEL), lambda bi, i: (bi, 0, 0)),
            tok(ATT_WIDTH),
            tok(ATT_WIDTH),
            pl.BlockSpec((2, 1, tm, RWKV_WIDTH), lambda bi, i: (0, bi, i, 0)),
            tok(RWKV_WIDTH),
            tok(RWKV_WIDTH),
            const((ATT_WIDTH, D_MODEL)),
            const((RWKV_WIDTH, D_MODEL)),
            const((D_MODEL, 2 * D_MODEL)),
            const((1, 2 * D_MODEL)),
            const((D_MODEL, D_MODEL)),
            const((1, RWKV_WIDTH)),
            const((1, RWKV_WIDTH)),
            const((1, D_MODEL)),
            const((1, D_MODEL)),
            const((RWKV_WIDTH, RWKV_WIDTH)),
        ],
        out_specs=tok(D_MODEL),
        out_shape=jax.ShapeDtypeStruct((b, t, D_MODEL), F32),
        compiler_params=_cparams(("parallel", "parallel")),
        name="merge",
    )(x, mod, y_att, ga, o, bonus, gb, w_pa, w_pb, w_mg, b_mg, w_o, gn_w, gn_b, ln_g, ln_b, seg)


def _rope_tables_t(t):
    pos = jnp.arange(t, dtype=jnp.int32)
    row = (pos // GRID_W).astype(F32)
    col = (pos % GRID_W).astype(F32)
    inv = ROPE_BASE ** (-jnp.arange(ROPE_FREQS, dtype=F32) / ROPE_FREQS)
    ang_r = inv[:, None] * row[None, :]
    ang_c = inv[:, None] * col[None, :]
    cos_t = jnp.concatenate([jnp.cos(ang_r), jnp.cos(ang_r), jnp.cos(ang_c), jnp.cos(ang_c)], axis=0)
    sin_t = jnp.concatenate([-jnp.sin(ang_r), jnp.sin(ang_r), -jnp.sin(ang_c), jnp.sin(ang_c)], axis=0)
    return cos_t, sin_t


def kernel(x, c, ctx, c_ctx, w_ada, b_ada, w_in, q_norm, k_norm, mu_prev, mu_next, w0, w_up, a0, a_up,
           k_k, k_a, r_k, gn_w, gn_b, w_pa, w_pb, w_mg, b_mg, w_o, ln_g, ln_b):
    b, t, _ = x.shape
    tc = ctx.shape[1]
    l = 0
    seg = jnp.kron(jnp.eye(RWKV_HEADS, dtype=F32), jnp.ones((RWKV_HEAD, RWKV_HEAD), F32)).astype(BF16)

    rows = 8 * ((b + 1 + 7) // 8)
    cc = jnp.zeros((rows, D_MODEL), F32).at[0:b].set(c).at[b].set(c_ctx)
    ada = _ada(cc, w_ada[l], b_ada[l][None, :])
    mod_x = ada[0:b].reshape(b, 3, D_MODEL)
    mod_c = jnp.broadcast_to(ada[b].reshape(1, 3, D_MODEL), (b, 3, D_MODEL))

    w_t = w_in[l][:, 0:QKV_WIDTH].T.astype(BF16)
    w_n = w_in[l][:, QKV_WIDTH:].astype(BF16)
    q_gain = q_norm[l][:, None]
    k_gain = k_norm[l][:, None]
    cos_x, sin_x = _rope_tables_t(t)
    cos_c = jnp.ones((HEAD_DIM, tc), F32)
    sin_c = jnp.zeros((HEAD_DIM, tc), F32)

    q_x, k_x, v_x, ga_x, mix_x, gb_x = _inproj(x, mod_x, cos_x, sin_x, w_t, w_n, q_gain, k_gain)
    _, k_c, v_c, _, mix_c, _ = _inproj(ctx, mod_c, cos_c, sin_c, w_t, w_n, q_gain, k_gain)

    y_att = _attention(q_x, jnp.concatenate([k_c, k_x], axis=2), jnp.concatenate([v_c, v_x], axis=2))

    prep_args = (mu_prev[l][None, :], mu_next[l][None, :], w0[l], w_up[l].astype(BF16), a0[l], a_up[l].astype(BF16),
                 k_k[l][None, :], k_a[l][None, :], r_k[l].reshape(2, RWKV_WIDTH), seg)
    _, qe_c, o0_c, g_c, h_c = _wkvprep(mix_c, *prep_args)
    zero_state = jnp.zeros((2, b, RWKV_HEADS, RWKV_HEAD, RWKV_HEAD), F32)
    _, s_ctx = _wkvscan(qe_c, o0_c, g_c, h_c, zero_state)
    bonus, qe_x, o0_x, g_x, h_x = _wkvprep(mix_x, *prep_args)
    o_x, _ = _wkvscan(qe_x, o0_x, g_x, h_x, s_ctx)

    return _merge(x, mod_x, y_att, ga_x, o_x, bonus, gb_x,
                  w_pa[l].astype(BF16), w_pb[l].astype(BF16), w_mg[l].astype(BF16), b_mg[l][None, :],
                  w_o[l].astype(BF16), gn_w[l][None, :], gn_b[l][None, :], ln_g[l][None, :], ln_b[l][None, :], seg)
```

```python
import functools
import math

import jax
import jax.numpy as jnp
from jax import lax
from jax.experimental import pallas as pl
from jax.experimental.pallas import tpu as pltpu

F32 = jnp.float32
BF16 = jnp.bfloat16
HIGHEST = lax.Precision.HIGHEST

D_MODEL = 1024
GRID_W = 64
HEAD_DIM = 64
N_Q_HEADS = 8
N_KV_HEADS = 2
GQA_GROUP = N_Q_HEADS // N_KV_HEADS
ATT_WIDTH = N_Q_HEADS * HEAD_DIM
KV_WIDTH = N_KV_HEADS * HEAD_DIM
QKV_WIDTH = ATT_WIDTH + 2 * KV_WIDTH
ROPE_FREQS = HEAD_DIM // 4
ROPE_BASE = 10000.0
RWKV_HEAD = 64
RWKV_HEADS = 8
RWKV_WIDTH = RWKV_HEADS * RWKV_HEAD
W_LORA = 64
A_LORA = 64
SHIFT_WIDTH = 3 * RWKV_WIDTH + 2 * W_LORA + 2 * A_LORA
IN_WIDTH = 2 * ATT_WIDTH + 2 * KV_WIDTH + SHIFT_WIDTH + RWKV_WIDTH
NAT_WIDTH = IN_WIDTH - QKV_WIDTH
LN_EPS = 1e-5
QK_EPS = 1e-6
GN_EPS = 64e-5
DEPTH = 1
ALPHA = (2.0 * DEPTH) ** 0.25

CHUNK = 64
KEY_BLOCK = 256
V_ROWS = HEAD_DIM + 8
TOKEN_TILE = 256
Q_TILE = 512
ATTN_COLS = 256
VMEM_LIMIT = 48 * 1024 * 1024
NEG_BIG = -1e30


def _cparams(sem):
    return pltpu.CompilerParams(dimension_semantics=sem, vmem_limit_bytes=VMEM_LIMIT)


def _silu(x):
    return x * jax.nn.sigmoid(x)


def _layer_norm(x):
    mu = jnp.mean(x, axis=-1, keepdims=True)
    xc = x - mu
    var = jnp.mean(xc * xc, axis=-1, keepdims=True)
    return xc * lax.rsqrt(var + LN_EPS)


def _split_bf16(x, parts):
    out = []
    rem = x
    for _ in range(parts):
        p = rem.astype(BF16)
        out.append(p)
        rem = rem - p.astype(F32)
    return out


def _dot_split_lhs(x, m_bf16, parts):
    acc = None
    for p in _split_bf16(x, parts):
        t = jnp.dot(p, m_bf16, preferred_element_type=F32)
        acc = t if acc is None else acc + t
    return acc


def _dot_split_rhs(m_bf16, x, parts):
    acc = None
    for p in _split_bf16(x, parts):
        t = jnp.dot(m_bf16, p, preferred_element_type=F32)
        acc = t if acc is None else acc + t
    return acc


def _ada_kernel(c_ref, w_ref, b_ref, o_ref):
    s = _silu(c_ref[...])
    o_ref[...] = jnp.dot(s, w_ref[...], precision=HIGHEST, preferred_element_type=F32) + b_ref[...]


def _ada(cc, w_ada, b_ada):
    rows = cc.shape[0]
    nblk = (3 * D_MODEL) // D_MODEL
    return pl.pallas_call(
        _ada_kernel,
        grid=(nblk,),
        in_specs=[
            pl.BlockSpec((rows, D_MODEL), lambda j: (0, 0)),
            pl.BlockSpec((D_MODEL, D_MODEL), lambda j: (0, j)),
            pl.BlockSpec((1, D_MODEL), lambda j: (0, j)),
        ],
        out_specs=pl.BlockSpec((rows, D_MODEL), lambda j: (0, j)),
        out_shape=jax.ShapeDtypeStruct((rows, 3 * D_MODEL), F32),
        compiler_params=_cparams(("arbitrary",)),
        name="ada",
    )(cc, w_ada, b_ada)


def _rope_t(x, cos, sin):
    xs = jnp.concatenate([x[:, 16:32], x[:, 0:16], x[:, 48:64], x[:, 32:48]], axis=1)
    return x * cos + xs * sin


def _inproj_kernel(x_ref, mod_ref, cos_ref, sin_ref, wt_ref, wn_ref, qg_ref, kg_ref,
                   q_ref, k_ref, v_ref, ga_ref, mix_ref, gb_ref):
    tm = x_ref.shape[1]
    x = x_ref[0]
    shift = mod_ref[0, 0:1, :]
    scale = mod_ref[0, 1:2, :]
    h = _layer_norm(x) * (1.0 + scale) + shift
    hb = h.astype(BF16)
    zt = lax.dot_general(wt_ref[...], hb, (((1,), (1,)), ((), ())), preferred_element_type=F32)
    zn = jnp.dot(hb, wn_ref[...], preferred_element_type=F32)
    cos = cos_ref[...][None]
    sin = sin_ref[...][None]

    qz = zt[0:ATT_WIDTH].reshape(N_Q_HEADS, HEAD_DIM, tm)
    qn = qz * lax.rsqrt(jnp.mean(qz * qz, axis=1, keepdims=True) + QK_EPS) * qg_ref[...][None]
    qr = _rope_t(qn, cos, sin) * (HEAD_DIM ** -0.5 * math.log2(math.e))
    q_ref[0] = qr.astype(BF16)

    kz = zt[ATT_WIDTH:ATT_WIDTH + KV_WIDTH].reshape(N_KV_HEADS, HEAD_DIM, tm)
    kn = kz * lax.rsqrt(jnp.mean(kz * kz, axis=1, keepdims=True) + QK_EPS) * kg_ref[...][None]
    kr = _rope_t(kn, cos, sin)
    vz = zt[ATT_WIDTH + KV_WIDTH:QKV_WIDTH].reshape(N_KV_HEADS, HEAD_DIM, tm)
    pad_row = lax.broadcasted_iota(jnp.int32, (V_ROWS - HEAD_DIM, tm), 0)
    ones_pad = jnp.where(pad_row == 0, 1.0, 0.0).astype(BF16)
    for j in range(N_KV_HEADS):
        k_ref[0, j, 0] = kr[j].T.astype(BF16)
        v_ref[0, j, 0, 0:HEAD_DIM, :] = vz[j].astype(BF16)
        v_ref[0, j, 0, HEAD_DIM:V_ROWS, :] = ones_pad

    ga_ref[0] = _silu(zn[:, 0:ATT_WIDTH]).astype(BF16)
    mix_ref[0] = zn[:, ATT_WIDTH:ATT_WIDTH + SHIFT_WIDTH]
    gb_ref[0] = _silu(zn[:, ATT_WIDTH + SHIFT_WIDTH:NAT_WIDTH]).astype(BF16)


def _inproj(x, mod, cos_t, sin_t, w_t, w_n, q_gain, k_gain):
    b, t, _ = x.shape
    tm = TOKEN_TILE
    assert tm == KEY_BLOCK and t % tm == 0
    nt = t // tm
    const = lambda shape: pl.BlockSpec(shape, lambda bi, i: (0,) * len(shape))
    return pl.pallas_call(
        _inproj_kernel,
        grid=(b, nt),
        in_specs=[
            pl.BlockSpec((1, tm, D_MODEL), lambda bi, i: (bi, i, 0)),
            pl.BlockSpec((1, 3, D_MODEL), lambda bi, i: (bi, 0, 0)),
            pl.BlockSpec((HEAD_DIM, tm), lambda bi, i: (0, i)),
            pl.BlockSpec((HEAD_DIM, tm), lambda bi, i: (0, i)),
            const((QKV_WIDTH, D_MODEL)),
            const((D_MODEL, NAT_WIDTH)),
            const((HEAD_DIM, 1)),
            const((HEAD_DIM, 1)),
        ],
        out_specs=[
            pl.BlockSpec((1, N_Q_HEADS, HEAD_DIM, tm), lambda bi, i: (bi, 0, 0, i)),
            pl.BlockSpec((1, N_KV_HEADS, 1, tm, HEAD_DIM), lambda bi, i: (bi, 0, i, 0, 0)),
            pl.BlockSpec((1, N_KV_HEADS, 1, V_ROWS, tm), lambda bi, i: (bi, 0, i, 0, 0)),
            pl.BlockSpec((1, tm, ATT_WIDTH), lambda bi, i: (bi, i, 0)),
            pl.BlockSpec((1, tm, SHIFT_WIDTH), lambda bi, i: (bi, i, 0)),
            pl.BlockSpec((1, tm, RWKV_WIDTH), lambda bi, i: (bi, i, 0)),
        ],
        out_shape=[
            jax.ShapeDtypeStruct((b, N_Q_HEADS, HEAD_DIM, t), BF16),
            jax.ShapeDtypeStruct((b, N_KV_HEADS, nt, tm, HEAD_DIM), BF16),
            jax.ShapeDtypeStruct((b, N_KV_HEADS, nt, V_ROWS, tm), BF16),
            jax.ShapeDtypeStruct((b, t, ATT_WIDTH), BF16),
            jax.ShapeDtypeStruct((b, t, SHIFT_WIDTH), F32),
            jax.ShapeDtypeStruct((b, t, RWKV_WIDTH), BF16),
        ],
        compiler_params=_cparams(("parallel", "parallel")),
        name="inproj",
    )(x, mod, cos_t, sin_t, w_t, w_n, q_gain, k_gain)


def _attn_kernel(q_ref, k_ref, v_ref, o_ref, s_sc, p_sc, cmax_sc, alpha_sc, m_sc, acc_sc):
    nkb = k_ref.shape[2]
    tq = q_ref.shape[3]
    width = GQA_GROUP * tq
    nchunk = width // ATTN_COLS
    per_head = tq // ATTN_COLS
    m_sc[...] = jnp.full(m_sc.shape, NEG_BIG, F32)
    acc_sc[...] = jnp.zeros(acc_sc.shape, F32)

    def tick(t, par, a=True, b=True, c=True):
        if b:
            m_old = m_sc[...]
            m_new = jnp.maximum(m_old, cmax_sc[1 - par])
            alpha_sc[1 - par] = jnp.exp2(m_old - m_new)
            m_sc[...] = m_new
        if a:
            kblk = k_ref[0, 0, t]
        if c:
            vblk = v_ref[0, 0, t - 2]
        for ch in range(nchunk):
            cols = slice(ch * ATTN_COLS, (ch + 1) * ATTN_COLS)
            if c:
                acc_sc[:, cols] = acc_sc[:, cols] * alpha_sc[par, :, cols] + jnp.dot(
                    vblk, p_sc[par, :, cols], preferred_element_type=F32)
            if b:
                p_sc[1 - par, :, cols] = jnp.exp2(s_sc[1 - par, :, cols] - m_new[:, cols]).astype(BF16)
            if a:
                g, h = divmod(ch, per_head)
                s = jnp.dot(kblk, q_ref[0, g, :, h * ATTN_COLS:(h + 1) * ATTN_COLS], preferred_element_type=F32)
                s_sc[par, :, cols] = s
                cmax_sc[par, :, cols] = jnp.max(s, axis=0, keepdims=True)

    assert nkb % 2 == 1 and nkb >= 3
    tick(0, 0, b=False, c=False)
    tick(1, 1, c=False)

    def pair(j, carry):
        tick(2 * j, 0)
        tick(2 * j + 1, 1)
        return carry

    lax.fori_loop(1, (nkb - 1) // 2, pair, 0)
    tick(nkb - 1, 0)
    tick(nkb, 1, a=False)
    tick(nkb + 1, 0, a=False, b=False)
    for g in range(GQA_GROUP):
        acc = acc_sc[:, g * tq:(g + 1) * tq]
        o = acc[0:HEAD_DIM] / acc[HEAD_DIM:HEAD_DIM + 1]
        o_ref[0, :, g * HEAD_DIM:(g + 1) * HEAD_DIM] = o.T


def _attention(q_t, k_blk, v_blk):
    b, _, _, t = q_t.shape
    nkb = k_blk.shape[2]
    tq = min(Q_TILE, t)
    assert t % tq == 0
    width = GQA_GROUP * tq
    return pl.pallas_call(
        _attn_kernel,
        grid=(b, N_KV_HEADS, t // tq),
        in_specs=[
            pl.BlockSpec((1, GQA_GROUP, HEAD_DIM, tq), lambda bi, j, i: (bi, j, 0, i)),
            pl.BlockSpec((1, 1, nkb, KEY_BLOCK, HEAD_DIM), lambda bi, j, i: (bi, j, 0, 0, 0)),
            pl.BlockSpec((1, 1, nkb, V_ROWS, KEY_BLOCK), lambda bi, j, i: (bi, j, 0, 0, 0)),
        ],
        out_specs=pl.BlockSpec((1, tq, GQA_GROUP * HEAD_DIM), lambda bi, j, i: (bi, i, j)),
        out_shape=jax.ShapeDtypeStruct((b, t, ATT_WIDTH), F32),
        scratch_shapes=[
            pltpu.VMEM((2, KEY_BLOCK, width), F32),
            pltpu.VMEM((2, KEY_BLOCK, width), BF16),
            pltpu.VMEM((2, 1, width), F32),
            pltpu.VMEM((2, 1, width), F32),
            pltpu.VMEM((1, width), F32),
            pltpu.VMEM((V_ROWS, width), F32),
        ],
        compiler_params=_cparams(("parallel", "parallel", "parallel")),
        name="attn",
    )(q_t, k_blk, v_blk)


def _heads(x):
    return jnp.stack([x[:, h * RWKV_HEAD:(h + 1) * RWKV_HEAD] for h in range(RWKV_HEADS)], axis=0)


def _bmm(a, b):
    return jnp.einsum('hij,hjk->hik', a.astype(BF16), b.astype(BF16), preferred_element_type=F32)


def _bmm_nt(a, b):
    return jnp.einsum('hik,hjk->hij', a.astype(BF16), b.astype(BF16), preferred_element_type=F32)


def _softplus(y):
    return jnp.maximum(y, 0.0) + jnp.log(1.0 + jnp.exp(-jnp.abs(y)))


def _wkvprep_kernel(mix_ref, prev_ref, next_ref, mup_ref, mun_ref, w0_ref, wup_ref, a0_ref, aup_ref,
                    kk_ref, ka_ref, rk_ref, seg_ref,
                    bonus_ref, qeff_ref, o0_ref, g_ref, h_ref,
                    r_sc, v_sc, kn_sc, ld_sc, kd_sc, b_sc):
    tm = mix_ref.shape[1]
    i = pl.program_id(1)
    n = pl.num_programs(1)
    z = mix_ref[0]
    prev_row = jnp.where(i > 0, prev_ref[0, 7:8, :], 0.0)
    next_row = jnp.where(i < n - 1, next_ref[0, 0:1, :], 0.0)
    rows = lax.broadcasted_iota(jnp.int32, (tm, 1), 0)
    zp = jnp.where(rows == 0, prev_row, pltpu.roll(z, 1, 0))
    zn = jnp.where(rows == tm - 1, next_row, pltpu.roll(z, tm - 1, 0))
    zs = z + mup_ref[...] * (zp - z) + mun_ref[...] * (zn - z)

    w = RWKV_WIDTH
    r = zs[:, 0:w]
    k = zs[:, w:2 * w]
    v = zs[:, 2 * w:3 * w]
    seg = seg_ref[...]
    kkr = k * kk_ref[...]
    norm = jnp.sqrt(_dot_split_lhs(kkr * kkr, seg, 2))
    kn = kkr / jnp.maximum(norm, 1e-12)
    r_sc[...] = r
    v_sc[...] = v
    kn_sc[...] = kn
    coef = jnp.zeros((tm, w), F32)
    for d in range(2):
        wl = zs[:, 3 * w + d * W_LORA:3 * w + (d + 1) * W_LORA]
        al = zs[:, 3 * w + 2 * W_LORA + d * A_LORA:3 * w + 2 * W_LORA + (d + 1) * A_LORA]
        w_pre = w0_ref[d:d + 1, :] + jnp.dot(jnp.tanh(wl).astype(BF16), wup_ref[d], preferred_element_type=F32)
        w_log = -_softplus(-w_pre) - 0.5
        ld_sc[d] = -jnp.exp(w_log)
        a = jax.nn.sigmoid(a0_ref[d:d + 1, :] + jnp.dot(al.astype(BF16), aup_ref[d], preferred_element_type=F32))
        kd = k * (1.0 + (a - 1.0) * ka_ref[...])
        kd_sc[d] = kd
        b_sc[d] = kn * a
        coef = coef + _dot_split_lhs(r * kd * rk_ref[d:d + 1, :], seg, 2)
    bonus_ref[0] = coef * v

    ri = lax.broadcasted_iota(jnp.int32, (CHUNK, CHUNK), 0)
    ci = lax.broadcasted_iota(jnp.int32, (CHUNK, CHUNK), 1)
    eye = (ri == ci)[None]

    def chunk(c, carry):
        sl = pl.ds(pl.multiple_of(c * CHUNK, CHUNK), CHUNK)
        r_c = r_sc[sl, :]
        v_c = v_sc[sl, :]
        kn_c = kn_sc[sl, :]
        vh = _heads(v_c)
        for d in range(2):
            before = (ci < ri) if d == 0 else (ci > ri)
            strict = before[None]
            incl = (before | (ri == ci))[None]
            ld = ld_sc[d, sl, :]
            cum = _dot_split_rhs(jnp.where(before | (ri == ci), 1.0, 0.0).astype(BF16), ld, 3)
            tot = cum[CHUNK - 1:CHUNK, :] if d == 0 else cum[0:1, :]
            e_pos = jnp.exp(cum)
            e_prev = jnp.exp(cum - ld)
            e_neg = jnp.exp(-cum)
            e_rest = jnp.exp(tot - cum)
            kd_c = kd_sc[d, sl, :]
            b_c = b_sc[d, sl, :]
            rh = _heads(r_c * e_pos)
            kkh = _heads(kn_c * e_prev)
            kt = _heads(kd_c * e_neg)
            bt = _heads(b_c * e_neg)
            kv = _heads(kd_c * e_rest)
            bv = _heads(b_c * e_rest)
            gam = _heads(jnp.exp(tot))

            lb = jnp.where(strict, _bmm_nt(kkh, bt), 0.0)
            lk = jnp.where(strict, _bmm_nt(kkh, kt), 0.0)
            ark = jnp.where(incl, _bmm_nt(rh, kt), 0.0)
            arb = jnp.where(incl, _bmm_nt(rh, bt), 0.0)
            xp = -lb
            tinv = jnp.where(eye, 1.0, 0.0) + xp
            span = 2
            while span < CHUNK:
                xp = _bmm(xp, xp)
                tinv = tinv + _bmm(tinv, xp)
                span *= 2
            wmat = _bmm(tinv, kkh)
            u0 = _bmm(tinv, _bmm(lk, vh))
            qeff_ref[d, 0, :, sl, :] = rh - _bmm(arb, wmat)
            o0_ref[d, 0, :, sl, :] = _bmm(ark, vh) - _bmm(arb, u0)
            bvt = jnp.swapaxes(bv, 1, 2)
            kvt = jnp.swapaxes(kv, 1, 2)
            g_ref[d, 0, c] = jnp.where(eye, jnp.broadcast_to(gam, (RWKV_HEADS, CHUNK, RWKV_HEAD)), 0.0) - _bmm(bvt, wmat)
            h_ref[d, 0, c] = _bmm(kvt, vh) - _bmm(bvt, u0)
        return carry

    lax.fori_loop(0, tm // CHUNK, chunk, 0)


def _wkvprep(mix, mu_prev, mu_next, w0, w_up, a0, a_up, k_k, k_a, r_k, seg):
    b, t, _ = mix.shape
    tm = min(TOKEN_TILE, t)
    assert t % tm == 0 and tm % CHUNK == 0
    nt = t // tm
    nc = tm // CHUNK
    hb = tm // 8
    last8 = t // 8 - 1
    const = lambda shape: pl.BlockSpec(shape, lambda bi, i: (0,) * len(shape))
    hd = (2, 1, RWKV_HEADS, tm, RWKV_HEAD)
    st = (2, 1, nc, RWKV_HEADS, RWKV_HEAD, RWKV_HEAD)
    return pl.pallas_call(
        _wkvprep_kernel,
        grid=(b, nt),
        in_specs=[
            pl.BlockSpec((1, tm, SHIFT_WIDTH), lambda bi, i: (bi, i, 0)),
            pl.BlockSpec((1, 8, SHIFT_WIDTH), lambda bi, i: (bi, jnp.maximum(i * hb - 1, 0), 0)),
            pl.BlockSpec((1, 8, SHIFT_WIDTH), lambda bi, i: (bi, jnp.minimum((i + 1) * hb, last8), 0)),
            const((1, SHIFT_WIDTH)),
            const((1, SHIFT_WIDTH)),
            const((2, RWKV_WIDTH)),
            const((2, W_LORA, RWKV_WIDTH)),
            const((2, RWKV_WIDTH)),
            const((2, A_LORA, RWKV_WIDTH)),
            const((1, RWKV_WIDTH)),
            const((1, RWKV_WIDTH)),
            const((2, RWKV_WIDTH)),
            const((RWKV_WIDTH, RWKV_WIDTH)),
        ],
        out_specs=[
            pl.BlockSpec((1, tm, RWKV_WIDTH), lambda bi, i: (bi, i, 0)),
            pl.BlockSpec(hd, lambda bi, i: (0, bi, 0, i, 0)),
            pl.BlockSpec(hd, lambda bi, i: (0, bi, 0, i, 0)),
            pl.BlockSpec(st, lambda bi, i: (0, bi, i, 0, 0, 0)),
            pl.BlockSpec(st, lambda bi, i: (0, bi, i, 0, 0, 0)),
        ],
        out_shape=[
            jax.ShapeDtypeStruct((b, t, RWKV_WIDTH), F32),
            jax.ShapeDtypeStruct((2, b, RWKV_HEADS, t, RWKV_HEAD), F32),
            jax.ShapeDtypeStruct((2, b, RWKV_HEADS, t, RWKV_HEAD), F32),
            jax.ShapeDtypeStruct((2, b, t // CHUNK, RWKV_HEADS, RWKV_HEAD, RWKV_HEAD), F32),
            jax.ShapeDtypeStruct((2, b, t // CHUNK, RWKV_HEADS, RWKV_HEAD, RWKV_HEAD), F32),
        ],
        scratch_shapes=[
            pltpu.VMEM((tm, RWKV_WIDTH), F32),
            pltpu.VMEM((tm, RWKV_WIDTH), F32),
            pltpu.VMEM((tm, RWKV_WIDTH), F32),
            pltpu.VMEM((2, tm, RWKV_WIDTH), F32),
            pltpu.VMEM((2, tm, RWKV_WIDTH), F32),
            pltpu.VMEM((2, tm, RWKV_WIDTH), F32),
        ],
        compiler_params=_cparams(("parallel", "parallel")),
        name="wkvprep",
    )(mix, mix, mix, mu_prev, mu_next, w0, w_up, a0, a_up, k_k, k_a, r_k, seg)


def _wkvscan_kernel(qeff_ref, o0_ref, g_ref, h_ref, s0_ref, o_ref, sfin_ref, s_sc):
    d = pl.program_id(0)
    c = pl.program_id(2)
    nc = g_ref.shape[2]

    @pl.when(c == 0)
    def _():
        s_sc[...] = s0_ref[0, 0]

    for j in range(nc):
        jj = j + d * (nc - 1 - 2 * j)
        sl = pl.ds(pl.multiple_of(jj * CHUNK, CHUNK), CHUNK)
        s = s_sc[...]
        o = jnp.einsum('hij,hjk->hik', qeff_ref[0, 0, :, sl, :], s, precision=HIGHEST,
                       preferred_element_type=F32) + o0_ref[0, 0, :, sl, :]
        s_sc[...] = jnp.einsum('hij,hjk->hik', g_ref[0, 0, jj], s, precision=HIGHEST,
                               preferred_element_type=F32) + h_ref[0, 0, jj]
        for hh in range(RWKV_HEADS):
            o_ref[0, 0, sl, hh * RWKV_HEAD:(hh + 1) * RWKV_HEAD] = o[hh]

    @pl.when(c == pl.num_programs(2) - 1)
    def _():
        sfin_ref[0, 0] = s_sc[...]


def _wkvscan(qeff, o0, g, h, s0):
    _, b, _, t, _ = qeff.shape
    tc = min(TOKEN_TILE, t)
    nc = tc // CHUNK
    nt = t // tc
    visit = lambda d, c: c + d * (nt - 1 - 2 * c)
    hd = (1, 1, RWKV_HEADS, tc, RWKV_HEAD)
    st = (1, 1, nc, RWKV_HEADS, RWKV_HEAD, RWKV_HEAD)
    s_spec = pl.BlockSpec((1, 1, RWKV_HEADS, RWKV_HEAD, RWKV_HEAD), lambda d, bi, c: (d, bi, 0, 0, 0))
    return pl.pallas_call(
        _wkvscan_kernel,
        grid=(2, b, nt),
        in_specs=[
            pl.BlockSpec(hd, lambda d, bi, c: (d, bi, 0, visit(d, c), 0)),
            pl.BlockSpec(hd, lambda d, bi, c: (d, bi, 0, visit(d, c), 0)),
            pl.BlockSpec(st, lambda d, bi, c: (d, bi, visit(d, c), 0, 0, 0)),
            pl.BlockSpec(st, lambda d, bi, c: (d, bi, visit(d, c), 0, 0, 0)),
            s_spec,
        ],
        out_specs=[
            pl.BlockSpec((1, 1, tc, RWKV_WIDTH), lambda d, bi, c: (d, bi, visit(d, c), 0)),
            s_spec,
        ],
        out_shape=[
            jax.ShapeDtypeStruct((2, b, t, RWKV_WIDTH), F32),
            jax.ShapeDtypeStruct((2, b, RWKV_HEADS, RWKV_HEAD, RWKV_HEAD), F32),
        ],
        scratch_shapes=[pltpu.VMEM((RWKV_HEADS, RWKV_HEAD, RWKV_HEAD), F32)],
        compiler_params=_cparams(("arbitrary", "arbitrary", "arbitrary")),
        name="wkvscan",
    )(qeff, o0, g, h, s0)


def _merge_kernel(x_ref, mod_ref, ya_ref, ga_ref, o_ref, bonus_ref, gb_ref,
                  wpa_ref, wpb_ref, wmg_ref, bmg_ref, wo_ref, gnw_ref, gnb_ref, lng_ref, lnb_ref, seg_ref,
                  out_ref):
    x = x_ref[0]
    shift = mod_ref[0, 0:1, :]
    scale = mod_ref[0, 1:2, :]
    gate = mod_ref[0, 2:3, :]
    h = _layer_norm(x) * (1.0 + scale) + shift
    gates = jax.nn.sigmoid(jnp.dot(h.astype(BF16), wmg_ref[...], preferred_element_type=F32) + bmg_ref[...])
    ya = jnp.dot((ya_ref[0] * ga_ref[0].astype(F32)).astype(BF16), wpa_ref[...], preferred_element_type=F32)

    o = o_ref[0, 0] + o_ref[1, 0]
    seg = seg_ref[...]
    mu = _dot_split_lhs(o, seg, 2) * (1.0 / RWKV_HEAD)
    oc = o - mu
    var = _dot_split_lhs(oc * oc, seg, 2) * (1.0 / RWKV_HEAD)
    y = oc * lax.rsqrt(var + GN_EPS) * gnw_ref[...] + gnb_ref[...] + bonus_ref[0]
    yb = jnp.dot((y * gb_ref[0].astype(F32)).astype(BF16), wpb_ref[...], preferred_element_type=F32)

    mixed = gates[:, 0:D_MODEL] * ya + gates[:, D_MODEL:2 * D_MODEL] * yb
    out = jnp.dot(mixed.astype(BF16), wo_ref[...], preferred_element_type=F32)
    out_ref[0] = _layer_norm(ALPHA * x + gate * out) * lng_ref[...] + lnb_ref[...]


def _merge(x, mod, y_att, ga, o, bonus, gb, w_pa, w_pb, w_mg, b_mg, w_o, gn_w, gn_b, ln_g, ln_b, seg):
    b, t, _ = x.shape
    tm = min(TOKEN_TILE, t)
    const = lambda shape: pl.BlockSpec(shape, lambda bi, i: (0,) * len(shape))
    tok = lambda width: pl.BlockSpec((1, tm, width), lambda bi, i: (bi, i, 0))
    return pl.pallas_call(
        _merge_kernel,
        grid=(b, t // tm),
        in_specs=[
            tok(D_MODEL),
            pl.BlockSpec((1, 3, D_MODEL), lambda bi, i: (bi, 0, 0)),
            tok(ATT_WIDTH),
            tok(ATT_WIDTH),
            pl.BlockSpec((2, 1, tm, RWKV_WIDTH), lambda bi, i: (0, bi, i, 0)),
            tok(RWKV_WIDTH),
            tok(RWKV_WIDTH),
            const((ATT_WIDTH, D_MODEL)),
            const((RWKV_WIDTH, D_MODEL)),
            const((D_MODEL, 2 * D_MODEL)),
            const((1, 2 * D_MODEL)),
            const((D_MODEL, D_MODEL)),
            const((1, RWKV_WIDTH)),
            const((1, RWKV_WIDTH)),
            const((1, D_MODEL)),
            const((1, D_MODEL)),
            const((RWKV_WIDTH, RWKV_WIDTH)),
        ],
        out_specs=tok(D_MODEL),
        out_shape=jax.ShapeDtypeStruct((b, t, D_MODEL), F32),
        compiler_params=_cparams(("parallel", "parallel")),
        name="merge",
    )(x, mod, y_att, ga, o, bonus, gb, w_pa, w_pb, w_mg, b_mg, w_o, gn_w, gn_b, ln_g, ln_b, seg)


def _rope_tables_t(t):
    pos = jnp.arange(t, dtype=jnp.int32)
    row = (pos // GRID_W).astype(F32)
    col = (pos % GRID_W).astype(F32)
    inv = ROPE_BASE ** (-jnp.arange(ROPE_FREQS, dtype=F32) / ROPE_FREQS)
    ang_r = inv[:, None] * row[None, :]
    ang_c = inv[:, None] * col[None, :]
    cos_t = jnp.concatenate([jnp.cos(ang_r), jnp.cos(ang_r), jnp.cos(ang_c), jnp.cos(ang_c)], axis=0)
    sin_t = jnp.concatenate([-jnp.sin(ang_r), jnp.sin(ang_r), -jnp.sin(ang_c), jnp.sin(ang_c)], axis=0)
    return cos_t, sin_t


def kernel(x, c, ctx, c_ctx, w_ada, b_ada, w_in, q_norm, k_norm, mu_prev, mu_next, w0, w_up, a0, a_up,
           k_k, k_a, r_k, gn_w, gn_b, w_pa, w_pb, w_mg, b_mg, w_o, ln_g, ln_b):
    b, t, _ = x.shape
    tc = ctx.shape[1]
    l = 0
    seg = jnp.kron(jnp.eye(RWKV_HEADS, dtype=F32), jnp.ones((RWKV_HEAD, RWKV_HEAD), F32)).astype(BF16)

    rows = 8 * ((b + 1 + 7) // 8)
    cc = jnp.zeros((rows, D_MODEL), F32).at[0:b].set(c).at[b].set(c_ctx)
    ada = _ada(cc, w_ada[l], b_ada[l][None, :])
    mod_x = ada[0:b].reshape(b, 3, D_MODEL)
    mod_c = jnp.broadcast_to(ada[b].reshape(1, 3, D_MODEL), (b, 3, D_MODEL))

    w_t = w_in[l][:, 0:QKV_WIDTH].T.astype(BF16)
    w_n = w_in[l][:, QKV_WIDTH:].astype(BF16)
    q_gain = q_norm[l][:, None]
    k_gain = k_norm[l][:, None]
    cos_x, sin_x = _rope_tables_t(t)
    cos_c = jnp.ones((HEAD_DIM, tc), F32)
    sin_c = jnp.zeros((HEAD_DIM, tc), F32)

    q_x, k_x, v_x, ga_x, mix_x, gb_x = _inproj(x, mod_x, cos_x, sin_x, w_t, w_n, q_gain, k_gain)
    _, k_c, v_c, _, mix_c, _ = _inproj(ctx, mod_c, cos_c, sin_c, w_t, w_n, q_gain, k_gain)

    y_att = _attention(q_x, jnp.concatenate([k_c, k_x], axis=2), jnp.concatenate([v_c, v_x], axis=2))

    prep_args = (mu_prev[l][None, :], mu_next[l][None, :], w0[l], w_up[l].astype(BF16), a0[l], a_up[l].astype(BF16),
                 k_k[l][None, :], k_a[l][None, :], r_k[l].reshape(2, RWKV_WIDTH), seg)
    _, qe_c, o0_c, g_c, h_c = _wkvprep(mix_c, *prep_args)
    zero_state = jnp.zeros((2, b, RWKV_HEADS, RWKV_HEAD, RWKV_HEAD), F32)
    _, s_ctx = _wkvscan(qe_c, o0_c, g_c, h_c, zero_state)
    bonus, qe_x, o0_x, g_x, h_x = _wkvprep(mix_x, *prep_args)
    o_x, _ = _wkvscan(qe_x, o0_x, g_x, h_x, s_ctx)

    return _merge(x, mod_x, y_att, ga_x, o_x, bonus, gb_x,
                  w_pa[l].astype(BF16), w_pb[l].astype(BF16), w_mg[l].astype(BF16), b_mg[l][None, :],
                  w_o[l].astype(BF16), gn_w[l][None, :], gn_b[l][None, :], ln_g[l][None, :], ln_b[l][None, :], seg)
```

```python
import functools
import math

import jax
import jax.numpy as jnp
from jax import lax
from jax.experimental import pallas as pl
from jax.experimental.pallas import tpu as pltpu

F32 = jnp.float32
BF16 = jnp.bfloat16
HIGHEST = lax.Precision.HIGHEST

D_MODEL = 1024
GRID_W = 64
HEAD_DIM = 64
N_Q_HEADS = 8
N_KV_HEADS = 2
GQA_GROUP = N_Q_HEADS // N_KV_HEADS
ATT_WIDTH = N_Q_HEADS * HEAD_DIM
KV_WIDTH = N_KV_HEADS * HEAD_DIM
QKV_WIDTH = ATT_WIDTH + 2 * KV_WIDTH
ROPE_FREQS = HEAD_DIM // 4
ROPE_BASE = 10000.0
RWKV_HEAD = 64
RWKV_HEADS = 8
RWKV_WIDTH = RWKV_HEADS * RWKV_HEAD
W_LORA = 64
A_LORA = 64
SHIFT_WIDTH = 3 * RWKV_WIDTH + 2 * W_LORA + 2 * A_LORA
IN_WIDTH = 2 * ATT_WIDTH + 2 * KV_WIDTH + SHIFT_WIDTH + RWKV_WIDTH
NAT_WIDTH = IN_WIDTH - QKV_WIDTH
LN_EPS = 1e-5
QK_EPS = 1e-6
GN_EPS = 64e-5
DEPTH = 1
ALPHA = (2.0 * DEPTH) ** 0.25

CHUNK = 64
KEY_BLOCK = 256
V_ROWS = HEAD_DIM + 8
TOKEN_TILE = 256
Q_TILE = 512
ATTN_COLS = 256
ATTN_UNROLL = 8
WKV_GROUP = 2
VMEM_LIMIT = 48 * 1024 * 1024
NEG_BIG = -1e30


def _cparams(sem):
    return pltpu.CompilerParams(dimension_semantics=sem, vmem_limit_bytes=VMEM_LIMIT)


def _silu(x):
    return x * jax.nn.sigmoid(x)


def _layer_norm(x):
    mu = jnp.mean(x, axis=-1, keepdims=True)
    xc = x - mu
    var = jnp.mean(xc * xc, axis=-1, keepdims=True)
    return xc * lax.rsqrt(var + LN_EPS)


def _split_bf16(x, parts):
    out = []
    rem = x
    for _ in range(parts):
        p = rem.astype(BF16)
        out.append(p)
        rem = rem - p.astype(F32)
    return out


def _dot_split_lhs(x, m_bf16, parts):
    acc = None
    for p in _split_bf16(x, parts):
        t = jnp.dot(p, m_bf16, preferred_element_type=F32)
        acc = t if acc is None else acc + t
    return acc


def _dot_split_rhs(m_bf16, x, parts):
    acc = None
    for p in _split_bf16(x, parts):
        t = jnp.dot(m_bf16, p, preferred_element_type=F32)
        acc = t if acc is None else acc + t
    return acc


def _ada_kernel(c_ref, w_ref, b_ref, o_ref):
    s = _silu(c_ref[...])
    o_ref[...] = jnp.dot(s, w_ref[...], precision=HIGHEST, preferred_element_type=F32) + b_ref[...]


def _ada(cc, w_ada, b_ada):
    rows = cc.shape[0]
    nblk = (3 * D_MODEL) // D_MODEL
    return pl.pallas_call(
        _ada_kernel,
        grid=(nblk,),
        in_specs=[
            pl.BlockSpec((rows, D_MODEL), lambda j: (0, 0)),
            pl.BlockSpec((D_MODEL, D_MODEL), lambda j: (0, j)),
            pl.BlockSpec((1, D_MODEL), lambda j: (0, j)),
        ],
        out_specs=pl.BlockSpec((rows, D_MODEL), lambda j: (0, j)),
        out_shape=jax.ShapeDtypeStruct((rows, 3 * D_MODEL), F32),
        compiler_params=_cparams(("arbitrary",)),
        name="ada",
    )(cc, w_ada, b_ada)


def _rope_t(x, cos, sin):
    xs = jnp.concatenate([x[:, 16:32], x[:, 0:16], x[:, 48:64], x[:, 32:48]], axis=1)
    return x * cos + xs * sin


def _inproj_kernel(x_ref, mod_ref, cos_ref, sin_ref, wt_ref, wn_ref, qg_ref, kg_ref,
                   q_ref, k_ref, v_ref, ga_ref, mix_ref, gb_ref):
    tm = x_ref.shape[1]
    x = x_ref[0]
    shift = mod_ref[0, 0:1, :]
    scale = mod_ref[0, 1:2, :]
    h = _layer_norm(x) * (1.0 + scale) + shift
    hb = h.astype(BF16)
    zt = lax.dot_general(wt_ref[...], hb, (((1,), (1,)), ((), ())), preferred_element_type=F32)
    zn = jnp.dot(hb, wn_ref[...], preferred_element_type=F32)
    cos = cos_ref[...][None]
    sin = sin_ref[...][None]

    qz = zt[0:ATT_WIDTH].reshape(N_Q_HEADS, HEAD_DIM, tm)
    qn = qz * lax.rsqrt(jnp.mean(qz * qz, axis=1, keepdims=True) + QK_EPS) * qg_ref[...][None]
    qr = _rope_t(qn, cos, sin) * (HEAD_DIM ** -0.5 * math.log2(math.e))
    q_ref[0] = qr.astype(BF16)

    kz = zt[ATT_WIDTH:ATT_WIDTH + KV_WIDTH].reshape(N_KV_HEADS, HEAD_DIM, tm)
    kn = kz * lax.rsqrt(jnp.mean(kz * kz, axis=1, keepdims=True) + QK_EPS) * kg_ref[...][None]
    kr = _rope_t(kn, cos, sin)
    vz = zt[ATT_WIDTH + KV_WIDTH:QKV_WIDTH].reshape(N_KV_HEADS, HEAD_DIM, tm)
    pad_row = lax.broadcasted_iota(jnp.int32, (V_ROWS - HEAD_DIM, tm), 0)
    ones_pad = jnp.where(pad_row == 0, 1.0, 0.0).astype(BF16)
    for j in range(N_KV_HEADS):
        k_ref[0, j, 0] = kr[j].T.astype(BF16)
        v_ref[0, j, 0, 0:HEAD_DIM, :] = vz[j].astype(BF16)
        v_ref[0, j, 0, HEAD_DIM:V_ROWS, :] = ones_pad

    ga_ref[0] = _silu(zn[:, 0:ATT_WIDTH]).astype(BF16)
    mix_ref[0] = zn[:, ATT_WIDTH:ATT_WIDTH + SHIFT_WIDTH]
    gb_ref[0] = _silu(zn[:, ATT_WIDTH + SHIFT_WIDTH:NAT_WIDTH]).astype(BF16)


def _inproj(x, mod, cos_t, sin_t, w_t, w_n, q_gain, k_gain):
    b, t, _ = x.shape
    tm = TOKEN_TILE
    assert tm == KEY_BLOCK and t % tm == 0
    nt = t // tm
    const = lambda shape: pl.BlockSpec(shape, lambda bi, i: (0,) * len(shape))
    return pl.pallas_call(
        _inproj_kernel,
        grid=(b, nt),
        in_specs=[
            pl.BlockSpec((1, tm, D_MODEL), lambda bi, i: (bi, i, 0)),
            pl.BlockSpec((1, 3, D_MODEL), lambda bi, i: (bi, 0, 0)),
            pl.BlockSpec((HEAD_DIM, tm), lambda bi, i: (0, i)),
            pl.BlockSpec((HEAD_DIM, tm), lambda bi, i: (0, i)),
            const((QKV_WIDTH, D_MODEL)),
            const((D_MODEL, NAT_WIDTH)),
            const((HEAD_DIM, 1)),
            const((HEAD_DIM, 1)),
        ],
        out_specs=[
            pl.BlockSpec((1, N_Q_HEADS, HEAD_DIM, tm), lambda bi, i: (bi, 0, 0, i)),
            pl.BlockSpec((1, N_KV_HEADS, 1, tm, HEAD_DIM), lambda bi, i: (bi, 0, i, 0, 0)),
            pl.BlockSpec((1, N_KV_HEADS, 1, V_ROWS, tm), lambda bi, i: (bi, 0, i, 0, 0)),
            pl.BlockSpec((1, tm, ATT_WIDTH), lambda bi, i: (bi, i, 0)),
            pl.BlockSpec((1, tm, SHIFT_WIDTH), lambda bi, i: (bi, i, 0)),
            pl.BlockSpec((1, tm, RWKV_WIDTH), lambda bi, i: (bi, i, 0)),
        ],
        out_shape=[
            jax.ShapeDtypeStruct((b, N_Q_HEADS, HEAD_DIM, t), BF16),
            jax.ShapeDtypeStruct((b, N_KV_HEADS, nt, tm, HEAD_DIM), BF16),
            jax.ShapeDtypeStruct((b, N_KV_HEADS, nt, V_ROWS, tm), BF16),
            jax.ShapeDtypeStruct((b, t, ATT_WIDTH), BF16),
            jax.ShapeDtypeStruct((b, t, SHIFT_WIDTH), F32),
            jax.ShapeDtypeStruct((b, t, RWKV_WIDTH), BF16),
        ],
        compiler_params=_cparams(("parallel", "parallel")),
        name="inproj",
    )(x, mod, cos_t, sin_t, w_t, w_n, q_gain, k_gain)


def _attn_kernel(q_ref, k_ref, v_ref, o_ref, s_sc, cmax_sc, m_sc, acc_sc):
    nkb = k_ref.shape[2]
    tq = q_ref.shape[3]
    width = GQA_GROUP * tq
    nchunk = width // ATTN_COLS
    per_head = tq // ATTN_COLS
    m_sc[...] = jnp.full(m_sc.shape, NEG_BIG, F32)
    acc_sc[...] = jnp.zeros(acc_sc.shape, F32)

    def tick(t, par, score=True, accumulate=True):
        if accumulate:
            m_old = m_sc[...]
            m_new = jnp.maximum(m_old, cmax_sc[1 - par])
            alpha = jnp.exp2(m_old - m_new)
            m_sc[...] = m_new
            vblk = v_ref[0, 0, t - 1]
        if score:
            kblk = k_ref[0, 0, t]
        for ch in range(nchunk):
            cols = slice(ch * ATTN_COLS, (ch + 1) * ATTN_COLS)
            if accumulate:
                p = jnp.exp2(s_sc[1 - par, :, cols] - m_new[:, cols]).astype(BF16)
                acc_sc[:, cols] = acc_sc[:, cols] * alpha[:, cols] + jnp.dot(vblk, p, preferred_element_type=F32)
            if score:
                g, h = divmod(ch, per_head)
                s = jnp.dot(kblk, q_ref[0, g, :, h * ATTN_COLS:(h + 1) * ATTN_COLS], preferred_element_type=F32)
                s_sc[par, :, cols] = s
                cmax_sc[par, :, cols] = jnp.max(s, axis=0, keepdims=True)

    unroll = max(u for u in range(2, ATTN_UNROLL + 1, 2) if (nkb - 1) % u == 0)
    tick(0, 0, accumulate=False)

    def group(j, carry):
        for u in range(unroll):
            tick(unroll * j + u + 1, (u + 1) % 2)
        return carry

    lax.fori_loop(0, (nkb - 1) // unroll, group, 0)
    tick(nkb, 1, score=False)
    for g in range(GQA_GROUP):
        acc = acc_sc[:, g * tq:(g + 1) * tq]
        o = acc[0:HEAD_DIM] / acc[HEAD_DIM:HEAD_DIM + 1]
        o_ref[0, :, g * HEAD_DIM:(g + 1) * HEAD_DIM] = o.T


def _attention(q_t, k_blk, v_blk):
    b, _, _, t = q_t.shape
    nkb = k_blk.shape[2]
    tq = min(Q_TILE, t)
    assert t % tq == 0
    width = GQA_GROUP * tq
    return pl.pallas_call(
        _attn_kernel,
        grid=(b, N_KV_HEADS, t // tq),
        in_specs=[
            pl.BlockSpec((1, GQA_GROUP, HEAD_DIM, tq), lambda bi, j, i: (bi, j, 0, i)),
            pl.BlockSpec((1, 1, nkb, KEY_BLOCK, HEAD_DIM), lambda bi, j, i: (bi, j, 0, 0, 0)),
            pl.BlockSpec((1, 1, nkb, V_ROWS, KEY_BLOCK), lambda bi, j, i: (bi, j, 0, 0, 0)),
        ],
        out_specs=pl.BlockSpec((1, tq, GQA_GROUP * HEAD_DIM), lambda bi, j, i: (bi, i, j)),
        out_shape=jax.ShapeDtypeStruct((b, t, ATT_WIDTH), F32),
        scratch_shapes=[
            pltpu.VMEM((2, KEY_BLOCK, width), F32),
            pltpu.VMEM((2, 1, width), F32),
            pltpu.VMEM((1, width), F32),
            pltpu.VMEM((V_ROWS, width), F32),
        ],
        compiler_params=_cparams(("parallel", "parallel", "parallel")),
        name="attn",
    )(q_t, k_blk, v_blk)


def _heads(x):
    return jnp.stack([x[:, h * RWKV_HEAD:(h + 1) * RWKV_HEAD] for h in range(RWKV_HEADS)], axis=0)


def _bmm(a, b):
    return jnp.einsum('hij,hjk->hik', a.astype(BF16), b.astype(BF16), preferred_element_type=F32)


def _bmm_nt(a, b):
    return jnp.einsum('hik,hjk->hij', a.astype(BF16), b.astype(BF16), preferred_element_type=F32)


def _softplus(y):
    return jnp.maximum(y, 0.0) + jnp.log(1.0 + jnp.exp(-jnp.abs(y)))


def _wkvprep_kernel(mix_ref, prev_ref, next_ref, mup_ref, mun_ref, w0_ref, wup_ref, a0_ref, aup_ref,
                    kk_ref, ka_ref, rk_ref, seg_ref,
                    bonus_ref, qeff_ref, o0_ref, g_ref, h_ref,
                    r_sc, v_sc, kn_sc, ld_sc, kd_sc, b_sc):
    tm = mix_ref.shape[1]
    i = pl.program_id(1)
    n = pl.num_programs(1)
    z = mix_ref[0]
    prev_row = jnp.where(i > 0, prev_ref[0, 7:8, :], 0.0)
    next_row = jnp.where(i < n - 1, next_ref[0, 0:1, :], 0.0)
    rows = lax.broadcasted_iota(jnp.int32, (tm, 1), 0)
    zp = jnp.where(rows == 0, prev_row, pltpu.roll(z, 1, 0))
    zn = jnp.where(rows == tm - 1, next_row, pltpu.roll(z, tm - 1, 0))
    zs = z + mup_ref[...] * (zp - z) + mun_ref[...] * (zn - z)

    w = RWKV_WIDTH
    r = zs[:, 0:w]
    k = zs[:, w:2 * w]
    v = zs[:, 2 * w:3 * w]
    seg = seg_ref[...]
    kkr = k * kk_ref[...]
    norm = jnp.sqrt(_dot_split_lhs(kkr * kkr, seg, 2))
    kn = kkr / jnp.maximum(norm, 1e-12)
    r_sc[...] = r
    v_sc[...] = v
    kn_sc[...] = kn
    coef = jnp.zeros((tm, w), F32)
    for d in range(2):
        wl = zs[:, 3 * w + d * W_LORA:3 * w + (d + 1) * W_LORA]
        al = zs[:, 3 * w + 2 * W_LORA + d * A_LORA:3 * w + 2 * W_LORA + (d + 1) * A_LORA]
        w_pre = w0_ref[d:d + 1, :] + jnp.dot(jnp.tanh(wl).astype(BF16), wup_ref[d], preferred_element_type=F32)
        w_log = -_softplus(-w_pre) - 0.5
        ld_sc[d] = -jnp.exp(w_log)
        a = jax.nn.sigmoid(a0_ref[d:d + 1, :] + jnp.dot(al.astype(BF16), aup_ref[d], preferred_element_type=F32))
        kd = k * (1.0 + (a - 1.0) * ka_ref[...])
        kd_sc[d] = kd
        b_sc[d] = kn * a
        coef = coef + _dot_split_lhs(r * kd * rk_ref[d:d + 1, :], seg, 2)
    bonus_ref[0] = coef * v

    ri = lax.broadcasted_iota(jnp.int32, (CHUNK, CHUNK), 0)
    ci = lax.broadcasted_iota(jnp.int32, (CHUNK, CHUNK), 1)
    eye_f = jnp.where(ri == ci, 1.0, 0.0)
    strict_f = [jnp.where(ci < ri, 1.0, 0.0), jnp.where(ci > ri, 1.0, 0.0)]
    incl_f = [s + eye_f for s in strict_f]
    nh = RWKV_HEADS

    def group(gi, carry):
        combos = [(gi * WKV_GROUP + u, d) for u in range(WKV_GROUP) for d in range(2)]
        slices = []
        parts = {name: [] for name in ("rh", "kkh", "kt", "bt", "kv", "bv", "gam", "vh")}
        for c, d in combos:
            sl = pl.ds(pl.multiple_of(c * CHUNK, CHUNK), CHUNK)
            slices.append(sl)
            ld = ld_sc[d, sl, :]
            cum = _dot_split_rhs(incl_f[d].astype(BF16), ld, 3)
            tot = cum[CHUNK - 1:CHUNK, :] if d == 0 else cum[0:1, :]
            e_pos = jnp.exp(cum)
            e_prev = jnp.exp(cum - ld)
            e_neg = jnp.exp(-cum)
            e_rest = jnp.exp(tot - cum)
            kd_c = kd_sc[d, sl, :]
            b_c = b_sc[d, sl, :]
            parts["rh"].append(_heads(r_sc[sl, :] * e_pos))
            parts["kkh"].append(_heads(kn_sc[sl, :] * e_prev))
            parts["kt"].append(_heads(kd_c * e_neg))
            parts["bt"].append(_heads(b_c * e_neg))
            parts["kv"].append(_heads(kd_c * e_rest))
            parts["bv"].append(_heads(b_c * e_rest))
            parts["gam"].append(_heads(jnp.exp(tot)))
            parts["vh"].append(_heads(v_sc[sl, :]))
        cat = lambda xs: jnp.concatenate(xs, axis=0)
        rh, kkh, kt, bt, kv, bv, gam, vh = (cat(parts[n]) for n in ("rh", "kkh", "kt", "bt", "kv", "bv", "gam", "vh"))
        per = lambda ms: cat([jnp.broadcast_to(ms[d][None], (nh, CHUNK, CHUNK)) for _, d in combos])
        strict = per(strict_f)
        incl = per(incl_f)

        lb = _bmm_nt(kkh, bt) * strict
        lk = _bmm_nt(kkh, kt) * strict
        ark = _bmm_nt(rh, kt) * incl
        arb = _bmm_nt(rh, bt) * incl
        xp = -lb
        tinv = eye_f[None] + xp
        span = 2
        while span < CHUNK:
            xp = _bmm(xp, xp)
            tinv = tinv + _bmm(tinv, xp)
            span *= 2
        wmat = _bmm(tinv, kkh)
        u0 = _bmm(tinv, _bmm(lk, vh))
        qeff = rh - _bmm(arb, wmat)
        o0 = _bmm(ark, vh) - _bmm(arb, u0)
        bvt = jnp.swapaxes(bv, 1, 2)
        kvt = jnp.swapaxes(kv, 1, 2)
        gmat = eye_f[None] * gam - _bmm(bvt, wmat)
        hmat = _bmm(kvt, vh) - _bmm(bvt, u0)
        for idx, (c, d) in enumerate(combos):
            rng = slice(idx * nh, (idx + 1) * nh)
            qeff_ref[d, 0, :, slices[idx], :] = qeff[rng]
            o0_ref[d, 0, :, slices[idx], :] = o0[rng]
            g_ref[d, 0, c] = gmat[rng]
            h_ref[d, 0, c] = hmat[rng]
        return carry

    lax.fori_loop(0, tm // (CHUNK * WKV_GROUP), group, 0)


def _wkvprep(mix, mu_prev, mu_next, w0, w_up, a0, a_up, k_k, k_a, r_k, seg):
    b, t, _ = mix.shape
    tm = min(TOKEN_TILE, t)
    assert t % tm == 0 and tm % CHUNK == 0
    nt = t // tm
    nc = tm // CHUNK
    hb = tm // 8
    last8 = t // 8 - 1
    const = lambda shape: pl.BlockSpec(shape, lambda bi, i: (0,) * len(shape))
    hd = (2, 1, RWKV_HEADS, tm, RWKV_HEAD)
    st = (2, 1, nc, RWKV_HEADS, RWKV_HEAD, RWKV_HEAD)
    return pl.pallas_call(
        _wkvprep_kernel,
        grid=(b, nt),
        in_specs=[
            pl.BlockSpec((1, tm, SHIFT_WIDTH), lambda bi, i: (bi, i, 0)),
            pl.BlockSpec((1, 8, SHIFT_WIDTH), lambda bi, i: (bi, jnp.maximum(i * hb - 1, 0), 0)),
            pl.BlockSpec((1, 8, SHIFT_WIDTH), lambda bi, i: (bi, jnp.minimum((i + 1) * hb, last8), 0)),
            const((1, SHIFT_WIDTH)),
            const((1, SHIFT_WIDTH)),
            const((2, RWKV_WIDTH)),
            const((2, W_LORA, RWKV_WIDTH)),
            const((2, RWKV_WIDTH)),
            const((2, A_LORA, RWKV_WIDTH)),
            const((1, RWKV_WIDTH)),
            const((1, RWKV_WIDTH)),
            const((2, RWKV_WIDTH)),
            const((RWKV_WIDTH, RWKV_WIDTH)),
        ],
        out_specs=[
            pl.BlockSpec((1, tm, RWKV_WIDTH), lambda bi, i: (bi, i, 0)),
            pl.BlockSpec(hd, lambda bi, i: (0, bi, 0, i, 0)),
            pl.BlockSpec(hd, lambda bi, i: (0, bi, 0, i, 0)),
            pl.BlockSpec(st, lambda bi, i: (0, bi, i, 0, 0, 0)),
            pl.BlockSpec(st, lambda bi, i: (0, bi, i, 0, 0, 0)),
        ],
        out_shape=[
            jax.ShapeDtypeStruct((b, t, RWKV_WIDTH), F32),
            jax.ShapeDtypeStruct((2, b, RWKV_HEADS, t, RWKV_HEAD), F32),
            jax.ShapeDtypeStruct((2, b, RWKV_HEADS, t, RWKV_HEAD), F32),
            jax.ShapeDtypeStruct((2, b, t // CHUNK, RWKV_HEADS, RWKV_HEAD, RWKV_HEAD), F32),
            jax.ShapeDtypeStruct((2, b, t // CHUNK, RWKV_HEADS, RWKV_HEAD, RWKV_HEAD), F32),
        ],
        scratch_shapes=[
            pltpu.VMEM((tm, RWKV_WIDTH), F32),
            pltpu.VMEM((tm, RWKV_WIDTH), F32),
            pltpu.VMEM((tm, RWKV_WIDTH), F32),
            pltpu.VMEM((2, tm, RWKV_WIDTH), F32),
            pltpu.VMEM((2, tm, RWKV_WIDTH), F32),
            pltpu.VMEM((2, tm, RWKV_WIDTH), F32),
        ],
        compiler_params=_cparams(("parallel", "parallel")),
        name="wkvprep",
    )(mix, mix, mix, mu_prev, mu_next, w0, w_up, a0, a_up, k_k, k_a, r_k, seg)


def _wkvscan_kernel(qf_ref, qb_ref, of_ref, ob_ref, gf_ref, gb_ref, hf_ref, hb_ref, s0_ref,
                    outf_ref, outb_ref, sfin_ref, s_sc):
    c = pl.program_id(0)
    nb = qf_ref.shape[1]
    nc = gf_ref.shape[2]
    nh = RWKV_HEADS

    @pl.when(c == 0)
    def _():
        s_sc[...] = s0_ref[...].reshape(s_sc.shape)

    def bdot(a, s_parts):
        ab = a.astype(BF16)
        return sum(jnp.einsum('hij,hjk->hik', ab, p, preferred_element_type=F32) for p in s_parts)

    for j in range(nc):
        jb = nc - 1 - j
        rf = slice(j * CHUNK, (j + 1) * CHUNK)
        rb = slice(jb * CHUNK, (jb + 1) * CHUNK)
        s = _split_bf16(s_sc[...], 2)
        qe = jnp.concatenate([qf_ref[0, bi, :, rf, :] for bi in range(nb)]
                             + [qb_ref[0, bi, :, rb, :] for bi in range(nb)], axis=0)
        o0 = jnp.concatenate([of_ref[0, bi, :, rf, :] for bi in range(nb)]
                             + [ob_ref[0, bi, :, rb, :] for bi in range(nb)], axis=0)
        gm = jnp.concatenate([gf_ref[0, bi, j] for bi in range(nb)] + [gb_ref[0, bi, jb] for bi in range(nb)], axis=0)
        hm = jnp.concatenate([hf_ref[0, bi, j] for bi in range(nb)] + [hb_ref[0, bi, jb] for bi in range(nb)], axis=0)
        o = bdot(qe, s) + o0
        s_sc[...] = bdot(gm, s) + hm
        for bi in range(nb):
            for hh in range(nh):
                outf_ref[bi, rf, hh * RWKV_HEAD:(hh + 1) * RWKV_HEAD] = o[bi * nh + hh]
                outb_ref[bi, rb, hh * RWKV_HEAD:(hh + 1) * RWKV_HEAD] = o[(nb + bi) * nh + hh]

    @pl.when(c == pl.num_programs(0) - 1)
    def _():
        sfin_ref[...] = s_sc[...].reshape(sfin_ref.shape)


def _wkvscan(qeff, o0, g, h, s0):
    _, b, _, t, _ = qeff.shape
    tc = min(TOKEN_TILE, t)
    nc = tc // CHUNK
    nt = t // tc
    hd = (1, b, RWKV_HEADS, tc, RWKV_HEAD)
    st = (1, b, nc, RWKV_HEADS, RWKV_HEAD, RWKV_HEAD)
    fwd_hd = pl.BlockSpec(hd, lambda c: (0, 0, 0, c, 0))
    bwd_hd = pl.BlockSpec(hd, lambda c: (1, 0, 0, nt - 1 - c, 0))
    fwd_st = pl.BlockSpec(st, lambda c: (0, 0, c, 0, 0, 0))
    bwd_st = pl.BlockSpec(st, lambda c: (1, 0, nt - 1 - c, 0, 0, 0))
    s_spec = pl.BlockSpec((2, b, RWKV_HEADS, RWKV_HEAD, RWKV_HEAD), lambda c: (0, 0, 0, 0, 0))
    return pl.pallas_call(
        _wkvscan_kernel,
        grid=(nt,),
        in_specs=[fwd_hd, bwd_hd, fwd_hd, bwd_hd, fwd_st, bwd_st, fwd_st, bwd_st, s_spec],
        out_specs=[
            pl.BlockSpec((b, tc, RWKV_WIDTH), lambda c: (0, c, 0)),
            pl.BlockSpec((b, tc, RWKV_WIDTH), lambda c: (0, nt - 1 - c, 0)),
            s_spec,
        ],
        out_shape=[
            jax.ShapeDtypeStruct((b, t, RWKV_WIDTH), F32),
            jax.ShapeDtypeStruct((b, t, RWKV_WIDTH), F32),
            jax.ShapeDtypeStruct((2, b, RWKV_HEADS, RWKV_HEAD, RWKV_HEAD), F32),
        ],
        scratch_shapes=[pltpu.VMEM((2 * b * RWKV_HEADS, RWKV_HEAD, RWKV_HEAD), F32)],
        compiler_params=_cparams(("arbitrary",)),
        name="wkvscan",
    )(qeff, qeff, o0, o0, g, g, h, h, s0)


def _merge_kernel(x_ref, mod_ref, ya_ref, ga_ref, of_ref, ob_ref, bonus_ref, gb_ref,
                  wpa_ref, wpb_ref, wmg_ref, bmg_ref, wo_ref, gnw_ref, gnb_ref, lng_ref, lnb_ref, seg_ref,
                  out_ref):
    x = x_ref[0]
    shift = mod_ref[0, 0:1, :]
    scale = mod_ref[0, 1:2, :]
    gate = mod_ref[0, 2:3, :]
    h = _layer_norm(x) * (1.0 + scale) + shift
    gates = jax.nn.sigmoid(jnp.dot(h.astype(BF16), wmg_ref[...], preferred_element_type=F32) + bmg_ref[...])
    ya = jnp.dot((ya_ref[0] * ga_ref[0].astype(F32)).astype(BF16), wpa_ref[...], preferred_element_type=F32)

    o = of_ref[0] + ob_ref[0]
    seg = seg_ref[...]
    mu = _dot_split_lhs(o, seg, 2) * (1.0 / RWKV_HEAD)
    oc = o - mu
    var = _dot_split_lhs(oc * oc, seg, 2) * (1.0 / RWKV_HEAD)
    y = oc * lax.rsqrt(var + GN_EPS) * gnw_ref[...] + gnb_ref[...] + bonus_ref[0]
    yb = jnp.dot((y * gb_ref[0].astype(F32)).astype(BF16), wpb_ref[...], preferred_element_type=F32)

    mixed = gates[:, 0:D_MODEL] * ya + gates[:, D_MODEL:2 * D_MODEL] * yb
    out = jnp.dot(mixed.astype(BF16), wo_ref[...], preferred_element_type=F32)
    out_ref[0] = _layer_norm(ALPHA * x + gate * out) * lng_ref[...] + lnb_ref[...]


def _merge(x, mod, y_att, ga, o_f, o_b, bonus, gb, w_pa, w_pb, w_mg, b_mg, w_o, gn_w, gn_b, ln_g, ln_b, seg):
    b, t, _ = x.shape
    tm = min(TOKEN_TILE, t)
    const = lambda shape: pl.BlockSpec(shape, lambda bi, i: (0,) * len(shape))
    tok = lambda width: pl.BlockSpec((1, tm, width), lambda bi, i: (bi, i, 0))
    return pl.pallas_call(
        _merge_kernel,
        grid=(b, t // tm),
        in_specs=[
            tok(D_MODEL),
            pl.BlockSpec((1, 3, D_MODEL), lambda bi, i: (bi, 0, 0)),
            tok(ATT_WIDTH),
            tok(ATT_WIDTH),
            tok(RWKV_WIDTH),
            tok(RWKV_WIDTH),
            tok(RWKV_WIDTH),
            tok(RWKV_WIDTH),
            const((ATT_WIDTH, D_MODEL)),
            const((RWKV_WIDTH, D_MODEL)),
            const((D_MODEL, 2 * D_MODEL)),
            const((1, 2 * D_MODEL)),
            const((D_MODEL, D_MODEL)),
            const((1, RWKV_WIDTH)),
            const((1, RWKV_WIDTH)),
            const((1, D_MODEL)),
            const((1, D_MODEL)),
            const((RWKV_WIDTH, RWKV_WIDTH)),
        ],
        out_specs=tok(D_MODEL),
        out_shape=jax.ShapeDtypeStruct((b, t, D_MODEL), F32),
        compiler_params=_cparams(("parallel", "parallel")),
        name="merge",
    )(x, mod, y_att, ga, o_f, o_b, bonus, gb, w_pa, w_pb, w_mg, b_mg, w_o, gn_w, gn_b, ln_g, ln_b, seg)


def _rope_tables_t(t):
    pos = jnp.arange(t, dtype=jnp.int32)
    row = (pos // GRID_W).astype(F32)
    col = (pos % GRID_W).astype(F32)
    inv = ROPE_BASE ** (-jnp.arange(ROPE_FREQS, dtype=F32) / ROPE_FREQS)
    ang_r = inv[:, None] * row[None, :]
    ang_c = inv[:, None] * col[None, :]
    cos_t = jnp.concatenate([jnp.cos(ang_r), jnp.cos(ang_r), jnp.cos(ang_c), jnp.cos(ang_c)], axis=0)
    sin_t = jnp.concatenate([-jnp.sin(ang_r), jnp.sin(ang_r), -jnp.sin(ang_c), jnp.sin(ang_c)], axis=0)
    return cos_t, sin_t


def kernel(x, c, ctx, c_ctx, w_ada, b_ada, w_in, q_norm, k_norm, mu_prev, mu_next, w0, w_up, a0, a_up,
           k_k, k_a, r_k, gn_w, gn_b, w_pa, w_pb, w_mg, b_mg, w_o, ln_g, ln_b):
    b, t, _ = x.shape
    tc = ctx.shape[1]
    l = 0
    seg = jnp.kron(jnp.eye(RWKV_HEADS, dtype=F32), jnp.ones((RWKV_HEAD, RWKV_HEAD), F32)).astype(BF16)

    rows = 8 * ((b + 1 + 7) // 8)
    cc = jnp.zeros((rows, D_MODEL), F32).at[0:b].set(c).at[b].set(c_ctx)
    ada = _ada(cc, w_ada[l], b_ada[l][None, :])
    mod_x = ada[0:b].reshape(b, 3, D_MODEL)
    mod_c = jnp.broadcast_to(ada[b].reshape(1, 3, D_MODEL), (b, 3, D_MODEL))

    w_t = w_in[l][:, 0:QKV_WIDTH].T.astype(BF16)
    w_n = w_in[l][:, QKV_WIDTH:].astype(BF16)
    q_gain = q_norm[l][:, None]
    k_gain = k_norm[l][:, None]
    cos_x, sin_x = _rope_tables_t(t)
    cos_c = jnp.ones((HEAD_DIM, tc), F32)
    sin_c = jnp.zeros((HEAD_DIM, tc), F32)

    q_x, k_x, v_x, ga_x, mix_x, gb_x = _inproj(x, mod_x, cos_x, sin_x, w_t, w_n, q_gain, k_gain)
    _, k_c, v_c, _, mix_c, _ = _inproj(ctx, mod_c, cos_c, sin_c, w_t, w_n, q_gain, k_gain)

    y_att = _attention(q_x, jnp.concatenate([k_c, k_x], axis=2), jnp.concatenate([v_c, v_x], axis=2))

    prep_args = (mu_prev[l][None, :], mu_next[l][None, :], w0[l], w_up[l].astype(BF16), a0[l], a_up[l].astype(BF16),
                 k_k[l][None, :], k_a[l][None, :], r_k[l].reshape(2, RWKV_WIDTH), seg)
    _, qe_c, o0_c, g_c, h_c = _wkvprep(mix_c, *prep_args)
    zero_state = jnp.zeros((2, b, RWKV_HEADS, RWKV_HEAD, RWKV_HEAD), F32)
    _, _, s_ctx = _wkvscan(qe_c, o0_c, g_c, h_c, zero_state)
    bonus, qe_x, o0_x, g_x, h_x = _wkvprep(mix_x, *prep_args)
    of_x, ob_x, _ = _wkvscan(qe_x, o0_x, g_x, h_x, s_ctx)

    return _merge(x, mod_x, y_att, ga_x, of_x, ob_x, bonus, gb_x,
                  w_pa[l].astype(BF16), w_pb[l].astype(BF16), w_mg[l].astype(BF16), b_mg[l][None, :],
                  w_o[l].astype(BF16), gn_w[l][None, :], gn_b[l][None, :], ln_g[l][None, :], ln_b[l][None, :], seg)
```

```python
import functools
import math

import jax
import jax.numpy as jnp
from jax import lax
from jax.experimental import pallas as pl
from jax.experimental.pallas import tpu as pltpu

F32 = jnp.float32
BF16 = jnp.bfloat16
FP8 = jnp.float8_e4m3fn
HIGHEST = lax.Precision.HIGHEST

D_MODEL = 1024
GRID_W = 64
HEAD_DIM = 64
N_Q_HEADS = 8
N_KV_HEADS = 2
GQA_GROUP = N_Q_HEADS // N_KV_HEADS
ATT_WIDTH = N_Q_HEADS * HEAD_DIM
KV_WIDTH = N_KV_HEADS * HEAD_DIM
QKV_WIDTH = ATT_WIDTH + 2 * KV_WIDTH
ROPE_FREQS = HEAD_DIM // 4
ROPE_BASE = 10000.0
RWKV_HEAD = 64
RWKV_HEADS = 8
RWKV_WIDTH = RWKV_HEADS * RWKV_HEAD
W_LORA = 64
A_LORA = 64
SHIFT_WIDTH = 3 * RWKV_WIDTH + 2 * W_LORA + 2 * A_LORA
IN_WIDTH = 2 * ATT_WIDTH + 2 * KV_WIDTH + SHIFT_WIDTH + RWKV_WIDTH
NAT_WIDTH = IN_WIDTH - QKV_WIDTH
LN_EPS = 1e-5
QK_EPS = 1e-6
GN_EPS = 64e-5
DEPTH = 1
ALPHA = (2.0 * DEPTH) ** 0.25

CHUNK = 64
KEY_BLOCK = 256
V_ROWS = HEAD_DIM + 8
FP8_DEPTH = 3 * HEAD_DIM
FP8_LO_SCALE = 16.0
FP8_Q_SCALE = 2.0
FP8_MAX = 448.0
TOKEN_TILE = 256
Q_TILE = 512
ATTN_COLS = 256
ATTN_UNROLL = 8
WKV_GROUP = 2
VMEM_LIMIT = 48 * 1024 * 1024
NEG_BIG = -1e30


def _cparams(sem):
    return pltpu.CompilerParams(dimension_semantics=sem, vmem_limit_bytes=VMEM_LIMIT)


def _silu(x):
    return x * jax.nn.sigmoid(x)


def _layer_norm(x):
    mu = jnp.mean(x, axis=-1, keepdims=True)
    xc = x - mu
    var = jnp.mean(xc * xc, axis=-1, keepdims=True)
    return xc * lax.rsqrt(var + LN_EPS)


def _split_bf16(x, parts):
    out = []
    rem = x
    for _ in range(parts):
        p = rem.astype(BF16)
        out.append(p)
        rem = rem - p.astype(F32)
    return out


def _dot_split_lhs(x, m_bf16, parts):
    acc = None
    for p in _split_bf16(x, parts):
        t = jnp.dot(p, m_bf16, preferred_element_type=F32)
        acc = t if acc is None else acc + t
    return acc


def _dot_split_rhs(m_bf16, x, parts):
    acc = None
    for p in _split_bf16(x, parts):
        t = jnp.dot(m_bf16, p, preferred_element_type=F32)
        acc = t if acc is None else acc + t
    return acc


def _ada_kernel(c_ref, w_ref, b_ref, o_ref):
    s = _silu(c_ref[...])
    o_ref[...] = jnp.dot(s, w_ref[...], precision=HIGHEST, preferred_element_type=F32) + b_ref[...]


def _ada(cc, w_ada, b_ada):
    rows = cc.shape[0]
    nblk = (3 * D_MODEL) // D_MODEL
    return pl.pallas_call(
        _ada_kernel,
        grid=(nblk,),
        in_specs=[
            pl.BlockSpec((rows, D_MODEL), lambda j: (0, 0)),
            pl.BlockSpec((D_MODEL, D_MODEL), lambda j: (0, j)),
            pl.BlockSpec((1, D_MODEL), lambda j: (0, j)),
        ],
        out_specs=pl.BlockSpec((rows, D_MODEL), lambda j: (0, j)),
        out_shape=jax.ShapeDtypeStruct((rows, 3 * D_MODEL), F32),
        compiler_params=_cparams(("arbitrary",)),
        name="ada",
    )(cc, w_ada, b_ada)


def _rope_t(x, cos, sin):
    xs = jnp.concatenate([x[:, 16:32], x[:, 0:16], x[:, 48:64], x[:, 32:48]], axis=1)
    return x * cos + xs * sin


def _fp8_pieces(x):
    hi = x.astype(FP8).astype(F32)
    lo16 = ((x - hi) * FP8_LO_SCALE).astype(FP8).astype(F32)
    hi16 = (hi * (1.0 / FP8_LO_SCALE)).astype(FP8).astype(F32)
    return hi, hi16, lo16


def _inproj_kernel(x_ref, mod_ref, cos_ref, sin_ref, wt_ref, wn_ref, qg_ref, kg_ref,
                   q_ref, k_ref, q8_ref, k8_ref, v_ref, ga_ref, mix_ref, gb_ref):
    tm = x_ref.shape[1]
    x = x_ref[0]
    shift = mod_ref[0, 0:1, :]
    scale = mod_ref[0, 1:2, :]
    h = _layer_norm(x) * (1.0 + scale) + shift
    hb = h.astype(BF16)
    zt = lax.dot_general(wt_ref[...], hb, (((1,), (1,)), ((), ())), preferred_element_type=F32)
    zn = jnp.dot(hb, wn_ref[...], preferred_element_type=F32)
    cos = cos_ref[...][None]
    sin = sin_ref[...][None]

    qz = zt[0:ATT_WIDTH].reshape(N_Q_HEADS, HEAD_DIM, tm)
    qn = qz * lax.rsqrt(jnp.mean(qz * qz, axis=1, keepdims=True) + QK_EPS) * qg_ref[...][None]
    qr = _rope_t(qn, cos, sin) * (HEAD_DIM ** -0.5 * math.log2(math.e))
    q_ref[0] = qr.astype(BF16)
    q_hi, q_hi16, q_lo16 = _fp8_pieces(qr * FP8_Q_SCALE)
    q8_ref[0] = jnp.concatenate([q_hi, q_hi16, q_lo16], axis=1).astype(FP8)

    kz = zt[ATT_WIDTH:ATT_WIDTH + KV_WIDTH].reshape(N_KV_HEADS, HEAD_DIM, tm)
    kn = kz * lax.rsqrt(jnp.mean(kz * kz, axis=1, keepdims=True) + QK_EPS) * kg_ref[...][None]
    kr = _rope_t(kn, cos, sin)
    k_hi, k_hi16, k_lo16 = _fp8_pieces(kr * (1.0 / FP8_Q_SCALE))
    k8 = jnp.concatenate([k_hi, k_lo16, k_hi16], axis=1)
    vz = zt[ATT_WIDTH + KV_WIDTH:QKV_WIDTH].reshape(N_KV_HEADS, HEAD_DIM, tm)
    pad_row = lax.broadcasted_iota(jnp.int32, (V_ROWS - HEAD_DIM, tm), 0)
    ones_pad = jnp.where(pad_row == 0, 1.0, 0.0).astype(BF16)
    for j in range(N_KV_HEADS):
        k_ref[0, j, 0] = kr[j].T.astype(BF16)
        k8_ref[0, j, 0] = k8[j].T.astype(FP8)
        v_ref[0, j, 0, 0:HEAD_DIM, :] = vz[j].astype(BF16)
        v_ref[0, j, 0, HEAD_DIM:V_ROWS, :] = ones_pad

    ga_ref[0] = _silu(zn[:, 0:ATT_WIDTH]).astype(BF16)
    mix_ref[0] = zn[:, ATT_WIDTH:ATT_WIDTH + SHIFT_WIDTH]
    gb_ref[0] = _silu(zn[:, ATT_WIDTH + SHIFT_WIDTH:NAT_WIDTH]).astype(BF16)


def _inproj(x, mod, cos_t, sin_t, w_t, w_n, q_gain, k_gain):
    b, t, _ = x.shape
    tm = TOKEN_TILE
    assert tm == KEY_BLOCK and t % tm == 0
    nt = t // tm
    const = lambda shape: pl.BlockSpec(shape, lambda bi, i: (0,) * len(shape))
    return pl.pallas_call(
        _inproj_kernel,
        grid=(b, nt),
        in_specs=[
            pl.BlockSpec((1, tm, D_MODEL), lambda bi, i: (bi, i, 0)),
            pl.BlockSpec((1, 3, D_MODEL), lambda bi, i: (bi, 0, 0)),
            pl.BlockSpec((HEAD_DIM, tm), lambda bi, i: (0, i)),
            pl.BlockSpec((HEAD_DIM, tm), lambda bi, i: (0, i)),
            const((QKV_WIDTH, D_MODEL)),
            const((D_MODEL, NAT_WIDTH)),
            const((HEAD_DIM, 1)),
            const((HEAD_DIM, 1)),
        ],
        out_specs=[
            pl.BlockSpec((1, N_Q_HEADS, HEAD_DIM, tm), lambda bi, i: (bi, 0, 0, i)),
            pl.BlockSpec((1, N_KV_HEADS, 1, tm, HEAD_DIM), lambda bi, i: (bi, 0, i, 0, 0)),
            pl.BlockSpec((1, N_Q_HEADS, FP8_DEPTH, tm), lambda bi, i: (bi, 0, 0, i)),
            pl.BlockSpec((1, N_KV_HEADS, 1, tm, FP8_DEPTH), lambda bi, i: (bi, 0, i, 0, 0)),
            pl.BlockSpec((1, N_KV_HEADS, 1, V_ROWS, tm), lambda bi, i: (bi, 0, i, 0, 0)),
            pl.BlockSpec((1, tm, ATT_WIDTH), lambda bi, i: (bi, i, 0)),
            pl.BlockSpec((1, tm, SHIFT_WIDTH), lambda bi, i: (bi, i, 0)),
            pl.BlockSpec((1, tm, RWKV_WIDTH), lambda bi, i: (bi, i, 0)),
        ],
        out_shape=[
            jax.ShapeDtypeStruct((b, N_Q_HEADS, HEAD_DIM, t), BF16),
            jax.ShapeDtypeStruct((b, N_KV_HEADS, nt, tm, HEAD_DIM), BF16),
            jax.ShapeDtypeStruct((b, N_Q_HEADS, FP8_DEPTH, t), FP8),
            jax.ShapeDtypeStruct((b, N_KV_HEADS, nt, tm, FP8_DEPTH), FP8),
            jax.ShapeDtypeStruct((b, N_KV_HEADS, nt, V_ROWS, tm), BF16),
            jax.ShapeDtypeStruct((b, t, ATT_WIDTH), BF16),
            jax.ShapeDtypeStruct((b, t, SHIFT_WIDTH), F32),
            jax.ShapeDtypeStruct((b, t, RWKV_WIDTH), BF16),
        ],
        compiler_params=_cparams(("parallel", "parallel")),
        name="inproj",
    )(x, mod, cos_t, sin_t, w_t, w_n, q_gain, k_gain)


def _attn_kernel(q_ref, k_ref, v_ref, o_ref, s_sc, cmax_sc, m_sc, acc_sc):
    nkb = k_ref.shape[2]
    tq = q_ref.shape[3]
    width = GQA_GROUP * tq
    nchunk = width // ATTN_COLS
    per_head = tq // ATTN_COLS
    m_sc[...] = jnp.full(m_sc.shape, NEG_BIG, F32)
    acc_sc[...] = jnp.zeros(acc_sc.shape, F32)

    def tick(t, par, score=True, accumulate=True):
        if accumulate:
            m_old = m_sc[...]
            m_new = jnp.maximum(m_old, cmax_sc[1 - par])
            alpha = jnp.exp2(m_old - m_new)
            m_sc[...] = m_new
            vblk = v_ref[0, 0, t - 1]
        if score:
            kblk = k_ref[0, 0, t]
        for ch in range(nchunk):
            cols = slice(ch * ATTN_COLS, (ch + 1) * ATTN_COLS)
            if accumulate:
                p = jnp.exp2(s_sc[1 - par, :, cols] - m_new[:, cols]).astype(BF16)
                acc_sc[:, cols] = acc_sc[:, cols] * alpha[:, cols] + jnp.dot(vblk, p, preferred_element_type=F32)
            if score:
                g, h = divmod(ch, per_head)
                s = jnp.dot(kblk, q_ref[0, g, :, h * ATTN_COLS:(h + 1) * ATTN_COLS], preferred_element_type=F32)
                s_sc[par, :, cols] = s
                cmax_sc[par, :, cols] = jnp.max(s, axis=0, keepdims=True)

    unroll = max(u for u in range(2, ATTN_UNROLL + 1, 2) if (nkb - 1) % u == 0)
    tick(0, 0, accumulate=False)

    def group(j, carry):
        for u in range(unroll):
            tick(unroll * j + u + 1, (u + 1) % 2)
        return carry

    lax.fori_loop(0, (nkb - 1) // unroll, group, 0)
    tick(nkb, 1, score=False)
    for g in range(GQA_GROUP):
        acc = acc_sc[:, g * tq:(g + 1) * tq]
        o = acc[0:HEAD_DIM] / acc[HEAD_DIM:HEAD_DIM + 1]
        o_ref[0, :, g * HEAD_DIM:(g + 1) * HEAD_DIM] = o.T


def _attention(q_t, k_blk, v_blk):
    b, _, depth, t = q_t.shape
    nkb = k_blk.shape[2]
    tq = min(Q_TILE, t)
    assert t % tq == 0 and k_blk.shape[4] == depth
    width = GQA_GROUP * tq
    return pl.pallas_call(
        _attn_kernel,
        grid=(b, N_KV_HEADS, t // tq),
        in_specs=[
            pl.BlockSpec((1, GQA_GROUP, depth, tq), lambda bi, j, i: (bi, j, 0, i)),
            pl.BlockSpec((1, 1, nkb, KEY_BLOCK, depth), lambda bi, j, i: (bi, j, 0, 0, 0)),
            pl.BlockSpec((1, 1, nkb, V_ROWS, KEY_BLOCK), lambda bi, j, i: (bi, j, 0, 0, 0)),
        ],
        out_specs=pl.BlockSpec((1, tq, GQA_GROUP * HEAD_DIM), lambda bi, j, i: (bi, i, j)),
        out_shape=jax.ShapeDtypeStruct((b, t, ATT_WIDTH), F32),
        scratch_shapes=[
            pltpu.VMEM((2, KEY_BLOCK, width), F32),
            pltpu.VMEM((2, 1, width), F32),
            pltpu.VMEM((1, width), F32),
            pltpu.VMEM((V_ROWS, width), F32),
        ],
        compiler_params=_cparams(("parallel", "parallel", "parallel")),
        name="attn",
    )(q_t, k_blk, v_blk)


def _heads(x):
    return jnp.stack([x[:, h * RWKV_HEAD:(h + 1) * RWKV_HEAD] for h in range(RWKV_HEADS)], axis=0)


def _bmm(a, b):
    return jnp.einsum('hij,hjk->hik', a.astype(BF16), b.astype(BF16), preferred_element_type=F32)


def _bmm_nt(a, b):
    return jnp.einsum('hik,hjk->hij', a.astype(BF16), b.astype(BF16), preferred_element_type=F32)


def _softplus(y):
    return jnp.maximum(y, 0.0) + jnp.log(1.0 + jnp.exp(-jnp.abs(y)))


def _wkvprep_kernel(mix_ref, prev_ref, next_ref, mup_ref, mun_ref, w0_ref, wup_ref, a0_ref, aup_ref,
                    kk_ref, ka_ref, rk_ref, seg_ref,
                    bonus_ref, qeff_ref, o0_ref, g_ref, h_ref,
                    r_sc, v_sc, kn_sc, ld_sc, kd_sc, b_sc):
    tm = mix_ref.shape[1]
    i = pl.program_id(1)
    n = pl.num_programs(1)
    z = mix_ref[0]
    prev_row = jnp.where(i > 0, prev_ref[0, 7:8, :], 0.0)
    next_row = jnp.where(i < n - 1, next_ref[0, 0:1, :], 0.0)
    rows = lax.broadcasted_iota(jnp.int32, (tm, 1), 0)
    zp = jnp.where(rows == 0, prev_row, pltpu.roll(z, 1, 0))
    zn = jnp.where(rows == tm - 1, next_row, pltpu.roll(z, tm - 1, 0))
    zs = z + mup_ref[...] * (zp - z) + mun_ref[...] * (zn - z)

    w = RWKV_WIDTH
    r = zs[:, 0:w]
    k = zs[:, w:2 * w]
    v = zs[:, 2 * w:3 * w]
    seg = seg_ref[...]
    kkr = k * kk_ref[...]
    norm = jnp.sqrt(_dot_split_lhs(kkr * kkr, seg, 2))
    kn = kkr / jnp.maximum(norm, 1e-12)
    r_sc[...] = r
    v_sc[...] = v
    kn_sc[...] = kn
    coef = jnp.zeros((tm, w), F32)
    for d in range(2):
        wl = zs[:, 3 * w + d * W_LORA:3 * w + (d + 1) * W_LORA]
        al = zs[:, 3 * w + 2 * W_LORA + d * A_LORA:3 * w + 2 * W_LORA + (d + 1) * A_LORA]
        w_pre = w0_ref[d:d + 1, :] + jnp.dot(jnp.tanh(wl).astype(BF16), wup_ref[d], preferred_element_type=F32)
        w_log = -_softplus(-w_pre) - 0.5
        ld_sc[d] = -jnp.exp(w_log)
        a = jax.nn.sigmoid(a0_ref[d:d + 1, :] + jnp.dot(al.astype(BF16), aup_ref[d], preferred_element_type=F32))
        kd = k * (1.0 + (a - 1.0) * ka_ref[...])
        kd_sc[d] = kd
        b_sc[d] = kn * a
        coef = coef + _dot_split_lhs(r * kd * rk_ref[d:d + 1, :], seg, 2)
    bonus_ref[0] = coef * v

    ri = lax.broadcasted_iota(jnp.int32, (CHUNK, CHUNK), 0)
    ci = lax.broadcasted_iota(jnp.int32, (CHUNK, CHUNK), 1)
    eye_f = jnp.where(ri == ci, 1.0, 0.0)
    strict_f = [jnp.where(ci < ri, 1.0, 0.0), jnp.where(ci > ri, 1.0, 0.0)]
    incl_f = [s + eye_f for s in strict_f]
    nh = RWKV_HEADS

    def group(gi, carry):
        combos = [(gi * WKV_GROUP + u, d) for u in range(WKV_GROUP) for d in range(2)]
        slices = []
        parts = {name: [] for name in ("rh", "kkh", "kt", "bt", "kv", "bv", "gam", "vh")}
        for c, d in combos:
            sl = pl.ds(pl.multiple_of(c * CHUNK, CHUNK), CHUNK)
            slices.append(sl)
            ld = ld_sc[d, sl, :]
            cum = _dot_split_rhs(incl_f[d].astype(BF16), ld, 3)
            tot = cum[CHUNK - 1:CHUNK, :] if d == 0 else cum[0:1, :]
            e_pos = jnp.exp(cum)
            e_prev = jnp.exp(cum - ld)
            e_neg = jnp.exp(-cum)
            e_rest = jnp.exp(tot - cum)
            kd_c = kd_sc[d, sl, :]
            b_c = b_sc[d, sl, :]
            parts["rh"].append(_heads(r_sc[sl, :] * e_pos))
            parts["kkh"].append(_heads(kn_sc[sl, :] * e_prev))
            parts["kt"].append(_heads(kd_c * e_neg))
            parts["bt"].append(_heads(b_c * e_neg))
            parts["kv"].append(_heads(kd_c * e_rest))
            parts["bv"].append(_heads(b_c * e_rest))
            parts["gam"].append(_heads(jnp.exp(tot)))
            parts["vh"].append(_heads(v_sc[sl, :]))
        cat = lambda xs: jnp.concatenate(xs, axis=0)
        rh, kkh, kt, bt, kv, bv, gam, vh = (cat(parts[n]) for n in ("rh", "kkh", "kt", "bt", "kv", "bv", "gam", "vh"))
        per = lambda ms: cat([jnp.broadcast_to(ms[d][None], (nh, CHUNK, CHUNK)) for _, d in combos])
        strict = per(strict_f)
        incl = per(incl_f)

        lb = _bmm_nt(kkh, bt) * strict
        lk = _bmm_nt(kkh, kt) * strict
        ark = _bmm_nt(rh, kt) * incl
        arb = _bmm_nt(rh, bt) * incl
        xp = -lb
        tinv = eye_f[None] + xp
        span = 2
        while span < CHUNK:
            xp = _bmm(xp, xp)
            tinv = tinv + _bmm(tinv, xp)
            span *= 2
        wmat = _bmm(tinv, kkh)
        u0 = _bmm(tinv, _bmm(lk, vh))
        qeff = rh - _bmm(arb, wmat)
        o0 = _bmm(ark, vh) - _bmm(arb, u0)
        bvt = jnp.swapaxes(bv, 1, 2)
        kvt = jnp.swapaxes(kv, 1, 2)
        gmat = eye_f[None] * gam - _bmm(bvt, wmat)
        hmat = _bmm(kvt, vh) - _bmm(bvt, u0)
        for idx, (c, d) in enumerate(combos):
            rng = slice(idx * nh, (idx + 1) * nh)
            qeff_ref[d, 0, :, slices[idx], :] = qeff[rng].astype(BF16)
            o0_ref[d, 0, :, slices[idx], :] = o0[rng]
            g_ref[d, 0, c] = gmat[rng].astype(BF16)
            h_ref[d, 0, c] = hmat[rng]
        return carry

    lax.fori_loop(0, tm // (CHUNK * WKV_GROUP), group, 0)


def _wkvprep(mix, mu_prev, mu_next, w0, w_up, a0, a_up, k_k, k_a, r_k, seg):
    b, t, _ = mix.shape
    tm = min(TOKEN_TILE, t)
    assert t % tm == 0 and tm % CHUNK == 0
    nt = t // tm
    nc = tm // CHUNK
    hb = tm // 8
    last8 = t // 8 - 1
    const = lambda shape: pl.BlockSpec(shape, lambda bi, i: (0,) * len(shape))
    hd = (2, 1, RWKV_HEADS, tm, RWKV_HEAD)
    st = (2, 1, nc, RWKV_HEADS, RWKV_HEAD, RWKV_HEAD)
    return pl.pallas_call(
        _wkvprep_kernel,
        grid=(b, nt),
        in_specs=[
            pl.BlockSpec((1, tm, SHIFT_WIDTH), lambda bi, i: (bi, i, 0)),
            pl.BlockSpec((1, 8, SHIFT_WIDTH), lambda bi, i: (bi, jnp.maximum(i * hb - 1, 0), 0)),
            pl.BlockSpec((1, 8, SHIFT_WIDTH), lambda bi, i: (bi, jnp.minimum((i + 1) * hb, last8), 0)),
            const((1, SHIFT_WIDTH)),
            const((1, SHIFT_WIDTH)),
            const((2, RWKV_WIDTH)),
            const((2, W_LORA, RWKV_WIDTH)),
            const((2, RWKV_WIDTH)),
            const((2, A_LORA, RWKV_WIDTH)),
            const((1, RWKV_WIDTH)),
            const((1, RWKV_WIDTH)),
            const((2, RWKV_WIDTH)),
            const((RWKV_WIDTH, RWKV_WIDTH)),
        ],
        out_specs=[
            pl.BlockSpec((1, tm, RWKV_WIDTH), lambda bi, i: (bi, i, 0)),
            pl.BlockSpec(hd, lambda bi, i: (0, bi, 0, i, 0)),
            pl.BlockSpec(hd, lambda bi, i: (0, bi, 0, i, 0)),
            pl.BlockSpec(st, lambda bi, i: (0, bi, i, 0, 0, 0)),
            pl.BlockSpec(st, lambda bi, i: (0, bi, i, 0, 0, 0)),
        ],
        out_shape=[
            jax.ShapeDtypeStruct((b, t, RWKV_WIDTH), F32),
            jax.ShapeDtypeStruct((2, b, RWKV_HEADS, t, RWKV_HEAD), BF16),
            jax.ShapeDtypeStruct((2, b, RWKV_HEADS, t, RWKV_HEAD), F32),
            jax.ShapeDtypeStruct((2, b, t // CHUNK, RWKV_HEADS, RWKV_HEAD, RWKV_HEAD), BF16),
            jax.ShapeDtypeStruct((2, b, t // CHUNK, RWKV_HEADS, RWKV_HEAD, RWKV_HEAD), F32),
        ],
        scratch_shapes=[
            pltpu.VMEM((tm, RWKV_WIDTH), F32),
            pltpu.VMEM((tm, RWKV_WIDTH), F32),
            pltpu.VMEM((tm, RWKV_WIDTH), F32),
            pltpu.VMEM((2, tm, RWKV_WIDTH), F32),
            pltpu.VMEM((2, tm, RWKV_WIDTH), F32),
            pltpu.VMEM((2, tm, RWKV_WIDTH), F32),
        ],
        compiler_params=_cparams(("parallel", "parallel")),
        name="wkvprep",
    )(mix, mix, mix, mu_prev, mu_next, w0, w_up, a0, a_up, k_k, k_a, r_k, seg)


def _wkvscan_kernel(qf_ref, qb_ref, of_ref, ob_ref, gf_ref, gb_ref, hf_ref, hb_ref, s0_ref,
                    outf_ref, outb_ref, sfin_ref, s_sc):
    c = pl.program_id(0)
    nb = qf_ref.shape[1]
    nc = gf_ref.shape[2]
    nh = RWKV_HEADS

    @pl.when(c == 0)
    def _():
        s_sc[...] = s0_ref[...].reshape(s_sc.shape)

    def bdot(a, s_parts):
        ab = a.astype(BF16)
        return sum(jnp.einsum('hij,hjk->hik', ab, p, preferred_element_type=F32) for p in s_parts)

    for j in range(nc):
        jb = nc - 1 - j
        rf = slice(j * CHUNK, (j + 1) * CHUNK)
        rb = slice(jb * CHUNK, (jb + 1) * CHUNK)
        s = _split_bf16(s_sc[...], 2)
        qe = jnp.concatenate([qf_ref[0, bi, :, rf, :] for bi in range(nb)]
                             + [qb_ref[0, bi, :, rb, :] for bi in range(nb)], axis=0)
        o0 = jnp.concatenate([of_ref[0, bi, :, rf, :] for bi in range(nb)]
                             + [ob_ref[0, bi, :, rb, :] for bi in range(nb)], axis=0)
        gm = jnp.concatenate([gf_ref[0, bi, j] for bi in range(nb)] + [gb_ref[0, bi, jb] for bi in range(nb)], axis=0)
        hm = jnp.concatenate([hf_ref[0, bi, j] for bi in range(nb)] + [hb_ref[0, bi, jb] for bi in range(nb)], axis=0)
        o = bdot(qe, s) + o0
        s_sc[...] = bdot(gm, s) + hm
        for bi in range(nb):
            for hh in range(nh):
                outf_ref[bi, rf, hh * RWKV_HEAD:(hh + 1) * RWKV_HEAD] = o[bi * nh + hh]
                outb_ref[bi, rb, hh * RWKV_HEAD:(hh + 1) * RWKV_HEAD] = o[(nb + bi) * nh + hh]

    @pl.when(c == pl.num_programs(0) - 1)
    def _():
        sfin_ref[...] = s_sc[...].reshape(sfin_ref.shape)


def _wkvscan(qeff, o0, g, h, s0):
    _, b, _, t, _ = qeff.shape
    tc = min(TOKEN_TILE, t)
    nc = tc // CHUNK
    nt = t // tc
    hd = (1, b, RWKV_HEADS, tc, RWKV_HEAD)
    st = (1, b, nc, RWKV_HEADS, RWKV_HEAD, RWKV_HEAD)
    fwd_hd = pl.BlockSpec(hd, lambda c: (0, 0, 0, c, 0))
    bwd_hd = pl.BlockSpec(hd, lambda c: (1, 0, 0, nt - 1 - c, 0))
    fwd_st = pl.BlockSpec(st, lambda c: (0, 0, c, 0, 0, 0))
    bwd_st = pl.BlockSpec(st, lambda c: (1, 0, nt - 1 - c, 0, 0, 0))
    s_spec = pl.BlockSpec((2, b, RWKV_HEADS, RWKV_HEAD, RWKV_HEAD), lambda c: (0, 0, 0, 0, 0))
    return pl.pallas_call(
        _wkvscan_kernel,
        grid=(nt,),
        in_specs=[fwd_hd, bwd_hd, fwd_hd, bwd_hd, fwd_st, bwd_st, fwd_st, bwd_st, s_spec],
        out_specs=[
            pl.BlockSpec((b, tc, RWKV_WIDTH), lambda c: (0, c, 0)),
            pl.BlockSpec((b, tc, RWKV_WIDTH), lambda c: (0, nt - 1 - c, 0)),
            s_spec,
        ],
        out_shape=[
            jax.ShapeDtypeStruct((b, t, RWKV_WIDTH), F32),
            jax.ShapeDtypeStruct((b, t, RWKV_WIDTH), F32),
            jax.ShapeDtypeStruct((2, b, RWKV_HEADS, RWKV_HEAD, RWKV_HEAD), F32),
        ],
        scratch_shapes=[pltpu.VMEM((2 * b * RWKV_HEADS, RWKV_HEAD, RWKV_HEAD), F32)],
        compiler_params=_cparams(("arbitrary",)),
        name="wkvscan",
    )(qeff, qeff, o0, o0, g, g, h, h, s0)


def _merge_kernel(x_ref, mod_ref, ya_ref, ga_ref, of_ref, ob_ref, bonus_ref, gb_ref,
                  wpa_ref, wpb_ref, wmg_ref, bmg_ref, wo_ref, gnw_ref, gnb_ref, lng_ref, lnb_ref, seg_ref,
                  out_ref):
    x = x_ref[0]
    shift = mod_ref[0, 0:1, :]
    scale = mod_ref[0, 1:2, :]
    gate = mod_ref[0, 2:3, :]
    h = _layer_norm(x) * (1.0 + scale) + shift
    gates = jax.nn.sigmoid(jnp.dot(h.astype(BF16), wmg_ref[...], preferred_element_type=F32) + bmg_ref[...])
    ya = jnp.dot((ya_ref[0] * ga_ref[0].astype(F32)).astype(BF16), wpa_ref[...], preferred_element_type=F32)

    o = of_ref[0] + ob_ref[0]
    seg = seg_ref[...]
    mu = _dot_split_lhs(o, seg, 2) * (1.0 / RWKV_HEAD)
    oc = o - mu
    var = _dot_split_lhs(oc * oc, seg, 2) * (1.0 / RWKV_HEAD)
    y = oc * lax.rsqrt(var + GN_EPS) * gnw_ref[...] + gnb_ref[...] + bonus_ref[0]
    yb = jnp.dot((y * gb_ref[0].astype(F32)).astype(BF16), wpb_ref[...], preferred_element_type=F32)

    mixed = gates[:, 0:D_MODEL] * ya + gates[:, D_MODEL:2 * D_MODEL] * yb
    out = jnp.dot(mixed.astype(BF16), wo_ref[...], preferred_element_type=F32)
    out_ref[0] = _layer_norm(ALPHA * x + gate * out) * lng_ref[...] + lnb_ref[...]


def _merge(x, mod, y_att, ga, o_f, o_b, bonus, gb, w_pa, w_pb, w_mg, b_mg, w_o, gn_w, gn_b, ln_g, ln_b, seg):
    b, t, _ = x.shape
    tm = min(TOKEN_TILE, t)
    const = lambda shape: pl.BlockSpec(shape, lambda bi, i: (0,) * len(shape))
    tok = lambda width: pl.BlockSpec((1, tm, width), lambda bi, i: (bi, i, 0))
    return pl.pallas_call(
        _merge_kernel,
        grid=(b, t // tm),
        in_specs=[
            tok(D_MODEL),
            pl.BlockSpec((1, 3, D_MODEL), lambda bi, i: (bi, 0, 0)),
            tok(ATT_WIDTH),
            tok(ATT_WIDTH),
            tok(RWKV_WIDTH),
            tok(RWKV_WIDTH),
            tok(RWKV_WIDTH),
            tok(RWKV_WIDTH),
            const((ATT_WIDTH, D_MODEL)),
            const((RWKV_WIDTH, D_MODEL)),
            const((D_MODEL, 2 * D_MODEL)),
            const((1, 2 * D_MODEL)),
            const((D_MODEL, D_MODEL)),
            const((1, RWKV_WIDTH)),
            const((1, RWKV_WIDTH)),
            const((1, D_MODEL)),
            const((1, D_MODEL)),
            const((RWKV_WIDTH, RWKV_WIDTH)),
        ],
        out_specs=tok(D_MODEL),
        out_shape=jax.ShapeDtypeStruct((b, t, D_MODEL), F32),
        compiler_params=_cparams(("parallel", "parallel")),
        name="merge",
    )(x, mod, y_att, ga, o_f, o_b, bonus, gb, w_pa, w_pb, w_mg, b_mg, w_o, gn_w, gn_b, ln_g, ln_b, seg)


def _rope_tables_t(t):
    pos = jnp.arange(t, dtype=jnp.int32)
    row = (pos // GRID_W).astype(F32)
    col = (pos % GRID_W).astype(F32)
    inv = ROPE_BASE ** (-jnp.arange(ROPE_FREQS, dtype=F32) / ROPE_FREQS)
    ang_r = inv[:, None] * row[None, :]
    ang_c = inv[:, None] * col[None, :]
    cos_t = jnp.concatenate([jnp.cos(ang_r), jnp.cos(ang_r), jnp.cos(ang_c), jnp.cos(ang_c)], axis=0)
    sin_t = jnp.concatenate([-jnp.sin(ang_r), jnp.sin(ang_r), -jnp.sin(ang_c), jnp.sin(ang_c)], axis=0)
    return cos_t, sin_t


def kernel(x, c, ctx, c_ctx, w_ada, b_ada, w_in, q_norm, k_norm, mu_prev, mu_next, w0, w_up, a0, a_up,
           k_k, k_a, r_k, gn_w, gn_b, w_pa, w_pb, w_mg, b_mg, w_o, ln_g, ln_b):
    b, t, _ = x.shape
    tc = ctx.shape[1]
    l = 0
    seg = jnp.kron(jnp.eye(RWKV_HEADS, dtype=F32), jnp.ones((RWKV_HEAD, RWKV_HEAD), F32)).astype(BF16)

    rows = 8 * ((b + 1 + 7) // 8)
    cc = jnp.zeros((rows, D_MODEL), F32).at[0:b].set(c).at[b].set(c_ctx)
    ada = _ada(cc, w_ada[l], b_ada[l][None, :])
    mod_x = ada[0:b].reshape(b, 3, D_MODEL)
    mod_c = jnp.broadcast_to(ada[b].reshape(1, 3, D_MODEL), (b, 3, D_MODEL))

    w_t = w_in[l][:, 0:QKV_WIDTH].T.astype(BF16)
    w_n = w_in[l][:, QKV_WIDTH:].astype(BF16)
    q_gain = q_norm[l][:, None]
    k_gain = k_norm[l][:, None]
    cos_x, sin_x = _rope_tables_t(t)
    cos_c = jnp.ones((HEAD_DIM, tc), F32)
    sin_c = jnp.zeros((HEAD_DIM, tc), F32)

    q_x, k_x, q8_x, k8_x, v_x, ga_x, mix_x, gb_x = _inproj(x, mod_x, cos_x, sin_x, w_t, w_n, q_gain, k_gain)
    _, k_c, _, k8_c, v_c, _, mix_c, _ = _inproj(ctx, mod_c, cos_c, sin_c, w_t, w_n, q_gain, k_gain)

    v_all = jnp.concatenate([v_c, v_x], axis=2)
    fp8_ok = jnp.logical_and(
        jnp.max(jnp.abs(q_norm[l])) * (2.0 * HEAD_DIM ** 0.5 * HEAD_DIM ** -0.5 * math.log2(math.e) * FP8_Q_SCALE) < FP8_MAX,
        jnp.max(jnp.abs(k_norm[l])) * (2.0 * HEAD_DIM ** 0.5 / FP8_Q_SCALE) < FP8_MAX)
    y_att = lax.cond(
        fp8_ok,
        lambda: _attention(q8_x, jnp.concatenate([k8_c, k8_x], axis=2), v_all),
        lambda: _attention(q_x, jnp.concatenate([k_c, k_x], axis=2), v_all))

    prep_args = (mu_prev[l][None, :], mu_next[l][None, :], w0[l], w_up[l].astype(BF16), a0[l], a_up[l].astype(BF16),
                 k_k[l][None, :], k_a[l][None, :], r_k[l].reshape(2, RWKV_WIDTH), seg)
    _, qe_c, o0_c, g_c, h_c = _wkvprep(mix_c, *prep_args)
    zero_state = jnp.zeros((2, b, RWKV_HEADS, RWKV_HEAD, RWKV_HEAD), F32)
    _, _, s_ctx = _wkvscan(qe_c, o0_c, g_c, h_c, zero_state)
    bonus, qe_x, o0_x, g_x, h_x = _wkvprep(mix_x, *prep_args)
    of_x, ob_x, _ = _wkvscan(qe_x, o0_x, g_x, h_x, s_ctx)

    return _merge(x, mod_x, y_att, ga_x, of_x, ob_x, bonus, gb_x,
                  w_pa[l].astype(BF16), w_pb[l].astype(BF16), w_mg[l].astype(BF16), b_mg[l][None, :],
                  w_o[l].astype(BF16), gn_w[l][None, :], gn_b[l][None, :], ln_g[l][None, :], ln_b[l][None, :], seg)
```

```python
import functools
import math

import jax
import jax.numpy as jnp
from jax import lax
from jax.experimental import pallas as pl
from jax.experimental.pallas import tpu as pltpu

F32 = jnp.float32
BF16 = jnp.bfloat16
FP8 = jnp.float8_e4m3fn
HIGHEST = lax.Precision.HIGHEST

D_MODEL = 1024
GRID_W = 64
HEAD_DIM = 64
N_Q_HEADS = 8
N_KV_HEADS = 2
GQA_GROUP = N_Q_HEADS // N_KV_HEADS
ATT_WIDTH = N_Q_HEADS * HEAD_DIM
KV_WIDTH = N_KV_HEADS * HEAD_DIM
QKV_WIDTH = ATT_WIDTH + 2 * KV_WIDTH
ROPE_FREQS = HEAD_DIM // 4
ROPE_BASE = 10000.0
RWKV_HEAD = 64
RWKV_HEADS = 8
RWKV_WIDTH = RWKV_HEADS * RWKV_HEAD
W_LORA = 64
A_LORA = 64
SHIFT_WIDTH = 3 * RWKV_WIDTH + 2 * W_LORA + 2 * A_LORA
IN_WIDTH = 2 * ATT_WIDTH + 2 * KV_WIDTH + SHIFT_WIDTH + RWKV_WIDTH
NAT_WIDTH = IN_WIDTH - QKV_WIDTH
LN_EPS = 1e-5
QK_EPS = 1e-6
GN_EPS = 64e-5
DEPTH = 1
ALPHA = (2.0 * DEPTH) ** 0.25

CHUNK = 64
KEY_BLOCK = 256
V_ROWS = HEAD_DIM + 8
FP8_DEPTH = 3 * HEAD_DIM
FP8_LO_SCALE = 16.0
FP8_Q_SCALE = 2.0
FP8_MAX = 448.0
TOKEN_TILE = 256
Q_TILE = 512
ATTN_COLS = 256
ATTN_UNIT_BLOCKS = 4
ATTN_UNROLL = 2
WKV_GROUP = 2
VMEM_LIMIT = 48 * 1024 * 1024
NEG_BIG = -1e30


def _cparams(sem):
    return pltpu.CompilerParams(dimension_semantics=sem, vmem_limit_bytes=VMEM_LIMIT)


def _silu(x):
    return x * jax.nn.sigmoid(x)


def _layer_norm(x):
    mu = jnp.mean(x, axis=-1, keepdims=True)
    xc = x - mu
    var = jnp.mean(xc * xc, axis=-1, keepdims=True)
    return xc * lax.rsqrt(var + LN_EPS)


def _split_bf16(x, parts):
    out = []
    rem = x
    for _ in range(parts):
        p = rem.astype(BF16)
        out.append(p)
        rem = rem - p.astype(F32)
    return out


def _dot_split_lhs(x, m_bf16, parts):
    acc = None
    for p in _split_bf16(x, parts):
        t = jnp.dot(p, m_bf16, preferred_element_type=F32)
        acc = t if acc is None else acc + t
    return acc


def _dot_split_rhs(m_bf16, x, parts):
    acc = None
    for p in _split_bf16(x, parts):
        t = jnp.dot(m_bf16, p, preferred_element_type=F32)
        acc = t if acc is None else acc + t
    return acc


def _ada_kernel(c_ref, w_ref, b_ref, o_ref):
    s = _silu(c_ref[...])
    o_ref[...] = jnp.dot(s, w_ref[...], precision=HIGHEST, preferred_element_type=F32) + b_ref[...]


def _ada(cc, w_ada, b_ada):
    rows = cc.shape[0]
    nblk = (3 * D_MODEL) // D_MODEL
    return pl.pallas_call(
        _ada_kernel,
        grid=(nblk,),
        in_specs=[
            pl.BlockSpec((rows, D_MODEL), lambda j: (0, 0)),
            pl.BlockSpec((D_MODEL, D_MODEL), lambda j: (0, j)),
            pl.BlockSpec((1, D_MODEL), lambda j: (0, j)),
        ],
        out_specs=pl.BlockSpec((rows, D_MODEL), lambda j: (0, j)),
        out_shape=jax.ShapeDtypeStruct((rows, 3 * D_MODEL), F32),
        compiler_params=_cparams(("arbitrary",)),
        name="ada",
    )(cc, w_ada, b_ada)


def _rope_t(x, cos, sin):
    xs = jnp.concatenate([x[:, 16:32], x[:, 0:16], x[:, 48:64], x[:, 32:48]], axis=1)
    return x * cos + xs * sin


def _fp8_pieces(x):
    hi = x.astype(FP8).astype(F32)
    lo16 = ((x - hi) * FP8_LO_SCALE).astype(FP8).astype(F32)
    hi16 = (hi * (1.0 / FP8_LO_SCALE)).astype(FP8).astype(F32)
    return hi, hi16, lo16


def _inproj_kernel(x_ref, mod_ref, cos_ref, sin_ref, wt_ref, wn_ref, qg_ref, kg_ref,
                   q_ref, k_ref, q8_ref, k8_ref, v_ref, ga_ref, mix_ref, gb_ref):
    tm = x_ref.shape[1]
    x = x_ref[0]
    shift = mod_ref[0, 0:1, :]
    scale = mod_ref[0, 1:2, :]
    h = _layer_norm(x) * (1.0 + scale) + shift
    hb = h.astype(BF16)
    zt = lax.dot_general(wt_ref[...], hb, (((1,), (1,)), ((), ())), preferred_element_type=F32)
    zn = jnp.dot(hb, wn_ref[...], preferred_element_type=F32)
    cos = cos_ref[...][None]
    sin = sin_ref[...][None]

    qz = zt[0:ATT_WIDTH].reshape(N_Q_HEADS, HEAD_DIM, tm)
    qn = qz * lax.rsqrt(jnp.mean(qz * qz, axis=1, keepdims=True) + QK_EPS) * qg_ref[...][None]
    qr = _rope_t(qn, cos, sin) * (HEAD_DIM ** -0.5 * math.log2(math.e))
    q_ref[0] = qr.astype(BF16)
    q_hi, q_hi16, q_lo16 = _fp8_pieces(qr * FP8_Q_SCALE)
    q8_ref[0] = jnp.concatenate([q_hi, q_hi16, q_lo16], axis=1).astype(FP8)

    kz = zt[ATT_WIDTH:ATT_WIDTH + KV_WIDTH].reshape(N_KV_HEADS, HEAD_DIM, tm)
    kn = kz * lax.rsqrt(jnp.mean(kz * kz, axis=1, keepdims=True) + QK_EPS) * kg_ref[...][None]
    kr = _rope_t(kn, cos, sin)
    k_hi, k_hi16, k_lo16 = _fp8_pieces(kr * (1.0 / FP8_Q_SCALE))
    k8 = jnp.concatenate([k_hi, k_lo16, k_hi16], axis=1)
    vz = zt[ATT_WIDTH + KV_WIDTH:QKV_WIDTH].reshape(N_KV_HEADS, HEAD_DIM, tm)
    pad_row = lax.broadcasted_iota(jnp.int32, (V_ROWS - HEAD_DIM, tm), 0)
    ones_pad = jnp.where(pad_row == 0, 1.0, 0.0).astype(BF16)
    for j in range(N_KV_HEADS):
        k_ref[0, j, 0] = kr[j].T.astype(BF16)
        k8_ref[0, j, 0] = k8[j].T.astype(FP8)
        v_ref[0, j, 0, 0:HEAD_DIM, :] = vz[j].astype(BF16)
        v_ref[0, j, 0, HEAD_DIM:V_ROWS, :] = ones_pad

    ga_ref[0] = _silu(zn[:, 0:ATT_WIDTH]).astype(BF16)
    mix_ref[0] = zn[:, ATT_WIDTH:ATT_WIDTH + SHIFT_WIDTH]
    gb_ref[0] = _silu(zn[:, ATT_WIDTH + SHIFT_WIDTH:NAT_WIDTH]).astype(BF16)


def _inproj(x, mod, cos_t, sin_t, w_t, w_n, q_gain, k_gain):
    b, t, _ = x.shape
    tm = TOKEN_TILE
    assert tm == KEY_BLOCK and t % tm == 0
    nt = t // tm
    const = lambda shape: pl.BlockSpec(shape, lambda bi, i: (0,) * len(shape))
    return pl.pallas_call(
        _inproj_kernel,
        grid=(b, nt),
        in_specs=[
            pl.BlockSpec((1, tm, D_MODEL), lambda bi, i: (bi, i, 0)),
            pl.BlockSpec((1, 3, D_MODEL), lambda bi, i: (bi, 0, 0)),
            pl.BlockSpec((HEAD_DIM, tm), lambda bi, i: (0, i)),
            pl.BlockSpec((HEAD_DIM, tm), lambda bi, i: (0, i)),
            const((QKV_WIDTH, D_MODEL)),
            const((D_MODEL, NAT_WIDTH)),
            const((HEAD_DIM, 1)),
            const((HEAD_DIM, 1)),
        ],
        out_specs=[
            pl.BlockSpec((1, N_Q_HEADS, HEAD_DIM, tm), lambda bi, i: (bi, 0, 0, i)),
            pl.BlockSpec((1, N_KV_HEADS, 1, tm, HEAD_DIM), lambda bi, i: (bi, 0, i, 0, 0)),
            pl.BlockSpec((1, N_Q_HEADS, FP8_DEPTH, tm), lambda bi, i: (bi, 0, 0, i)),
            pl.BlockSpec((1, N_KV_HEADS, 1, tm, FP8_DEPTH), lambda bi, i: (bi, 0, i, 0, 0)),
            pl.BlockSpec((1, N_KV_HEADS, 1, V_ROWS, tm), lambda bi, i: (bi, 0, i, 0, 0)),
            pl.BlockSpec((1, tm, ATT_WIDTH), lambda bi, i: (bi, i, 0)),
            pl.BlockSpec((1, tm, SHIFT_WIDTH), lambda bi, i: (bi, i, 0)),
            pl.BlockSpec((1, tm, RWKV_WIDTH), lambda bi, i: (bi, i, 0)),
        ],
        out_shape=[
            jax.ShapeDtypeStruct((b, N_Q_HEADS, HEAD_DIM, t), BF16),
            jax.ShapeDtypeStruct((b, N_KV_HEADS, nt, tm, HEAD_DIM), BF16),
            jax.ShapeDtypeStruct((b, N_Q_HEADS, FP8_DEPTH, t), FP8),
            jax.ShapeDtypeStruct((b, N_KV_HEADS, nt, tm, FP8_DEPTH), FP8),
            jax.ShapeDtypeStruct((b, N_KV_HEADS, nt, V_ROWS, tm), BF16),
            jax.ShapeDtypeStruct((b, t, ATT_WIDTH), BF16),
            jax.ShapeDtypeStruct((b, t, SHIFT_WIDTH), F32),
            jax.ShapeDtypeStruct((b, t, RWKV_WIDTH), BF16),
        ],
        compiler_params=_cparams(("parallel", "parallel")),
        name="inproj",
    )(x, mod, cos_t, sin_t, w_t, w_n, q_gain, k_gain)


def _attn_kernel(q_ref, k_ref, v_ref, o_ref, s_sc, cmax_sc, m_sc, acc_sc):
    nkb = k_ref.shape[2]
    tq = q_ref.shape[3]
    width = GQA_GROUP * tq
    nchunk = width // ATTN_COLS
    per_head = tq // ATTN_COLS
    kb = KEY_BLOCK
    kbt = ATTN_UNIT_BLOCKS
    m_sc[...] = jnp.full(m_sc.shape, NEG_BIG, F32)
    acc_sc[...] = jnp.zeros(acc_sc.shape, F32)
    first = nkb % kbt if nkb % kbt else kbt
    nunits = 1 + (nkb - first) // kbt

    def unit_start(u):
        return first + (u - 1) * kbt

    def tick(u, par, nb_acc, nb_score):
        if nb_acc:
            acc0 = 0 if (isinstance(u, int) and u == 1) else unit_start(u - 1)
            m_old = m_sc[...]
            m_new = jnp.maximum(m_old, cmax_sc[1 - par])
            alpha = jnp.exp2(m_old - m_new)
            m_sc[...] = m_new
        if nb_score:
            sc0 = 0 if (isinstance(u, int) and u == 0) else unit_start(u)
        for ch in range(nchunk):
            cols = slice(ch * ATTN_COLS, (ch + 1) * ATTN_COLS)
            if nb_acc:
                pv = None
                for i in range(nb_acc):
                    p = jnp.exp2(s_sc[1 - par, i * kb:(i + 1) * kb, cols] - m_new[:, cols]).astype(BF16)
                    t = jnp.dot(v_ref[0, 0, acc0 + i], p, preferred_element_type=F32)
                    pv = t if pv is None else pv + t
                acc_sc[:, cols] = acc_sc[:, cols] * alpha[:, cols] + pv
            if nb_score:
                g, h = divmod(ch, per_head)
                qc = q_ref[0, g, :, h * ATTN_COLS:(h + 1) * ATTN_COLS]
                cm = None
                for i in range(nb_score):
                    s = jnp.dot(k_ref[0, 0, sc0 + i], qc, preferred_element_type=F32)
                    s_sc[par, i * kb:(i + 1) * kb, cols] = s
                    c1 = jnp.max(s, axis=0, keepdims=True)
                    cm = c1 if cm is None else jnp.maximum(cm, c1)
                cmax_sc[par, :, cols] = cm

    tick(0, 0, 0, first)
    if nunits > 1:
        tick(1, 1, first, kbt)
    nsteady = max(nunits - 2, 0)
    npeel = nsteady % ATTN_UNROLL
    for u in range(2, 2 + npeel):
        tick(u, u % 2, kbt, kbt)
    base = 2 + npeel

    def group(j, carry):
        for i in range(ATTN_UNROLL):
            tick(base + ATTN_UNROLL * j + i, (base + i) % 2, kbt, kbt)
        return carry

    lax.fori_loop(0, (nsteady - npeel) // ATTN_UNROLL, group, 0)
    tick(nunits, nunits % 2, kbt if nunits > 1 else first, 0)
    for g in range(GQA_GROUP):
        acc = acc_sc[:, g * tq:(g + 1) * tq]
        o = acc[0:HEAD_DIM] / acc[HEAD_DIM:HEAD_DIM + 1]
        o_ref[0, :, g * HEAD_DIM:(g + 1) * HEAD_DIM] = o.T


def _attention(q_t, k_blk, v_blk):
    b, _, depth, t = q_t.shape
    nkb = k_blk.shape[2]
    tq = min(Q_TILE, t)
    assert t % tq == 0 and k_blk.shape[4] == depth
    width = GQA_GROUP * tq
    return pl.pallas_call(
        _attn_kernel,
        grid=(b, N_KV_HEADS, t // tq),
        in_specs=[
            pl.BlockSpec((1, GQA_GROUP, depth, tq), lambda bi, j, i: (bi, j, 0, i)),
            pl.BlockSpec((1, 1, nkb, KEY_BLOCK, depth), lambda bi, j, i: (bi, j, 0, 0, 0)),
            pl.BlockSpec((1, 1, nkb, V_ROWS, KEY_BLOCK), lambda bi, j, i: (bi, j, 0, 0, 0)),
        ],
        out_specs=pl.BlockSpec((1, tq, GQA_GROUP * HEAD_DIM), lambda bi, j, i: (bi, i, j)),
        out_shape=jax.ShapeDtypeStruct((b, t, ATT_WIDTH), F32),
        scratch_shapes=[
            pltpu.VMEM((2, ATTN_UNIT_BLOCKS * KEY_BLOCK, width), F32),
            pltpu.VMEM((2, 1, width), F32),
            pltpu.VMEM((1, width), F32),
            pltpu.VMEM((V_ROWS, width), F32),
        ],
        compiler_params=_cparams(("parallel", "parallel", "parallel")),
        name="attn",
    )(q_t, k_blk, v_blk)


def _heads(x):
    return jnp.stack([x[:, h * RWKV_HEAD:(h + 1) * RWKV_HEAD] for h in range(RWKV_HEADS)], axis=0)


def _bmm(a, b):
    return jnp.einsum('hij,hjk->hik', a.astype(BF16), b.astype(BF16), preferred_element_type=F32)


def _bmm_nt(a, b):
    return jnp.einsum('hik,hjk->hij', a.astype(BF16), b.astype(BF16), preferred_element_type=F32)


def _softplus(y):
    return jnp.maximum(y, 0.0) + jnp.log(1.0 + jnp.exp(-jnp.abs(y)))


def _wkvprep_kernel(mix_ref, prev_ref, next_ref, mup_ref, mun_ref, w0_ref, wup_ref, a0_ref, aup_ref,
                    kk_ref, ka_ref, rk_ref, seg_ref,
                    bonus_ref, qeff_ref, o0_ref, g_ref, h_ref,
                    r_sc, v_sc, kn_sc, ld_sc, kd_sc, b_sc):
    tm = mix_ref.shape[1]
    i = pl.program_id(1)
    n = pl.num_programs(1)
    z = mix_ref[0]
    prev_row = jnp.where(i > 0, prev_ref[0, 7:8, :], 0.0)
    next_row = jnp.where(i < n - 1, next_ref[0, 0:1, :], 0.0)
    rows = lax.broadcasted_iota(jnp.int32, (tm, 1), 0)
    zp = jnp.where(rows == 0, prev_row, pltpu.roll(z, 1, 0))
    zn = jnp.where(rows == tm - 1, next_row, pltpu.roll(z, tm - 1, 0))
    zs = z + mup_ref[...] * (zp - z) + mun_ref[...] * (zn - z)

    w = RWKV_WIDTH
    r = zs[:, 0:w]
    k = zs[:, w:2 * w]
    v = zs[:, 2 * w:3 * w]
    seg = seg_ref[...]
    kkr = k * kk_ref[...]
    norm = jnp.sqrt(_dot_split_lhs(kkr * kkr, seg, 2))
    kn = kkr / jnp.maximum(norm, 1e-12)
    r_sc[...] = r
    v_sc[...] = v
    kn_sc[...] = kn
    coef = jnp.zeros((tm, w), F32)
    for d in range(2):
        wl = zs[:, 3 * w + d * W_LORA:3 * w + (d + 1) * W_LORA]
        al = zs[:, 3 * w + 2 * W_LORA + d * A_LORA:3 * w + 2 * W_LORA + (d + 1) * A_LORA]
        w_pre = w0_ref[d:d + 1, :] + jnp.dot(jnp.tanh(wl).astype(BF16), wup_ref[d], preferred_element_type=F32)
        w_log = -_softplus(-w_pre) - 0.5
        ld_sc[d] = -jnp.exp(w_log)
        a = jax.nn.sigmoid(a0_ref[d:d + 1, :] + jnp.dot(al.astype(BF16), aup_ref[d], preferred_element_type=F32))
        kd = k * (1.0 + (a - 1.0) * ka_ref[...])
        kd_sc[d] = kd
        b_sc[d] = kn * a
        coef = coef + _dot_split_lhs(r * kd * rk_ref[d:d + 1, :], seg, 2)
    bonus_ref[0] = coef * v

    ri = lax.broadcasted_iota(jnp.int32, (CHUNK, CHUNK), 0)
    ci = lax.broadcasted_iota(jnp.int32, (CHUNK, CHUNK), 1)
    eye_f = jnp.where(ri == ci, 1.0, 0.0)
    strict_f = [jnp.where(ci < ri, 1.0, 0.0), jnp.where(ci > ri, 1.0, 0.0)]
    incl_f = [s + eye_f for s in strict_f]
    nh = RWKV_HEADS

    def group(gi, carry):
        combos = [(gi * WKV_GROUP + u, d) for u in range(WKV_GROUP) for d in range(2)]
        slices = []
        parts = {name: [] for name in ("rh", "kkh", "kt", "bt", "kv", "bv", "gam", "vh")}
        for c, d in combos:
            sl = pl.ds(pl.multiple_of(c * CHUNK, CHUNK), CHUNK)
            slices.append(sl)
            ld = ld_sc[d, sl, :]
            cum = _dot_split_rhs(incl_f[d].astype(BF16), ld, 3)
            tot = cum[CHUNK - 1:CHUNK, :] if d == 0 else cum[0:1, :]
            e_pos = jnp.exp(cum)
            e_prev = jnp.exp(cum - ld)
            e_neg = jnp.exp(-cum)
            e_rest = jnp.exp(tot - cum)
            kd_c = kd_sc[d, sl, :]
            b_c = b_sc[d, sl, :]
            parts["rh"].append(_heads(r_sc[sl, :] * e_pos))
            parts["kkh"].append(_heads(kn_sc[sl, :] * e_prev))
            parts["kt"].append(_heads(kd_c * e_neg))
            parts["bt"].append(_heads(b_c * e_neg))
            parts["kv"].append(_heads(kd_c * e_rest))
            parts["bv"].append(_heads(b_c * e_rest))
            parts["gam"].append(_heads(jnp.exp(tot)))
            parts["vh"].append(_heads(v_sc[sl, :]))
        cat = lambda xs: jnp.concatenate(xs, axis=0)
        rh, kkh, kt, bt, kv, bv, gam, vh = (cat(parts[n]) for n in ("rh", "kkh", "kt", "bt", "kv", "bv", "gam", "vh"))
        per = lambda ms: cat([jnp.broadcast_to(ms[d][None], (nh, CHUNK, CHUNK)) for _, d in combos])
        strict = per(strict_f)
        incl = per(incl_f)

        kb_rows = jnp.concatenate([kt, bt], axis=1)
        kk_kb = _bmm_nt(kkh, kb_rows)
        r_kb = _bmm_nt(rh, kb_rows)
        lk = kk_kb[:, :, 0:CHUNK] * strict
        lb = kk_kb[:, :, CHUNK:2 * CHUNK] * strict
        ark = r_kb[:, :, 0:CHUNK] * incl
        arb = r_kb[:, :, CHUNK:2 * CHUNK] * incl
        xp = -lb
        tinv = eye_f[None] + xp
        span = 2
        while span < CHUNK:
            xp = _bmm(xp, xp)
            tinv = tinv + _bmm(tinv, xp)
            span *= 2
        wu = _bmm(tinv, jnp.concatenate([kkh, _bmm(lk, vh)], axis=2))
        arb_wu = _bmm(arb, wu)
        qeff = rh - arb_wu[:, :, 0:RWKV_HEAD]
        o0 = _bmm(ark, vh) - arb_wu[:, :, RWKV_HEAD:2 * RWKV_HEAD]
        bvt = jnp.swapaxes(bv, 1, 2)
        kvt = jnp.swapaxes(kv, 1, 2)
        bvt_wu = _bmm(bvt, wu)
        gmat = eye_f[None] * gam - bvt_wu[:, :, 0:RWKV_HEAD]
        hmat = _bmm(kvt, vh) - bvt_wu[:, :, RWKV_HEAD:2 * RWKV_HEAD]
        for idx, (c, d) in enumerate(combos):
            rng = slice(idx * nh, (idx + 1) * nh)
            qeff_ref[d, 0, :, slices[idx], :] = qeff[rng].astype(BF16)
            o0_ref[d, 0, :, slices[idx], :] = o0[rng]
            g_ref[d, 0, c] = gmat[rng].astype(BF16)
            h_ref[d, 0, c] = hmat[rng]
        return carry

    lax.fori_loop(0, tm // (CHUNK * WKV_GROUP), group, 0)


def _wkvprep(mix, mu_prev, mu_next, w0, w_up, a0, a_up, k_k, k_a, r_k, seg):
    b, t, _ = mix.shape
    tm = min(TOKEN_TILE, t)
    assert t % tm == 0 and tm % CHUNK == 0
    nt = t // tm
    nc = tm // CHUNK
    hb = tm // 8
    last8 = t // 8 - 1
    const = lambda shape: pl.BlockSpec(shape, lambda bi, i: (0,) * len(shape))
    hd = (2, 1, RWKV_HEADS, tm, RWKV_HEAD)
    st = (2, 1, nc, RWKV_HEADS, RWKV_HEAD, RWKV_HEAD)
    return pl.pallas_call(
        _wkvprep_kernel,
        grid=(b, nt),
        in_specs=[
            pl.BlockSpec((1, tm, SHIFT_WIDTH), lambda bi, i: (bi, i, 0)),
            pl.BlockSpec((1, 8, SHIFT_WIDTH), lambda bi, i: (bi, jnp.maximum(i * hb - 1, 0), 0)),
            pl.BlockSpec((1, 8, SHIFT_WIDTH), lambda bi, i: (bi, jnp.minimum((i + 1) * hb, last8), 0)),
            const((1, SHIFT_WIDTH)),
            const((1, SHIFT_WIDTH)),
            const((2, RWKV_WIDTH)),
            const((2, W_LORA, RWKV_WIDTH)),
            const((2, RWKV_WIDTH)),
            const((2, A_LORA, RWKV_WIDTH)),
            const((1, RWKV_WIDTH)),
            const((1, RWKV_WIDTH)),
            const((2, RWKV_WIDTH)),
            const((RWKV_WIDTH, RWKV_WIDTH)),
        ],
        out_specs=[
            pl.BlockSpec((1, tm, RWKV_WIDTH), lambda bi, i: (bi, i, 0)),
            pl.BlockSpec(hd, lambda bi, i: (0, bi, 0, i, 0)),
            pl.BlockSpec(hd, lambda bi, i: (0, bi, 0, i, 0)),
            pl.BlockSpec(st, lambda bi, i: (0, bi, i, 0, 0, 0)),
            pl.BlockSpec(st, lambda bi, i: (0, bi, i, 0, 0, 0)),
        ],
        out_shape=[
            jax.ShapeDtypeStruct((b, t, RWKV_WIDTH), F32),
            jax.ShapeDtypeStruct((2, b, RWKV_HEADS, t, RWKV_HEAD), BF16),
            jax.ShapeDtypeStruct((2, b, RWKV_HEADS, t, RWKV_HEAD), F32),
            jax.ShapeDtypeStruct((2, b, t // CHUNK, RWKV_HEADS, RWKV_HEAD, RWKV_HEAD), BF16),
            jax.ShapeDtypeStruct((2, b, t // CHUNK, RWKV_HEADS, RWKV_HEAD, RWKV_HEAD), F32),
        ],
        scratch_shapes=[
            pltpu.VMEM((tm, RWKV_WIDTH), F32),
            pltpu.VMEM((tm, RWKV_WIDTH), F32),
            pltpu.VMEM((tm, RWKV_WIDTH), F32),
            pltpu.VMEM((2, tm, RWKV_WIDTH), F32),
            pltpu.VMEM((2, tm, RWKV_WIDTH), F32),
            pltpu.VMEM((2, tm, RWKV_WIDTH), F32),
        ],
        compiler_params=_cparams(("parallel", "parallel")),
        name="wkvprep",
    )(mix, mix, mix, mu_prev, mu_next, w0, w_up, a0, a_up, k_k, k_a, r_k, seg)


def _wkvscan_kernel(qf_ref, qb_ref, of_ref, ob_ref, gf_ref, gb_ref, hf_ref, hb_ref, s0_ref,
                    outf_ref, outb_ref, sfin_ref, s_sc):
    c = pl.program_id(0)
    nb = qf_ref.shape[1]
    nc = gf_ref.shape[2]
    nh = RWKV_HEADS

    @pl.when(c == 0)
    def _():
        s_sc[...] = s0_ref[...].reshape(s_sc.shape)

    def bdot(a, s_parts):
        ab = a.astype(BF16)
        return sum(jnp.einsum('hij,hjk->hik', ab, p, preferred_element_type=F32) for p in s_parts)

    for j in range(nc):
        jb = nc - 1 - j
        rf = slice(j * CHUNK, (j + 1) * CHUNK)
        rb = slice(jb * CHUNK, (jb + 1) * CHUNK)
        s = _split_bf16(s_sc[...], 2)
        qe = jnp.concatenate([qf_ref[0, bi, :, rf, :] for bi in range(nb)]
                             + [qb_ref[0, bi, :, rb, :] for bi in range(nb)], axis=0)
        o0 = jnp.concatenate([of_ref[0, bi, :, rf, :] for bi in range(nb)]
                             + [ob_ref[0, bi, :, rb, :] for bi in range(nb)], axis=0)
        gm = jnp.concatenate([gf_ref[0, bi, j] for bi in range(nb)] + [gb_ref[0, bi, jb] for bi in range(nb)], axis=0)
        hm = jnp.concatenate([hf_ref[0, bi, j] for bi in range(nb)] + [hb_ref[0, bi, jb] for bi in range(nb)], axis=0)
        o = bdot(qe, s) + o0
        s_sc[...] = bdot(gm, s) + hm
        for bi in range(nb):
            for hh in range(nh):
                outf_ref[bi, rf, hh * RWKV_HEAD:(hh + 1) * RWKV_HEAD] = o[bi * nh + hh]
                outb_ref[bi, rb, hh * RWKV_HEAD:(hh + 1) * RWKV_HEAD] = o[(nb + bi) * nh + hh]

    @pl.when(c == pl.num_programs(0) - 1)
    def _():
        sfin_ref[...] = s_sc[...].reshape(sfin_ref.shape)


def _wkvscan(qeff, o0, g, h, s0):
    _, b, _, t, _ = qeff.shape
    tc = min(TOKEN_TILE, t)
    nc = tc // CHUNK
    nt = t // tc
    hd = (1, b, RWKV_HEADS, tc, RWKV_HEAD)
    st = (1, b, nc, RWKV_HEADS, RWKV_HEAD, RWKV_HEAD)
    fwd_hd = pl.BlockSpec(hd, lambda c: (0, 0, 0, c, 0))
    bwd_hd = pl.BlockSpec(hd, lambda c: (1, 0, 0, nt - 1 - c, 0))
    fwd_st = pl.BlockSpec(st, lambda c: (0, 0, c, 0, 0, 0))
    bwd_st = pl.BlockSpec(st, lambda c: (1, 0, nt - 1 - c, 0, 0, 0))
    s_spec = pl.BlockSpec((2, b, RWKV_HEADS, RWKV_HEAD, RWKV_HEAD), lambda c: (0, 0, 0, 0, 0))
    return pl.pallas_call(
        _wkvscan_kernel,
        grid=(nt,),
        in_specs=[fwd_hd, bwd_hd, fwd_hd, bwd_hd, fwd_st, bwd_st, fwd_st, bwd_st, s_spec],
        out_specs=[
            pl.BlockSpec((b, tc, RWKV_WIDTH), lambda c: (0, c, 0)),
            pl.BlockSpec((b, tc, RWKV_WIDTH), lambda c: (0, nt - 1 - c, 0)),
            s_spec,
        ],
        out_shape=[
            jax.ShapeDtypeStruct((b, t, RWKV_WIDTH), F32),
            jax.ShapeDtypeStruct((b, t, RWKV_WIDTH), F32),
            jax.ShapeDtypeStruct((2, b, RWKV_HEADS, RWKV_HEAD, RWKV_HEAD), F32),
        ],
        scratch_shapes=[pltpu.VMEM((2 * b * RWKV_HEADS, RWKV_HEAD, RWKV_HEAD), F32)],
        compiler_params=_cparams(("arbitrary",)),
        name="wkvscan",
    )(qeff, qeff, o0, o0, g, g, h, h, s0)


def _merge_kernel(x_ref, mod_ref, ya_ref, ga_ref, of_ref, ob_ref, bonus_ref, gb_ref,
                  wpa_ref, wpb_ref, wmg_ref, bmg_ref, wo_ref, gnw_ref, gnb_ref, lng_ref, lnb_ref, seg_ref,
                  out_ref):
    x = x_ref[0]
    shift = mod_ref[0, 0:1, :]
    scale = mod_ref[0, 1:2, :]
    gate = mod_ref[0, 2:3, :]
    h = _layer_norm(x) * (1.0 + scale) + shift
    gates = jax.nn.sigmoid(jnp.dot(h.astype(BF16), wmg_ref[...], preferred_element_type=F32) + bmg_ref[...])
    ya = jnp.dot((ya_ref[0] * ga_ref[0].astype(F32)).astype(BF16), wpa_ref[...], preferred_element_type=F32)

    o = of_ref[0] + ob_ref[0]
    seg = seg_ref[...]
    mu = _dot_split_lhs(o, seg, 2) * (1.0 / RWKV_HEAD)
    oc = o - mu
    var = _dot_split_lhs(oc * oc, seg, 2) * (1.0 / RWKV_HEAD)
    y = oc * lax.rsqrt(var + GN_EPS) * gnw_ref[...] + gnb_ref[...] + bonus_ref[0]
    yb = jnp.dot((y * gb_ref[0].astype(F32)).astype(BF16), wpb_ref[...], preferred_element_type=F32)

    mixed = gates[:, 0:D_MODEL] * ya + gates[:, D_MODEL:2 * D_MODEL] * yb
    out = jnp.dot(mixed.astype(BF16), wo_ref[...], preferred_element_type=F32)
    out_ref[0] = _layer_norm(ALPHA * x + gate * out) * lng_ref[...] + lnb_ref[...]


def _merge(x, mod, y_att, ga, o_f, o_b, bonus, gb, w_pa, w_pb, w_mg, b_mg, w_o, gn_w, gn_b, ln_g, ln_b, seg):
    b, t, _ = x.shape
    tm = min(TOKEN_TILE, t)
    const = lambda shape: pl.BlockSpec(shape, lambda bi, i: (0,) * len(shape))
    tok = lambda width: pl.BlockSpec((1, tm, width), lambda bi, i: (bi, i, 0))
    return pl.pallas_call(
        _merge_kernel,
        grid=(b, t // tm),
        in_specs=[
            tok(D_MODEL),
            pl.BlockSpec((1, 3, D_MODEL), lambda bi, i: (bi, 0, 0)),
            tok(ATT_WIDTH),
            tok(ATT_WIDTH),
            tok(RWKV_WIDTH),
            tok(RWKV_WIDTH),
            tok(RWKV_WIDTH),
            tok(RWKV_WIDTH),
            const((ATT_WIDTH, D_MODEL)),
            const((RWKV_WIDTH, D_MODEL)),
            const((D_MODEL, 2 * D_MODEL)),
            const((1, 2 * D_MODEL)),
            const((D_MODEL, D_MODEL)),
            const((1, RWKV_WIDTH)),
            const((1, RWKV_WIDTH)),
            const((1, D_MODEL)),
            const((1, D_MODEL)),
            const((RWKV_WIDTH, RWKV_WIDTH)),
        ],
        out_specs=tok(D_MODEL),
        out_shape=jax.ShapeDtypeStruct((b, t, D_MODEL), F32),
        compiler_params=_cparams(("parallel", "parallel")),
        name="merge",
    )(x, mod, y_att, ga, o_f, o_b, bonus, gb, w_pa, w_pb, w_mg, b_mg, w_o, gn_w, gn_b, ln_g, ln_b, seg)


def _rope_tables_t(t):
    pos = jnp.arange(t, dtype=jnp.int32)
    row = (pos // GRID_W).astype(F32)
    col = (pos % GRID_W).astype(F32)
    inv = ROPE_BASE ** (-jnp.arange(ROPE_FREQS, dtype=F32) / ROPE_FREQS)
    ang_r = inv[:, None] * row[None, :]
    ang_c = inv[:, None] * col[None, :]
    cos_t = jnp.concatenate([jnp.cos(ang_r), jnp.cos(ang_r), jnp.cos(ang_c), jnp.cos(ang_c)], axis=0)
    sin_t = jnp.concatenate([-jnp.sin(ang_r), jnp.sin(ang_r), -jnp.sin(ang_c), jnp.sin(ang_c)], axis=0)
    return cos_t, sin_t


def kernel(x, c, ctx, c_ctx, w_ada, b_ada, w_in, q_norm, k_norm, mu_prev, mu_next, w0, w_up, a0, a_up,
           k_k, k_a, r_k, gn_w, gn_b, w_pa, w_pb, w_mg, b_mg, w_o, ln_g, ln_b):
    b, t, _ = x.shape
    tc = ctx.shape[1]
    l = 0
    seg = jnp.kron(jnp.eye(RWKV_HEADS, dtype=F32), jnp.ones((RWKV_HEAD, RWKV_HEAD), F32)).astype(BF16)

    rows = 8 * ((b + 1 + 7) // 8)
    cc = jnp.zeros((rows, D_MODEL), F32).at[0:b].set(c).at[b].set(c_ctx)
    ada = _ada(cc, w_ada[l], b_ada[l][None, :])
    mod_x = ada[0:b].reshape(b, 3, D_MODEL)
    mod_c = jnp.broadcast_to(ada[b].reshape(1, 3, D_MODEL), (b, 3, D_MODEL))

    w_t = w_in[l][:, 0:QKV_WIDTH].T.astype(BF16)
    w_n = w_in[l][:, QKV_WIDTH:].astype(BF16)
    q_gain = q_norm[l][:, None]
    k_gain = k_norm[l][:, None]
    cos_x, sin_x = _rope_tables_t(t)
    cos_c = jnp.ones((HEAD_DIM, tc), F32)
    sin_c = jnp.zeros((HEAD_DIM, tc), F32)

    q_x, k_x, q8_x, k8_x, v_x, ga_x, mix_x, gb_x = _inproj(x, mod_x, cos_x, sin_x, w_t, w_n, q_gain, k_gain)
    _, k_c, _, k8_c, v_c, _, mix_c, _ = _inproj(ctx, mod_c, cos_c, sin_c, w_t, w_n, q_gain, k_gain)

    v_all = jnp.concatenate([v_c, v_x], axis=2)
    fp8_ok = jnp.logical_and(
        jnp.max(jnp.abs(q_norm[l])) * (2.0 * HEAD_DIM ** 0.5 * HEAD_DIM ** -0.5 * math.log2(math.e) * FP8_Q_SCALE) < FP8_MAX,
        jnp.max(jnp.abs(k_norm[l])) * (2.0 * HEAD_DIM ** 0.5 / FP8_Q_SCALE) < FP8_MAX)
    y_att = lax.cond(
        fp8_ok,
        lambda: _attention(q8_x, jnp.concatenate([k8_c, k8_x], axis=2), v_all),
        lambda: _attention(q_x, jnp.concatenate([k_c, k_x], axis=2), v_all))

    prep_args = (mu_prev[l][None, :], mu_next[l][None, :], w0[l], w_up[l].astype(BF16), a0[l], a_up[l].astype(BF16),
                 k_k[l][None, :], k_a[l][None, :], r_k[l].reshape(2, RWKV_WIDTH), seg)
    _, qe_c, o0_c, g_c, h_c = _wkvprep(mix_c, *prep_args)
    zero_state = jnp.zeros((2, b, RWKV_HEADS, RWKV_HEAD, RWKV_HEAD), F32)
    _, _, s_ctx = _wkvscan(qe_c, o0_c, g_c, h_c, zero_state)
    bonus, qe_x, o0_x, g_x, h_x = _wkvprep(mix_x, *prep_args)
    of_x, ob_x, _ = _wkvscan(qe_x, o0_x, g_x, h_x, s_ctx)

    return _merge(x, mod_x, y_att, ga_x, of_x, ob_x, bonus, gb_x,
                  w_pa[l].astype(BF16), w_pb[l].astype(BF16), w_mg[l].astype(BF16), b_mg[l][None, :],
                  w_o[l].astype(BF16), gn_w[l][None, :], gn_b[l][None, :], ln_g[l][None, :], ln_b[l][None, :], seg)
```

```python
import functools
import math

import jax
import jax.numpy as jnp
from jax import lax
from jax.experimental import pallas as pl
from jax.experimental.pallas import tpu as pltpu

F32 = jnp.float32
BF16 = jnp.bfloat16
FP8 = jnp.float8_e4m3fn
HIGHEST = lax.Precision.HIGHEST

D_MODEL = 1024
GRID_W = 64
HEAD_DIM = 64
N_Q_HEADS = 8
N_KV_HEADS = 2
GQA_GROUP = N_Q_HEADS // N_KV_HEADS
ATT_WIDTH = N_Q_HEADS * HEAD_DIM
KV_WIDTH = N_KV_HEADS * HEAD_DIM
QKV_WIDTH = ATT_WIDTH + 2 * KV_WIDTH
ROPE_FREQS = HEAD_DIM // 4
ROPE_BASE = 10000.0
RWKV_HEAD = 64
RWKV_HEADS = 8
RWKV_WIDTH = RWKV_HEADS * RWKV_HEAD
W_LORA = 64
A_LORA = 64
SHIFT_WIDTH = 3 * RWKV_WIDTH + 2 * W_LORA + 2 * A_LORA
IN_WIDTH = 2 * ATT_WIDTH + 2 * KV_WIDTH + SHIFT_WIDTH + RWKV_WIDTH
NAT_WIDTH = IN_WIDTH - QKV_WIDTH
LN_EPS = 1e-5
QK_EPS = 1e-6
GN_EPS = 64e-5
DEPTH = 1
ALPHA = (2.0 * DEPTH) ** 0.25

CHUNK = 64
KEY_BLOCK = 256
V_ROWS = HEAD_DIM + 8
FP8_DEPTH = 3 * HEAD_DIM
FP8_LO_SCALE = 16.0
FP8_Q_SCALE = 2.0
FP8_MAX = 448.0
TOKEN_TILE = 256
Q_TILE = 512
ATTN_COLS = 256
ATTN_UNIT_BLOCKS = 4
ATTN_UNROLL = 2
WKV_GROUP = 2
HEAD_PAIRS = RWKV_HEADS // 2
PAIR_WIDTH = 2 * RWKV_HEAD
PAIR_SLAB = (HEAD_PAIRS, RWKV_HEAD, PAIR_WIDTH)
VMEM_LIMIT = 48 * 1024 * 1024
NEG_BIG = -1e30


def _cparams(sem):
    return pltpu.CompilerParams(dimension_semantics=sem, vmem_limit_bytes=VMEM_LIMIT)


def _silu(x):
    return x * jax.nn.sigmoid(x)


def _layer_norm(x):
    mu = jnp.mean(x, axis=-1, keepdims=True)
    xc = x - mu
    var = jnp.mean(xc * xc, axis=-1, keepdims=True)
    return xc * lax.rsqrt(var + LN_EPS)


def _split_bf16(x, parts):
    out = []
    rem = x
    for _ in range(parts):
        p = rem.astype(BF16)
        out.append(p)
        rem = rem - p.astype(F32)
    return out


def _dot_split_lhs(x, m_bf16, parts):
    acc = None
    for p in _split_bf16(x, parts):
        t = jnp.dot(p, m_bf16, preferred_element_type=F32)
        acc = t if acc is None else acc + t
    return acc


def _dot_split_rhs(m_bf16, x, parts):
    acc = None
    for p in _split_bf16(x, parts):
        t = jnp.dot(m_bf16, p, preferred_element_type=F32)
        acc = t if acc is None else acc + t
    return acc


def _ada_kernel(c_ref, w_ref, b_ref, o_ref):
    s = _silu(c_ref[...])
    o_ref[...] = jnp.dot(s, w_ref[...], precision=HIGHEST, preferred_element_type=F32) + b_ref[...]


def _ada(cc, w_ada, b_ada):
    rows = cc.shape[0]
    nblk = (3 * D_MODEL) // D_MODEL
    return pl.pallas_call(
        _ada_kernel,
        grid=(nblk,),
        in_specs=[
            pl.BlockSpec((rows, D_MODEL), lambda j: (0, 0)),
            pl.BlockSpec((D_MODEL, D_MODEL), lambda j: (0, j)),
            pl.BlockSpec((1, D_MODEL), lambda j: (0, j)),
        ],
        out_specs=pl.BlockSpec((rows, D_MODEL), lambda j: (0, j)),
        out_shape=jax.ShapeDtypeStruct((rows, 3 * D_MODEL), F32),
        compiler_params=_cparams(("arbitrary",)),
        name="ada",
    )(cc, w_ada, b_ada)


def _rope_t(x, cos, sin):
    xs = jnp.concatenate([x[:, 16:32], x[:, 0:16], x[:, 48:64], x[:, 32:48]], axis=1)
    return x * cos + xs * sin


def _fp8_pieces(x):
    hi = x.astype(FP8).astype(F32)
    lo16 = ((x - hi) * FP8_LO_SCALE).astype(FP8).astype(F32)
    hi16 = (hi * (1.0 / FP8_LO_SCALE)).astype(FP8).astype(F32)
    return hi, hi16, lo16


def _inproj_kernel(x_ref, mod_ref, cos_ref, sin_ref, wt_ref, wn_ref, qg_ref, kg_ref,
                   q_ref, k_ref, q8_ref, k8_ref, v_ref, ga_ref, mix_ref, gb_ref):
    tm = x_ref.shape[1]
    x = x_ref[0]
    shift = mod_ref[0, 0:1, :]
    scale = mod_ref[0, 1:2, :]
    h = _layer_norm(x) * (1.0 + scale) + shift
    hb = h.astype(BF16)
    zt = lax.dot_general(wt_ref[...], hb, (((1,), (1,)), ((), ())), preferred_element_type=F32)
    zn = jnp.dot(hb, wn_ref[...], preferred_element_type=F32)
    cos = cos_ref[...][None]
    sin = sin_ref[...][None]

    qz = zt[0:ATT_WIDTH].reshape(N_Q_HEADS, HEAD_DIM, tm)
    qn = qz * lax.rsqrt(jnp.mean(qz * qz, axis=1, keepdims=True) + QK_EPS) * qg_ref[...][None]
    qr = _rope_t(qn, cos, sin) * (HEAD_DIM ** -0.5 * math.log2(math.e))
    q_ref[0] = qr.astype(BF16)
    q_hi, q_hi16, q_lo16 = _fp8_pieces(qr * FP8_Q_SCALE)
    q8_ref[0] = jnp.concatenate([q_hi, q_hi16, q_lo16], axis=1).astype(FP8)

    kz = zt[ATT_WIDTH:ATT_WIDTH + KV_WIDTH].reshape(N_KV_HEADS, HEAD_DIM, tm)
    kn = kz * lax.rsqrt(jnp.mean(kz * kz, axis=1, keepdims=True) + QK_EPS) * kg_ref[...][None]
    kr = _rope_t(kn, cos, sin)
    k_hi, k_hi16, k_lo16 = _fp8_pieces(kr * (1.0 / FP8_Q_SCALE))
    k8 = jnp.concatenate([k_hi, k_lo16, k_hi16], axis=1)
    vz = zt[ATT_WIDTH + KV_WIDTH:QKV_WIDTH].reshape(N_KV_HEADS, HEAD_DIM, tm)
    pad_row = lax.broadcasted_iota(jnp.int32, (V_ROWS - HEAD_DIM, tm), 0)
    ones_pad = jnp.where(pad_row == 0, 1.0, 0.0).astype(BF16)
    for j in range(N_KV_HEADS):
        k_ref[0, j, 0] = kr[j].T.astype(BF16)
        k8_ref[0, j, 0] = k8[j].T.astype(FP8)
        v_ref[0, j, 0, 0:HEAD_DIM, :] = vz[j].astype(BF16)
        v_ref[0, j, 0, HEAD_DIM:V_ROWS, :] = ones_pad

    ga_ref[0] = _silu(zn[:, 0:ATT_WIDTH]).astype(BF16)
    mix_ref[0] = zn[:, ATT_WIDTH:ATT_WIDTH + SHIFT_WIDTH]
    gb_ref[0] = _silu(zn[:, ATT_WIDTH + SHIFT_WIDTH:NAT_WIDTH]).astype(BF16)


def _inproj(x, mod, cos_t, sin_t, w_t, w_n, q_gain, k_gain):
    b, t, _ = x.shape
    tm = TOKEN_TILE
    assert tm == KEY_BLOCK and t % tm == 0
    nt = t // tm
    const = lambda shape: pl.BlockSpec(shape, lambda bi, i: (0,) * len(shape))
    return pl.pallas_call(
        _inproj_kernel,
        grid=(b, nt),
        in_specs=[
            pl.BlockSpec((1, tm, D_MODEL), lambda bi, i: (bi, i, 0)),
            pl.BlockSpec((1, 3, D_MODEL), lambda bi, i: (bi, 0, 0)),
            pl.BlockSpec((HEAD_DIM, tm), lambda bi, i: (0, i)),
            pl.BlockSpec((HEAD_DIM, tm), lambda bi, i: (0, i)),
            const((QKV_WIDTH, D_MODEL)),
            const((D_MODEL, NAT_WIDTH)),
            const((HEAD_DIM, 1)),
            const((HEAD_DIM, 1)),
        ],
        out_specs=[
            pl.BlockSpec((1, N_Q_HEADS, HEAD_DIM, tm), lambda bi, i: (bi, 0, 0, i)),
            pl.BlockSpec((1, N_KV_HEADS, 1, tm, HEAD_DIM), lambda bi, i: (bi, 0, i, 0, 0)),
            pl.BlockSpec((1, N_Q_HEADS, FP8_DEPTH, tm), lambda bi, i: (bi, 0, 0, i)),
            pl.BlockSpec((1, N_KV_HEADS, 1, tm, FP8_DEPTH), lambda bi, i: (bi, 0, i, 0, 0)),
            pl.BlockSpec((1, N_KV_HEADS, 1, V_ROWS, tm), lambda bi, i: (bi, 0, i, 0, 0)),
            pl.BlockSpec((1, tm, ATT_WIDTH), lambda bi, i: (bi, i, 0)),
            pl.BlockSpec((1, tm, SHIFT_WIDTH), lambda bi, i: (bi, i, 0)),
            pl.BlockSpec((1, tm, RWKV_WIDTH), lambda bi, i: (bi, i, 0)),
        ],
        out_shape=[
            jax.ShapeDtypeStruct((b, N_Q_HEADS, HEAD_DIM, t), BF16),
            jax.ShapeDtypeStruct((b, N_KV_HEADS, nt, tm, HEAD_DIM), BF16),
            jax.ShapeDtypeStruct((b, N_Q_HEADS, FP8_DEPTH, t), FP8),
            jax.ShapeDtypeStruct((b, N_KV_HEADS, nt, tm, FP8_DEPTH), FP8),
            jax.ShapeDtypeStruct((b, N_KV_HEADS, nt, V_ROWS, tm), BF16),
            jax.ShapeDtypeStruct((b, t, ATT_WIDTH), BF16),
            jax.ShapeDtypeStruct((b, t, SHIFT_WIDTH), F32),
            jax.ShapeDtypeStruct((b, t, RWKV_WIDTH), BF16),
        ],
        compiler_params=_cparams(("parallel", "parallel")),
        name="inproj",
    )(x, mod, cos_t, sin_t, w_t, w_n, q_gain, k_gain)


def _attn_kernel(q_ref, k_ref, v_ref, o_ref, s_sc, cmax_sc, m_sc, acc_sc):
    nkb = k_ref.shape[2]
    tq = q_ref.shape[3]
    width = GQA_GROUP * tq
    nchunk = width // ATTN_COLS
    per_head = tq // ATTN_COLS
    kb = KEY_BLOCK
    kbt = ATTN_UNIT_BLOCKS
    m_sc[...] = jnp.full(m_sc.shape, NEG_BIG, F32)
    acc_sc[...] = jnp.zeros(acc_sc.shape, F32)
    first = nkb % kbt if nkb % kbt else kbt
    nunits = 1 + (nkb - first) // kbt

    def unit_start(u):
        return first + (u - 1) * kbt

    def tick(u, par, nb_acc, nb_score):
        if nb_acc:
            acc0 = 0 if (isinstance(u, int) and u == 1) else unit_start(u - 1)
            m_old = m_sc[...]
            m_new = jnp.maximum(m_old, cmax_sc[1 - par])
            alpha = jnp.exp2(m_old - m_new)
            m_sc[...] = m_new
        if nb_score:
            sc0 = 0 if (isinstance(u, int) and u == 0) else unit_start(u)
        for ch in range(nchunk):
            cols = slice(ch * ATTN_COLS, (ch + 1) * ATTN_COLS)
            if nb_acc:
                pv = None
                for i in range(nb_acc):
                    p = jnp.exp2(s_sc[1 - par, i * kb:(i + 1) * kb, cols] - m_new[:, cols]).astype(BF16)
                    t = jnp.dot(v_ref[0, 0, acc0 + i], p, preferred_element_type=F32)
                    pv = t if pv is None else pv + t
                acc_sc[:, cols] = acc_sc[:, cols] * alpha[:, cols] + pv
            if nb_score:
                g, h = divmod(ch, per_head)
                qc = q_ref[0, g, :, h * ATTN_COLS:(h + 1) * ATTN_COLS]
                cm = None
                for i in range(nb_score):
                    s = jnp.dot(k_ref[0, 0, sc0 + i], qc, preferred_element_type=F32)
                    s_sc[par, i * kb:(i + 1) * kb, cols] = s
                    c1 = jnp.max(s, axis=0, keepdims=True)
                    cm = c1 if cm is None else jnp.maximum(cm, c1)
                cmax_sc[par, :, cols] = cm

    tick(0, 0, 0, first)
    if nunits > 1:
        tick(1, 1, first, kbt)
    nsteady = max(nunits - 2, 0)
    npeel = nsteady % ATTN_UNROLL
    for u in range(2, 2 + npeel):
        tick(u, u % 2, kbt, kbt)
    base = 2 + npeel

    def group(j, carry):
        for i in range(ATTN_UNROLL):
            tick(base + ATTN_UNROLL * j + i, (base + i) % 2, kbt, kbt)
        return carry

    lax.fori_loop(0, (nsteady - npeel) // ATTN_UNROLL, group, 0)
    tick(nunits, nunits % 2, kbt if nunits > 1 else first, 0)
    for g in range(GQA_GROUP):
        acc = acc_sc[:, g * tq:(g + 1) * tq]
        o = acc[0:HEAD_DIM] / acc[HEAD_DIM:HEAD_DIM + 1]
        o_ref[0, :, g * HEAD_DIM:(g + 1) * HEAD_DIM] = o.T


def _attention(q_t, k_blk, v_blk):
    b, _, depth, t = q_t.shape
    nkb = k_blk.shape[2]
    tq = min(Q_TILE, t)
    assert t % tq == 0 and k_blk.shape[4] == depth
    width = GQA_GROUP * tq
    return pl.pallas_call(
        _attn_kernel,
        grid=(b, N_KV_HEADS, t // tq),
        in_specs=[
            pl.BlockSpec((1, GQA_GROUP, depth, tq), lambda bi, j, i: (bi, j, 0, i)),
            pl.BlockSpec((1, 1, nkb, KEY_BLOCK, depth), lambda bi, j, i: (bi, j, 0, 0, 0)),
            pl.BlockSpec((1, 1, nkb, V_ROWS, KEY_BLOCK), lambda bi, j, i: (bi, j, 0, 0, 0)),
        ],
        out_specs=pl.BlockSpec((1, tq, GQA_GROUP * HEAD_DIM), lambda bi, j, i: (bi, i, j)),
        out_shape=jax.ShapeDtypeStruct((b, t, ATT_WIDTH), F32),
        scratch_shapes=[
            pltpu.VMEM((2, ATTN_UNIT_BLOCKS * KEY_BLOCK, width), F32),
            pltpu.VMEM((2, 1, width), F32),
            pltpu.VMEM((1, width), F32),
            pltpu.VMEM((V_ROWS, width), F32),
        ],
        compiler_params=_cparams(("parallel", "parallel", "parallel")),
        name="attn",
    )(q_t, k_blk, v_blk)


def _heads(x):
    return jnp.stack([x[:, h * RWKV_HEAD:(h + 1) * RWKV_HEAD] for h in range(RWKV_HEADS)], axis=0)


def _bmm(a, b):
    return jnp.einsum('hij,hjk->hik', a.astype(BF16), b.astype(BF16), preferred_element_type=F32)


def _bmm_nt(a, b):
    return jnp.einsum('hik,hjk->hij', a.astype(BF16), b.astype(BF16), preferred_element_type=F32)


def _softplus(y):
    return jnp.maximum(y, 0.0) + jnp.log(1.0 + jnp.exp(-jnp.abs(y)))


def _wkvprep_kernel(mix_ref, prev_ref, next_ref, mup_ref, mun_ref, w0_ref, wup_ref, a0_ref, aup_ref,
                    kk_ref, ka_ref, rk_ref, seg_ref,
                    bonus_ref, qeff_ref, o0_ref, g_ref, h_ref,
                    r_sc, v_sc, kn_sc, ld_sc, kd_sc, b_sc):
    tm = mix_ref.shape[1]
    i = pl.program_id(1)
    n = pl.num_programs(1)
    z = mix_ref[0]
    prev_row = jnp.where(i > 0, prev_ref[0, 7:8, :], 0.0)
    next_row = jnp.where(i < n - 1, next_ref[0, 0:1, :], 0.0)
    rows = lax.broadcasted_iota(jnp.int32, (tm, 1), 0)
    zp = jnp.where(rows == 0, prev_row, pltpu.roll(z, 1, 0))
    zn = jnp.where(rows == tm - 1, next_row, pltpu.roll(z, tm - 1, 0))
    zs = z + mup_ref[...] * (zp - z) + mun_ref[...] * (zn - z)

    w = RWKV_WIDTH
    r = zs[:, 0:w]
    k = zs[:, w:2 * w]
    v = zs[:, 2 * w:3 * w]
    seg = seg_ref[...]
    kkr = k * kk_ref[...]
    norm = jnp.sqrt(_dot_split_lhs(kkr * kkr, seg, 2))
    kn = kkr / jnp.maximum(norm, 1e-12)
    r_sc[...] = r
    v_sc[...] = v
    kn_sc[...] = kn
    coef = jnp.zeros((tm, w), F32)
    for d in range(2):
        wl = zs[:, 3 * w + d * W_LORA:3 * w + (d + 1) * W_LORA]
        al = zs[:, 3 * w + 2 * W_LORA + d * A_LORA:3 * w + 2 * W_LORA + (d + 1) * A_LORA]
        w_pre = w0_ref[d:d + 1, :] + jnp.dot(jnp.tanh(wl).astype(BF16), wup_ref[d], preferred_element_type=F32)
        w_log = -_softplus(-w_pre) - 0.5
        ld_sc[d] = -jnp.exp(w_log)
        a = jax.nn.sigmoid(a0_ref[d:d + 1, :] + jnp.dot(al.astype(BF16), aup_ref[d], preferred_element_type=F32))
        kd = k * (1.0 + (a - 1.0) * ka_ref[...])
        kd_sc[d] = kd
        b_sc[d] = kn * a
        coef = coef + _dot_split_lhs(r * kd * rk_ref[d:d + 1, :], seg, 2)
    bonus_ref[0] = coef * v

    ri = lax.broadcasted_iota(jnp.int32, (CHUNK, CHUNK), 0)
    ci = lax.broadcasted_iota(jnp.int32, (CHUNK, CHUNK), 1)
    eye_f = jnp.where(ri == ci, 1.0, 0.0)
    strict_f = [jnp.where(ci < ri, 1.0, 0.0), jnp.where(ci > ri, 1.0, 0.0)]
    incl_f = [s + eye_f for s in strict_f]
    npair = RWKV_HEADS // 2
    pw = 2 * RWKV_HEAD
    lane = lax.broadcasted_iota(jnp.int32, (1, pw), 1)
    even_lanes = lane < RWKV_HEAD
    m_even = jnp.where(even_lanes, 1.0, 0.0)
    m_odd = 1.0 - m_even
    even_lanes2 = jnp.concatenate([even_lanes, even_lanes], axis=1)
    r128 = lax.broadcasted_iota(jnp.int32, (pw, pw), 0)
    c128 = lax.broadcasted_iota(jnp.int32, (pw, pw), 1)
    eye_pw = jnp.where(r128 == c128, 1.0, 0.0)

    def _heads(x):
        return jnp.stack([x[:, p * pw:(p + 1) * pw] for p in range(npair)], axis=0)

    def group(gi, carry):
        combos = [(gi * WKV_GROUP + u, d) for u in range(WKV_GROUP) for d in range(2)]
        slices = []
        parts = {name: [] for name in ("rh", "kkh", "kt", "bt", "kv", "bv", "gam", "vh")}
        for c, d in combos:
            sl = pl.ds(pl.multiple_of(c * CHUNK, CHUNK), CHUNK)
            slices.append(sl)
            ld = ld_sc[d, sl, :]
            cum = _dot_split_rhs(incl_f[d].astype(BF16), ld, 3)
            tot = cum[CHUNK - 1:CHUNK, :] if d == 0 else cum[0:1, :]
            e_pos = jnp.exp(cum)
            e_prev = jnp.exp(cum - ld)
            e_neg = jnp.exp(-cum)
            e_rest = jnp.exp(tot - cum)
            kd_c = kd_sc[d, sl, :]
            b_c = b_sc[d, sl, :]
            parts["rh"].append(_heads(r_sc[sl, :] * e_pos))
            parts["kkh"].append(_heads(kn_sc[sl, :] * e_prev))
            parts["kt"].append(_heads(kd_c * e_neg))
            parts["bt"].append(_heads(b_c * e_neg))
            parts["kv"].append(_heads(kd_c * e_rest))
            parts["bv"].append(_heads(b_c * e_rest))
            parts["gam"].append(_heads(jnp.exp(tot)))
            parts["vh"].append(_heads(v_sc[sl, :]))
        cat = lambda xs: jnp.concatenate(xs, axis=0)
        rh, kkh, kt, bt, kv, bv, gam, vh = (cat(parts[n]) for n in ("rh", "kkh", "kt", "bt", "kv", "bv", "gam", "vh"))
        nbp = len(combos) * npair
        per = lambda ms: cat([jnp.broadcast_to(ms[d][None], (npair, CHUNK, CHUNK)) for _, d in combos])
        both = lambda x: jnp.concatenate([x, x], axis=0)
        masked = lambda x: jnp.concatenate([x * m_even, x * m_odd], axis=0)
        pick = lambda z, ev: jnp.where(ev, z[0:nbp], z[nbp:2 * nbp])
        strict = both(per(strict_f))
        incl = both(per(incl_f))

        kk_m = masked(kkh)
        r_m = masked(rh)
        kt2 = both(kt)
        bt2 = both(bt)
        v2 = both(vh)
        lk = _bmm_nt(kk_m, kt2) * strict
        lb = _bmm_nt(kk_m, bt2) * strict
        ark = _bmm_nt(r_m, kt2) * incl
        arb = _bmm_nt(r_m, bt2) * incl
        xp = -lb
        tinv = eye_f[None] + xp
        span = 2
        while span < CHUNK:
            xp = _bmm(xp, xp)
            tinv = tinv + _bmm(tinv, xp)
            span *= 2
        wu_h = _bmm(tinv, jnp.concatenate([both(kkh), _bmm(lk, v2)], axis=2))
        arb_wu = pick(_bmm(arb, wu_h), even_lanes2)
        wu = pick(wu_h, even_lanes2)
        qeff = rh - arb_wu[:, :, 0:pw]
        o0 = pick(_bmm(ark, v2), even_lanes) - arb_wu[:, :, pw:2 * pw]
        bvt_wu = _bmm(jnp.swapaxes(bv, 1, 2), wu)
        kvt_v = _bmm(jnp.swapaxes(kv, 1, 2), vh)
        gfull = eye_pw[None] * gam - bvt_wu[:, :, 0:pw]
        hfull = kvt_v - bvt_wu[:, :, pw:2 * pw]
        slab = lambda z: jnp.where(even_lanes, z[:, 0:RWKV_HEAD, :], z[:, RWKV_HEAD:pw, :])
        gmat = slab(gfull)
        hmat = slab(hfull)
        for idx, (c, d) in enumerate(combos):
            for p in range(npair):
                qeff_ref[d, 0, slices[idx], p * pw:(p + 1) * pw] = qeff[idx * npair + p].astype(BF16)
                o0_ref[d, 0, slices[idx], p * pw:(p + 1) * pw] = o0[idx * npair + p]
            g_ref[d, 0, c] = gmat[idx * npair:(idx + 1) * npair].astype(BF16)
            h_ref[d, 0, c] = hmat[idx * npair:(idx + 1) * npair]
        return carry

    lax.fori_loop(0, tm // (CHUNK * WKV_GROUP), group, 0)


def _wkvprep(mix, mu_prev, mu_next, w0, w_up, a0, a_up, k_k, k_a, r_k, seg):
    b, t, _ = mix.shape
    tm = min(TOKEN_TILE, t)
    assert t % tm == 0 and tm % CHUNK == 0
    nt = t // tm
    nc = tm // CHUNK
    hb = tm // 8
    last8 = t // 8 - 1
    const = lambda shape: pl.BlockSpec(shape, lambda bi, i: (0,) * len(shape))
    hd = (2, 1, tm, RWKV_WIDTH)
    st = (2, 1, nc) + PAIR_SLAB
    return pl.pallas_call(
        _wkvprep_kernel,
        grid=(b, nt),
        in_specs=[
            pl.BlockSpec((1, tm, SHIFT_WIDTH), lambda bi, i: (bi, i, 0)),
            pl.BlockSpec((1, 8, SHIFT_WIDTH), lambda bi, i: (bi, jnp.maximum(i * hb - 1, 0), 0)),
            pl.BlockSpec((1, 8, SHIFT_WIDTH), lambda bi, i: (bi, jnp.minimum((i + 1) * hb, last8), 0)),
            const((1, SHIFT_WIDTH)),
            const((1, SHIFT_WIDTH)),
            const((2, RWKV_WIDTH)),
            const((2, W_LORA, RWKV_WIDTH)),
            const((2, RWKV_WIDTH)),
            const((2, A_LORA, RWKV_WIDTH)),
            const((1, RWKV_WIDTH)),
            const((1, RWKV_WIDTH)),
            const((2, RWKV_WIDTH)),
            const((RWKV_WIDTH, RWKV_WIDTH)),
        ],
        out_specs=[
            pl.BlockSpec((1, tm, RWKV_WIDTH), lambda bi, i: (bi, i, 0)),
            pl.BlockSpec(hd, lambda bi, i: (0, bi, i, 0)),
            pl.BlockSpec(hd, lambda bi, i: (0, bi, i, 0)),
            pl.BlockSpec(st, lambda bi, i: (0, bi, i, 0, 0, 0)),
            pl.BlockSpec(st, lambda bi, i: (0, bi, i, 0, 0, 0)),
        ],
        out_shape=[
            jax.ShapeDtypeStruct((b, t, RWKV_WIDTH), F32),
            jax.ShapeDtypeStruct((2, b, t, RWKV_WIDTH), BF16),
            jax.ShapeDtypeStruct((2, b, t, RWKV_WIDTH), F32),
            jax.ShapeDtypeStruct((2, b, t // CHUNK) + PAIR_SLAB, BF16),
            jax.ShapeDtypeStruct((2, b, t // CHUNK) + PAIR_SLAB, F32),
        ],
        scratch_shapes=[
            pltpu.VMEM((tm, RWKV_WIDTH), F32),
            pltpu.VMEM((tm, RWKV_WIDTH), F32),
            pltpu.VMEM((tm, RWKV_WIDTH), F32),
            pltpu.VMEM((2, tm, RWKV_WIDTH), F32),
            pltpu.VMEM((2, tm, RWKV_WIDTH), F32),
            pltpu.VMEM((2, tm, RWKV_WIDTH), F32),
        ],
        compiler_params=_cparams(("parallel", "parallel")),
        name="wkvprep",
    )(mix, mix, mix, mu_prev, mu_next, w0, w_up, a0, a_up, k_k, k_a, r_k, seg)


def _wkvscan_kernel(qf_ref, qb_ref, of_ref, ob_ref, gf_ref, gb_ref, hf_ref, hb_ref, s0_ref,
                    outf_ref, outb_ref, sfin_ref, s_sc):
    c = pl.program_id(0)
    nb = qf_ref.shape[1]
    nc = gf_ref.shape[2]
    npair = HEAD_PAIRS
    pw = PAIR_WIDTH
    even_lanes = lax.broadcasted_iota(jnp.int32, (1, pw), 1) < RWKV_HEAD

    @pl.when(c == 0)
    def _():
        s_sc[...] = s0_ref[...].reshape(s_sc.shape)

    def bdot(a, s_parts):
        ab = a.astype(BF16)
        return sum(jnp.einsum('hij,hjk->hik', ab, p, preferred_element_type=F32) for p in s_parts)

    def block_diag(slab):
        zero = jnp.zeros_like(slab)
        return jnp.concatenate([jnp.where(even_lanes, slab, zero), jnp.where(even_lanes, zero, slab)], axis=1)

    def pairs(ref, rows):
        return [ref[0, bi, rows, p * pw:(p + 1) * pw] for bi in range(nb) for p in range(npair)]

    for j in range(nc):
        jb = nc - 1 - j
        rf = slice(j * CHUNK, (j + 1) * CHUNK)
        rb = slice(jb * CHUNK, (jb + 1) * CHUNK)
        s = _split_bf16(s_sc[...], 2)
        qe = jnp.stack(pairs(qf_ref, rf) + pairs(qb_ref, rb), axis=0)
        o0 = jnp.stack(pairs(of_ref, rf) + pairs(ob_ref, rb), axis=0)
        gm = jnp.concatenate([gf_ref[0, bi, j] for bi in range(nb)] + [gb_ref[0, bi, jb] for bi in range(nb)], axis=0)
        hm = jnp.concatenate([hf_ref[0, bi, j] for bi in range(nb)] + [hb_ref[0, bi, jb] for bi in range(nb)], axis=0)
        o = bdot(qe, s) + o0
        s_sc[...] = bdot(block_diag(gm), s) + block_diag(hm)
        for bi in range(nb):
            for p in range(npair):
                outf_ref[bi, rf, p * pw:(p + 1) * pw] = o[bi * npair + p]
                outb_ref[bi, rb, p * pw:(p + 1) * pw] = o[(nb + bi) * npair + p]

    @pl.when(c == pl.num_programs(0) - 1)
    def _():
        sfin_ref[...] = s_sc[...].reshape(sfin_ref.shape)


def _wkvscan(qeff, o0, g, h, s0):
    _, b, t, _ = qeff.shape
    tc = min(TOKEN_TILE, t)
    nc = tc // CHUNK
    nt = t // tc
    hd = (1, b, tc, RWKV_WIDTH)
    st = (1, b, nc) + PAIR_SLAB
    fwd_hd = pl.BlockSpec(hd, lambda c: (0, 0, c, 0))
    bwd_hd = pl.BlockSpec(hd, lambda c: (1, 0, nt - 1 - c, 0))
    fwd_st = pl.BlockSpec(st, lambda c: (0, 0, c, 0, 0, 0))
    bwd_st = pl.BlockSpec(st, lambda c: (1, 0, nt - 1 - c, 0, 0, 0))
    s_shape = (2, b, HEAD_PAIRS, PAIR_WIDTH, PAIR_WIDTH)
    s_spec = pl.BlockSpec(s_shape, lambda c: (0, 0, 0, 0, 0))
    return pl.pallas_call(
        _wkvscan_kernel,
        grid=(nt,),
        in_specs=[fwd_hd, bwd_hd, fwd_hd, bwd_hd, fwd_st, bwd_st, fwd_st, bwd_st, s_spec],
        out_specs=[
            pl.BlockSpec((b, tc, RWKV_WIDTH), lambda c: (0, c, 0)),
            pl.BlockSpec((b, tc, RWKV_WIDTH), lambda c: (0, nt - 1 - c, 0)),
            s_spec,
        ],
        out_shape=[
            jax.ShapeDtypeStruct((b, t, RWKV_WIDTH), F32),
            jax.ShapeDtypeStruct((b, t, RWKV_WIDTH), F32),
            jax.ShapeDtypeStruct(s_shape, F32),
        ],
        scratch_shapes=[pltpu.VMEM((2 * b * HEAD_PAIRS, PAIR_WIDTH, PAIR_WIDTH), F32)],
        compiler_params=_cparams(("arbitrary",)),
        name="wkvscan",
    )(qeff, qeff, o0, o0, g, g, h, h, s0)


def _merge_kernel(x_ref, mod_ref, ya_ref, ga_ref, of_ref, ob_ref, bonus_ref, gb_ref,
                  wpa_ref, wpb_ref, wmg_ref, bmg_ref, wo_ref, gnw_ref, gnb_ref, lng_ref, lnb_ref, seg_ref,
                  out_ref):
    x = x_ref[0]
    shift = mod_ref[0, 0:1, :]
    scale = mod_ref[0, 1:2, :]
    gate = mod_ref[0, 2:3, :]
    h = _layer_norm(x) * (1.0 + scale) + shift
    gates = jax.nn.sigmoid(jnp.dot(h.astype(BF16), wmg_ref[...], preferred_element_type=F32) + bmg_ref[...])
    ya = jnp.dot((ya_ref[0] * ga_ref[0].astype(F32)).astype(BF16), wpa_ref[...], preferred_element_type=F32)

    o = of_ref[0] + ob_ref[0]
    seg = seg_ref[...]
    mu = _dot_split_lhs(o, seg, 2) * (1.0 / RWKV_HEAD)
    oc = o - mu
    var = _dot_split_lhs(oc * oc, seg, 2) * (1.0 / RWKV_HEAD)
    y = oc * lax.rsqrt(var + GN_EPS) * gnw_ref[...] + gnb_ref[...] + bonus_ref[0]
    yb = jnp.dot((y * gb_ref[0].astype(F32)).astype(BF16), wpb_ref[...], preferred_element_type=F32)

    mixed = gates[:, 0:D_MODEL] * ya + gates[:, D_MODEL:2 * D_MODEL] * yb
    out = jnp.dot(mixed.astype(BF16), wo_ref[...], preferred_element_type=F32)
    out_ref[0] = _layer_norm(ALPHA * x + gate * out) * lng_ref[...] + lnb_ref[...]


def _merge(x, mod, y_att, ga, o_f, o_b, bonus, gb, w_pa, w_pb, w_mg, b_mg, w_o, gn_w, gn_b, ln_g, ln_b, seg):
    b, t, _ = x.shape
    tm = min(TOKEN_TILE, t)
    const = lambda shape: pl.BlockSpec(shape, lambda bi, i: (0,) * len(shape))
    tok = lambda width: pl.BlockSpec((1, tm, width), lambda bi, i: (bi, i, 0))
    return pl.pallas_call(
        _merge_kernel,
        grid=(b, t // tm),
        in_specs=[
            tok(D_MODEL),
            pl.BlockSpec((1, 3, D_MODEL), lambda bi, i: (bi, 0, 0)),
            tok(ATT_WIDTH),
            tok(ATT_WIDTH),
            tok(RWKV_WIDTH),
            tok(RWKV_WIDTH),
            tok(RWKV_WIDTH),
            tok(RWKV_WIDTH),
            const((ATT_WIDTH, D_MODEL)),
            const((RWKV_WIDTH, D_MODEL)),
            const((D_MODEL, 2 * D_MODEL)),
            const((1, 2 * D_MODEL)),
            const((D_MODEL, D_MODEL)),
            const((1, RWKV_WIDTH)),
            const((1, RWKV_WIDTH)),
            const((1, D_MODEL)),
            const((1, D_MODEL)),
            const((RWKV_WIDTH, RWKV_WIDTH)),
        ],
        out_specs=tok(D_MODEL),
        out_shape=jax.ShapeDtypeStruct((b, t, D_MODEL), F32),
        compiler_params=_cparams(("parallel", "parallel")),
        name="merge",
    )(x, mod, y_att, ga, o_f, o_b, bonus, gb, w_pa, w_pb, w_mg, b_mg, w_o, gn_w, gn_b, ln_g, ln_b, seg)


def _rope_tables_t(t):
    pos = jnp.arange(t, dtype=jnp.int32)
    row = (pos // GRID_W).astype(F32)
    col = (pos % GRID_W).astype(F32)
    inv = ROPE_BASE ** (-jnp.arange(ROPE_FREQS, dtype=F32) / ROPE_FREQS)
    ang_r = inv[:, None] * row[None, :]
    ang_c = inv[:, None] * col[None, :]
    cos_t = jnp.concatenate([jnp.cos(ang_r), jnp.cos(ang_r), jnp.cos(ang_c), jnp.cos(ang_c)], axis=0)
    sin_t = jnp.concatenate([-jnp.sin(ang_r), jnp.sin(ang_r), -jnp.sin(ang_c), jnp.sin(ang_c)], axis=0)
    return cos_t, sin_t


def kernel(x, c, ctx, c_ctx, w_ada, b_ada, w_in, q_norm, k_norm, mu_prev, mu_next, w0, w_up, a0, a_up,
           k_k, k_a, r_k, gn_w, gn_b, w_pa, w_pb, w_mg, b_mg, w_o, ln_g, ln_b):
    b, t, _ = x.shape
    tc = ctx.shape[1]
    l = 0
    seg = jnp.kron(jnp.eye(RWKV_HEADS, dtype=F32), jnp.ones((RWKV_HEAD, RWKV_HEAD), F32)).astype(BF16)

    rows = 8 * ((b + 1 + 7) // 8)
    cc = jnp.zeros((rows, D_MODEL), F32).at[0:b].set(c).at[b].set(c_ctx)
    ada = _ada(cc, w_ada[l], b_ada[l][None, :])
    mod_x = ada[0:b].reshape(b, 3, D_MODEL)
    mod_c = jnp.broadcast_to(ada[b].reshape(1, 3, D_MODEL), (b, 3, D_MODEL))

    w_t = w_in[l][:, 0:QKV_WIDTH].T.astype(BF16)
    w_n = w_in[l][:, QKV_WIDTH:].astype(BF16)
    q_gain = q_norm[l][:, None]
    k_gain = k_norm[l][:, None]
    cos_x, sin_x = _rope_tables_t(t)
    cos_c = jnp.ones((HEAD_DIM, tc), F32)
    sin_c = jnp.zeros((HEAD_DIM, tc), F32)

    q_x, k_x, q8_x, k8_x, v_x, ga_x, mix_x, gb_x = _inproj(x, mod_x, cos_x, sin_x, w_t, w_n, q_gain, k_gain)
    _, k_c, _, k8_c, v_c, _, mix_c, _ = _inproj(ctx, mod_c, cos_c, sin_c, w_t, w_n, q_gain, k_gain)

    v_all = jnp.concatenate([v_c, v_x], axis=2)
    fp8_ok = jnp.logical_and(
        jnp.max(jnp.abs(q_norm[l])) * (2.0 * HEAD_DIM ** 0.5 * HEAD_DIM ** -0.5 * math.log2(math.e) * FP8_Q_SCALE) < FP8_MAX,
        jnp.max(jnp.abs(k_norm[l])) * (2.0 * HEAD_DIM ** 0.5 / FP8_Q_SCALE) < FP8_MAX)
    y_att = lax.cond(
        fp8_ok,
        lambda: _attention(q8_x, jnp.concatenate([k8_c, k8_x], axis=2), v_all),
        lambda: _attention(q_x, jnp.concatenate([k_c, k_x], axis=2), v_all))

    prep_args = (mu_prev[l][None, :], mu_next[l][None, :], w0[l], w_up[l].astype(BF16), a0[l], a_up[l].astype(BF16),
                 k_k[l][None, :], k_a[l][None, :], r_k[l].reshape(2, RWKV_WIDTH), seg)
    _, qe_c, o0_c, g_c, h_c = _wkvprep(mix_c, *prep_args)
    zero_state = jnp.zeros((2, b, HEAD_PAIRS, PAIR_WIDTH, PAIR_WIDTH), F32)
    _, _, s_ctx = _wkvscan(qe_c, o0_c, g_c, h_c, zero_state)
    bonus, qe_x, o0_x, g_x, h_x = _wkvprep(mix_x, *prep_args)
    of_x, ob_x, _ = _wkvscan(qe_x, o0_x, g_x, h_x, s_ctx)

    return _merge(x, mod_x, y_att, ga_x, of_x, ob_x, bonus, gb_x,
                  w_pa[l].astype(BF16), w_pb[l].astype(BF16), w_mg[l].astype(BF16), b_mg[l][None, :],
                  w_o[l].astype(BF16), gn_w[l][None, :], gn_b[l][None, :], ln_g[l][None, :], ln_b[l][None, :], seg)
```

```python
import functools
import math

import jax
import jax.numpy as jnp
from jax import lax
from jax.experimental import pallas as pl
from jax.experimental.pallas import tpu as pltpu

F32 = jnp.float32
BF16 = jnp.bfloat16
FP8 = jnp.float8_e4m3fn
HIGHEST = lax.Precision.HIGHEST

D_MODEL = 1024
GRID_W = 64
HEAD_DIM = 64
N_Q_HEADS = 8
N_KV_HEADS = 2
GQA_GROUP = N_Q_HEADS // N_KV_HEADS
ATT_WIDTH = N_Q_HEADS * HEAD_DIM
KV_WIDTH = N_KV_HEADS * HEAD_DIM
QKV_WIDTH = ATT_WIDTH + 2 * KV_WIDTH
ROPE_FREQS = HEAD_DIM // 4
ROPE_BASE = 10000.0
RWKV_HEAD = 64
RWKV_HEADS = 8
RWKV_WIDTH = RWKV_HEADS * RWKV_HEAD
W_LORA = 64
A_LORA = 64
SHIFT_WIDTH = 3 * RWKV_WIDTH + 2 * W_LORA + 2 * A_LORA
IN_WIDTH = 2 * ATT_WIDTH + 2 * KV_WIDTH + SHIFT_WIDTH + RWKV_WIDTH
NAT_WIDTH = IN_WIDTH - QKV_WIDTH
LN_EPS = 1e-5
QK_EPS = 1e-6
GN_EPS = 64e-5
DEPTH = 1
ALPHA = (2.0 * DEPTH) ** 0.25

CHUNK = 64
KEY_BLOCK = 256
V_ROWS = HEAD_DIM + 8
FP8_DEPTH = 3 * HEAD_DIM
FP8_LO_SCALE = 16.0
FP8_Q_SCALE = 2.0
FP8_MAX = 448.0
TOKEN_TILE = 256
MERGE_TILE = 512
Q_TILE = 512
ATTN_COLS = 256
ATTN_UNIT_BLOCKS = 4
ATTN_UNROLL = 2
WKV_GROUP = 2
SEG_SUM_PARTS = 1
HEAD_PAIRS = RWKV_HEADS // 2
PAIR_WIDTH = 2 * RWKV_HEAD
PAIR_SLAB = (HEAD_PAIRS, RWKV_HEAD, PAIR_WIDTH)
VMEM_LIMIT = 48 * 1024 * 1024
NEG_BIG = -1e30


def _cparams(sem):
    return pltpu.CompilerParams(dimension_semantics=sem, vmem_limit_bytes=VMEM_LIMIT)


def _silu(x):
    return x * jax.nn.sigmoid(x)


def _layer_norm(x):
    mu = jnp.mean(x, axis=-1, keepdims=True)
    xc = x - mu
    var = jnp.mean(xc * xc, axis=-1, keepdims=True)
    return xc * lax.rsqrt(var + LN_EPS)


def _split_bf16(x, parts):
    out = []
    rem = x
    for _ in range(parts):
        p = rem.astype(BF16)
        out.append(p)
        rem = rem - p.astype(F32)
    return out


def _dot_split_lhs(x, m_bf16, parts):
    acc = None
    for p in _split_bf16(x, parts):
        t = jnp.dot(p, m_bf16, preferred_element_type=F32)
        acc = t if acc is None else acc + t
    return acc


def _dot_split_rhs(m_bf16, x, parts):
    acc = None
    for p in _split_bf16(x, parts):
        t = jnp.dot(m_bf16, p, preferred_element_type=F32)
        acc = t if acc is None else acc + t
    return acc


def _ada_kernel(c_ref, w_ref, b_ref, o_ref):
    s = _silu(c_ref[...])
    o_ref[...] = jnp.dot(s, w_ref[...], precision=HIGHEST, preferred_element_type=F32) + b_ref[...]


def _ada(cc, w_ada, b_ada):
    rows = cc.shape[0]
    nblk = (3 * D_MODEL) // D_MODEL
    return pl.pallas_call(
        _ada_kernel,
        grid=(nblk,),
        in_specs=[
            pl.BlockSpec((rows, D_MODEL), lambda j: (0, 0)),
            pl.BlockSpec((D_MODEL, D_MODEL), lambda j: (0, j)),
            pl.BlockSpec((1, D_MODEL), lambda j: (0, j)),
        ],
        out_specs=pl.BlockSpec((rows, D_MODEL), lambda j: (0, j)),
        out_shape=jax.ShapeDtypeStruct((rows, 3 * D_MODEL), F32),
        compiler_params=_cparams(("arbitrary",)),
        name="ada",
    )(cc, w_ada, b_ada)


def _rope_t(x, cos, sin):
    xs = jnp.concatenate([x[:, 16:32], x[:, 0:16], x[:, 48:64], x[:, 32:48]], axis=1)
    return x * cos + xs * sin


def _fp8_pieces(x):
    hi = x.astype(FP8).astype(F32)
    lo16 = ((x - hi) * FP8_LO_SCALE).astype(FP8).astype(F32)
    hi16 = (hi * (1.0 / FP8_LO_SCALE)).astype(FP8).astype(F32)
    return hi, hi16, lo16


def _inproj_kernel(x_ref, mod_ref, cos_ref, sin_ref, wt_ref, wn_ref, qg_ref, kg_ref,
                   q_ref, k_ref, q8_ref, k8_ref, v_ref, ga_ref, mix_ref, gb_ref):
    tm = x_ref.shape[1]
    x = x_ref[0]
    shift = mod_ref[0, 0:1, :]
    scale = mod_ref[0, 1:2, :]
    h = _layer_norm(x) * (1.0 + scale) + shift
    hb = h.astype(BF16)
    zt = lax.dot_general(wt_ref[...], hb, (((1,), (1,)), ((), ())), preferred_element_type=F32)
    zn = jnp.dot(hb, wn_ref[...], preferred_element_type=F32)
    cos = cos_ref[...][None]
    sin = sin_ref[...][None]

    qz = zt[0:ATT_WIDTH].reshape(N_Q_HEADS, HEAD_DIM, tm)
    qn = qz * lax.rsqrt(jnp.mean(qz * qz, axis=1, keepdims=True) + QK_EPS) * qg_ref[...][None]
    qr = _rope_t(qn, cos, sin) * (HEAD_DIM ** -0.5 * math.log2(math.e))
    q_ref[0] = qr.astype(BF16)
    q_hi, q_hi16, q_lo16 = _fp8_pieces(qr * FP8_Q_SCALE)
    q8_ref[0] = jnp.concatenate([q_hi, q_hi16, q_lo16], axis=1).astype(FP8)

    kz = zt[ATT_WIDTH:ATT_WIDTH + KV_WIDTH].reshape(N_KV_HEADS, HEAD_DIM, tm)
    kn = kz * lax.rsqrt(jnp.mean(kz * kz, axis=1, keepdims=True) + QK_EPS) * kg_ref[...][None]
    kr = _rope_t(kn, cos, sin)
    k_hi, k_hi16, k_lo16 = _fp8_pieces(kr * (1.0 / FP8_Q_SCALE))
    k8 = jnp.concatenate([k_hi, k_lo16, k_hi16], axis=1)
    vz = zt[ATT_WIDTH + KV_WIDTH:QKV_WIDTH].reshape(N_KV_HEADS, HEAD_DIM, tm)
    pad_row = lax.broadcasted_iota(jnp.int32, (V_ROWS - HEAD_DIM, tm), 0)
    ones_pad = jnp.where(pad_row == 0, 1.0, 0.0).astype(BF16)
    for j in range(N_KV_HEADS):
        k_ref[0, j, 0] = kr[j].T.astype(BF16)
        k8_ref[0, j, 0] = k8[j].T.astype(FP8)
        v_ref[0, j, 0, 0:HEAD_DIM, :] = vz[j].astype(BF16)
        v_ref[0, j, 0, HEAD_DIM:V_ROWS, :] = ones_pad

    ga_ref[0] = _silu(zn[:, 0:ATT_WIDTH]).astype(BF16)
    mix_ref[0] = zn[:, ATT_WIDTH:ATT_WIDTH + SHIFT_WIDTH]
    gb_ref[0] = _silu(zn[:, ATT_WIDTH + SHIFT_WIDTH:NAT_WIDTH]).astype(BF16)


def _inproj(x, mod, cos_t, sin_t, w_t, w_n, q_gain, k_gain):
    b, t, _ = x.shape
    tm = TOKEN_TILE
    assert tm == KEY_BLOCK and t % tm == 0
    nt = t // tm
    const = lambda shape: pl.BlockSpec(shape, lambda bi, i: (0,) * len(shape))
    return pl.pallas_call(
        _inproj_kernel,
        grid=(b, nt),
        in_specs=[
            pl.BlockSpec((1, tm, D_MODEL), lambda bi, i: (bi, i, 0)),
            pl.BlockSpec((1, 3, D_MODEL), lambda bi, i: (bi, 0, 0)),
            pl.BlockSpec((HEAD_DIM, tm), lambda bi, i: (0, i)),
            pl.BlockSpec((HEAD_DIM, tm), lambda bi, i: (0, i)),
            const((QKV_WIDTH, D_MODEL)),
            const((D_MODEL, NAT_WIDTH)),
            const((HEAD_DIM, 1)),
            const((HEAD_DIM, 1)),
        ],
        out_specs=[
            pl.BlockSpec((1, N_Q_HEADS, HEAD_DIM, tm), lambda bi, i: (bi, 0, 0, i)),
            pl.BlockSpec((1, N_KV_HEADS, 1, tm, HEAD_DIM), lambda bi, i: (bi, 0, i, 0, 0)),
            pl.BlockSpec((1, N_Q_HEADS, FP8_DEPTH, tm), lambda bi, i: (bi, 0, 0, i)),
            pl.BlockSpec((1, N_KV_HEADS, 1, tm, FP8_DEPTH), lambda bi, i: (bi, 0, i, 0, 0)),
            pl.BlockSpec((1, N_KV_HEADS, 1, V_ROWS, tm), lambda bi, i: (bi, 0, i, 0, 0)),
            pl.BlockSpec((1, tm, ATT_WIDTH), lambda bi, i: (bi, i, 0)),
            pl.BlockSpec((1, tm, SHIFT_WIDTH), lambda bi, i: (bi, i, 0)),
            pl.BlockSpec((1, tm, RWKV_WIDTH), lambda bi, i: (bi, i, 0)),
        ],
        out_shape=[
            jax.ShapeDtypeStruct((b, N_Q_HEADS, HEAD_DIM, t), BF16),
            jax.ShapeDtypeStruct((b, N_KV_HEADS, nt, tm, HEAD_DIM), BF16),
            jax.ShapeDtypeStruct((b, N_Q_HEADS, FP8_DEPTH, t), FP8),
            jax.ShapeDtypeStruct((b, N_KV_HEADS, nt, tm, FP8_DEPTH), FP8),
            jax.ShapeDtypeStruct((b, N_KV_HEADS, nt, V_ROWS, tm), BF16),
            jax.ShapeDtypeStruct((b, t, ATT_WIDTH), BF16),
            jax.ShapeDtypeStruct((b, t, SHIFT_WIDTH), F32),
            jax.ShapeDtypeStruct((b, t, RWKV_WIDTH), BF16),
        ],
        compiler_params=_cparams(("parallel", "parallel")),
        name="inproj",
    )(x, mod, cos_t, sin_t, w_t, w_n, q_gain, k_gain)


def _attn_kernel(q_ref, k_ref, v_ref, o_ref, s_sc, cmax_sc, m_sc, acc_sc):
    nkb = k_ref.shape[2]
    tq = q_ref.shape[3]
    width = GQA_GROUP * tq
    nchunk = width // ATTN_COLS
    per_head = tq // ATTN_COLS
    kb = KEY_BLOCK
    kbt = ATTN_UNIT_BLOCKS
    m_sc[...] = jnp.full(m_sc.shape, NEG_BIG, F32)
    acc_sc[...] = jnp.zeros(acc_sc.shape, F32)
    first = nkb % kbt if nkb % kbt else kbt
    nunits = 1 + (nkb - first) // kbt

    def unit_start(u):
        return first + (u - 1) * kbt

    def tick(u, par, nb_acc, nb_score):
        if nb_acc:
            acc0 = 0 if (isinstance(u, int) and u == 1) else unit_start(u - 1)
            m_old = m_sc[...]
            m_new = jnp.maximum(m_old, cmax_sc[1 - par])
            alpha = jnp.exp2(m_old - m_new)
            m_sc[...] = m_new
        if nb_score:
            sc0 = 0 if (isinstance(u, int) and u == 0) else unit_start(u)
        for ch in range(nchunk):
            cols = slice(ch * ATTN_COLS, (ch + 1) * ATTN_COLS)
            if nb_acc:
                pv = None
                for i in range(nb_acc):
                    p = jnp.exp2(s_sc[1 - par, i * kb:(i + 1) * kb, cols] - m_new[:, cols]).astype(BF16)
                    t = jnp.dot(v_ref[0, 0, acc0 + i], p, preferred_element_type=F32)
                    pv = t if pv is None else pv + t
                acc_sc[:, cols] = acc_sc[:, cols] * alpha[:, cols] + pv
            if nb_score:
                g, h = divmod(ch, per_head)
                qc = q_ref[0, g, :, h * ATTN_COLS:(h + 1) * ATTN_COLS]
                cm = None
                for i in range(nb_score):
                    s = jnp.dot(k_ref[0, 0, sc0 + i], qc, preferred_element_type=F32)
                    s_sc[par, i * kb:(i + 1) * kb, cols] = s
                    c1 = jnp.max(s, axis=0, keepdims=True)
                    cm = c1 if cm is None else jnp.maximum(cm, c1)
                cmax_sc[par, :, cols] = cm

    tick(0, 0, 0, first)
    if nunits > 1:
        tick(1, 1, first, kbt)
    nsteady = max(nunits - 2, 0)
    npeel = nsteady % ATTN_UNROLL
    for u in range(2, 2 + npeel):
        tick(u, u % 2, kbt, kbt)
    base = 2 + npeel

    def group(j, carry):
        for i in range(ATTN_UNROLL):
            tick(base + ATTN_UNROLL * j + i, (base + i) % 2, kbt, kbt)
        return carry

    lax.fori_loop(0, (nsteady - npeel) // ATTN_UNROLL, group, 0)
    tick(nunits, nunits % 2, kbt if nunits > 1 else first, 0)
    for g in range(GQA_GROUP):
        acc = acc_sc[:, g * tq:(g + 1) * tq]
        o = acc[0:HEAD_DIM] / acc[HEAD_DIM:HEAD_DIM + 1]
        o_ref[0, :, g * HEAD_DIM:(g + 1) * HEAD_DIM] = o.T


def _attention(q_t, k_blk, v_blk):
    b, _, depth, t = q_t.shape
    nkb = k_blk.shape[2]
    tq = min(Q_TILE, t)
    assert t % tq == 0 and k_blk.shape[4] == depth
    width = GQA_GROUP * tq
    return pl.pallas_call(
        _attn_kernel,
        grid=(b, N_KV_HEADS, t // tq),
        in_specs=[
            pl.BlockSpec((1, GQA_GROUP, depth, tq), lambda bi, j, i: (bi, j, 0, i)),
            pl.BlockSpec((1, 1, nkb, KEY_BLOCK, depth), lambda bi, j, i: (bi, j, 0, 0, 0)),
            pl.BlockSpec((1, 1, nkb, V_ROWS, KEY_BLOCK), lambda bi, j, i: (bi, j, 0, 0, 0)),
        ],
        out_specs=pl.BlockSpec((1, tq, GQA_GROUP * HEAD_DIM), lambda bi, j, i: (bi, i, j)),
        out_shape=jax.ShapeDtypeStruct((b, t, ATT_WIDTH), F32),
        scratch_shapes=[
            pltpu.VMEM((2, ATTN_UNIT_BLOCKS * KEY_BLOCK, width), F32),
            pltpu.VMEM((2, 1, width), F32),
            pltpu.VMEM((1, width), F32),
            pltpu.VMEM((V_ROWS, width), F32),
        ],
        compiler_params=_cparams(("parallel", "parallel", "parallel")),
        name="attn",
    )(q_t, k_blk, v_blk)


def _bmm(a, b):
    return jnp.einsum('hij,hjk->hik', a.astype(BF16), b.astype(BF16), preferred_element_type=F32)


def _bmm_nt(a, b):
    return jnp.einsum('hik,hjk->hij', a.astype(BF16), b.astype(BF16), preferred_element_type=F32)


def _wkvprep_kernel(mix_ref, prev_ref, next_ref, mup_ref, mun_ref, w0_ref, wup_ref, a0_ref, aup_ref,
                    kk_ref, ka_ref, rk_ref, seg_ref,
                    bonus_ref, qeff_ref, o0_ref, g_ref, h_ref,
                    r_sc, v_sc, kn_sc, ld_sc, kd_sc, b_sc):
    tm = mix_ref.shape[1]
    i = pl.program_id(1)
    n = pl.num_programs(1)
    z = mix_ref[0]
    prev_row = jnp.where(i > 0, prev_ref[0, 7:8, :], 0.0)
    next_row = jnp.where(i < n - 1, next_ref[0, 0:1, :], 0.0)
    rows = lax.broadcasted_iota(jnp.int32, (tm, 1), 0)
    zp = jnp.where(rows == 0, prev_row, pltpu.roll(z, 1, 0))
    zn = jnp.where(rows == tm - 1, next_row, pltpu.roll(z, tm - 1, 0))
    zs = z + mup_ref[...] * (zp - z) + mun_ref[...] * (zn - z)

    w = RWKV_WIDTH
    r = zs[:, 0:w]
    k = zs[:, w:2 * w]
    v = zs[:, 2 * w:3 * w]
    seg = seg_ref[...]
    kkr = k * kk_ref[...]
    norm = jnp.sqrt(_dot_split_lhs(kkr * kkr, seg, SEG_SUM_PARTS))
    kn = kkr / jnp.maximum(norm, 1e-12)
    r_sc[...] = r
    v_sc[...] = v
    kn_sc[...] = kn
    coef = jnp.zeros((tm, w), F32)
    for d in range(2):
        wl = zs[:, 3 * w + d * W_LORA:3 * w + (d + 1) * W_LORA]
        al = zs[:, 3 * w + 2 * W_LORA + d * A_LORA:3 * w + 2 * W_LORA + (d + 1) * A_LORA]
        w_pre = w0_ref[d:d + 1, :] + jnp.dot(jnp.tanh(wl).astype(BF16), wup_ref[d], preferred_element_type=F32)
        ld_sc[d] = -math.exp(-0.5) * jax.nn.sigmoid(w_pre)
        a = jax.nn.sigmoid(a0_ref[d:d + 1, :] + jnp.dot(al.astype(BF16), aup_ref[d], preferred_element_type=F32))
        kd = k * (1.0 + (a - 1.0) * ka_ref[...])
        kd_sc[d] = kd
        b_sc[d] = kn * a
        coef = coef + _dot_split_lhs(r * kd * rk_ref[d:d + 1, :], seg, SEG_SUM_PARTS)
    bonus_ref[0] = coef * v

    ri = lax.broadcasted_iota(jnp.int32, (CHUNK, CHUNK), 0)
    ci = lax.broadcasted_iota(jnp.int32, (CHUNK, CHUNK), 1)
    eye_f = jnp.where(ri == ci, 1.0, 0.0)
    strict_f = [jnp.where(ci < ri, 1.0, 0.0), jnp.where(ci > ri, 1.0, 0.0)]
    incl_f = [s + eye_f for s in strict_f]
    npair = RWKV_HEADS // 2
    pw = 2 * RWKV_HEAD
    lane = lax.broadcasted_iota(jnp.int32, (1, pw), 1)
    even_lanes = lane < RWKV_HEAD
    m_even = jnp.where(even_lanes, 1.0, 0.0)
    m_odd = 1.0 - m_even
    even_lanes2 = jnp.concatenate([even_lanes, even_lanes], axis=1)
    r128 = lax.broadcasted_iota(jnp.int32, (pw, pw), 0)
    c128 = lax.broadcasted_iota(jnp.int32, (pw, pw), 1)
    eye_pw = jnp.where(r128 == c128, 1.0, 0.0)

    def _heads(x):
        return jnp.stack([x[:, p * pw:(p + 1) * pw] for p in range(npair)], axis=0)

    def group(gi, carry):
        combos = [(gi * WKV_GROUP + u, d) for u in range(WKV_GROUP) for d in range(2)]
        slices = []
        parts = {name: [] for name in ("rh", "kkh", "kt", "bt", "kv", "bv", "gam", "vh")}
        for c, d in combos:
            sl = pl.ds(pl.multiple_of(c * CHUNK, CHUNK), CHUNK)
            slices.append(sl)
            ld = ld_sc[d, sl, :]
            cum = _dot_split_rhs(incl_f[d].astype(BF16), ld, 3)
            tot = cum[CHUNK - 1:CHUNK, :] if d == 0 else cum[0:1, :]
            e_pos = jnp.exp(cum)
            e_prev = jnp.exp(cum - ld)
            e_neg = jnp.exp(-cum)
            e_rest = jnp.exp(tot - cum)
            kd_c = kd_sc[d, sl, :]
            b_c = b_sc[d, sl, :]
            parts["rh"].append(_heads(r_sc[sl, :] * e_pos))
            parts["kkh"].append(_heads(kn_sc[sl, :] * e_prev))
            parts["kt"].append(_heads(kd_c * e_neg))
            parts["bt"].append(_heads(b_c * e_neg))
            parts["kv"].append(_heads(kd_c * e_rest))
            parts["bv"].append(_heads(b_c * e_rest))
            parts["gam"].append(_heads(jnp.exp(tot)))
            parts["vh"].append(_heads(v_sc[sl, :]))
        cat = lambda xs: jnp.concatenate(xs, axis=0)
        rh, kkh, kt, bt, kv, bv, gam, vh = (cat(parts[n]) for n in ("rh", "kkh", "kt", "bt", "kv", "bv", "gam", "vh"))
        nbp = len(combos) * npair
        per = lambda ms: cat([jnp.broadcast_to(ms[d][None], (npair, CHUNK, CHUNK)) for _, d in combos])
        both = lambda x: jnp.concatenate([x, x], axis=0)
        masked = lambda x: jnp.concatenate([x * m_even, x * m_odd], axis=0)
        pick = lambda z, ev: jnp.where(ev, z[0:nbp], z[nbp:2 * nbp])
        strict = both(per(strict_f))
        incl = both(per(incl_f))

        kk_m = masked(kkh)
        r_m = masked(rh)
        kt2 = both(kt)
        bt2 = both(bt)
        v2 = both(vh)
        lk = _bmm_nt(kk_m, kt2) * strict
        lb = _bmm_nt(kk_m, bt2) * strict
        ark = _bmm_nt(r_m, kt2) * incl
        arb = _bmm_nt(r_m, bt2) * incl
        xp = -lb
        tinv = eye_f[None] + xp
        span = 2
        while span < CHUNK:
            xp = _bmm(xp, xp)
            tinv = tinv + _bmm(tinv, xp)
            span *= 2
        wu_h = _bmm(tinv, jnp.concatenate([both(kkh), _bmm(lk, v2)], axis=2))
        arb_wu = pick(_bmm(arb, wu_h), even_lanes2)
        wu = pick(wu_h, even_lanes2)
        qeff = rh - arb_wu[:, :, 0:pw]
        o0 = pick(_bmm(ark, v2), even_lanes) - arb_wu[:, :, pw:2 * pw]
        bvt_wu = _bmm(jnp.swapaxes(bv, 1, 2), wu)
        kvt_v = _bmm(jnp.swapaxes(kv, 1, 2), vh)
        gfull = eye_pw[None] * gam - bvt_wu[:, :, 0:pw]
        hfull = kvt_v - bvt_wu[:, :, pw:2 * pw]
        slab = lambda z: jnp.where(even_lanes, z[:, 0:RWKV_HEAD, :], z[:, RWKV_HEAD:pw, :])
        gmat = slab(gfull)
        hmat = slab(hfull)
        for idx, (c, d) in enumerate(combos):
            for p in range(npair):
                qeff_ref[d, 0, slices[idx], p * pw:(p + 1) * pw] = qeff[idx * npair + p].astype(BF16)
                o0_ref[d, 0, slices[idx], p * pw:(p + 1) * pw] = o0[idx * npair + p]
            g_ref[d, 0, c] = gmat[idx * npair:(idx + 1) * npair].astype(BF16)
            h_ref[d, 0, c] = hmat[idx * npair:(idx + 1) * npair]
        return carry

    lax.fori_loop(0, tm // (CHUNK * WKV_GROUP), group, 0)


def _wkvprep(mix, mu_prev, mu_next, w0, w_up, a0, a_up, k_k, k_a, r_k, seg):
    b, t, _ = mix.shape
    tm = min(TOKEN_TILE, t)
    assert t % tm == 0 and tm % CHUNK == 0
    nt = t // tm
    nc = tm // CHUNK
    hb = tm // 8
    last8 = t // 8 - 1
    const = lambda shape: pl.BlockSpec(shape, lambda bi, i: (0,) * len(shape))
    hd = (2, 1, tm, RWKV_WIDTH)
    st = (2, 1, nc) + PAIR_SLAB
    return pl.pallas_call(
        _wkvprep_kernel,
        grid=(b, nt),
        in_specs=[
            pl.BlockSpec((1, tm, SHIFT_WIDTH), lambda bi, i: (bi, i, 0)),
            pl.BlockSpec((1, 8, SHIFT_WIDTH), lambda bi, i: (bi, jnp.maximum(i * hb - 1, 0), 0)),
            pl.BlockSpec((1, 8, SHIFT_WIDTH), lambda bi, i: (bi, jnp.minimum((i + 1) * hb, last8), 0)),
            const((1, SHIFT_WIDTH)),
            const((1, SHIFT_WIDTH)),
            const((2, RWKV_WIDTH)),
            const((2, W_LORA, RWKV_WIDTH)),
            const((2, RWKV_WIDTH)),
            const((2, A_LORA, RWKV_WIDTH)),
            const((1, RWKV_WIDTH)),
            const((1, RWKV_WIDTH)),
            const((2, RWKV_WIDTH)),
            const((RWKV_WIDTH, RWKV_WIDTH)),
        ],
        out_specs=[
            pl.BlockSpec((1, tm, RWKV_WIDTH), lambda bi, i: (bi, i, 0)),
            pl.BlockSpec(hd, lambda bi, i: (0, bi, i, 0)),
            pl.BlockSpec(hd, lambda bi, i: (0, bi, i, 0)),
            pl.BlockSpec(st, lambda bi, i: (0, bi, i, 0, 0, 0)),
            pl.BlockSpec(st, lambda bi, i: (0, bi, i, 0, 0, 0)),
        ],
        out_shape=[
            jax.ShapeDtypeStruct((b, t, RWKV_WIDTH), F32),
            jax.ShapeDtypeStruct((2, b, t, RWKV_WIDTH), BF16),
            jax.ShapeDtypeStruct((2, b, t, RWKV_WIDTH), F32),
            jax.ShapeDtypeStruct((2, b, t // CHUNK) + PAIR_SLAB, BF16),
            jax.ShapeDtypeStruct((2, b, t // CHUNK) + PAIR_SLAB, F32),
        ],
        scratch_shapes=[
            pltpu.VMEM((tm, RWKV_WIDTH), F32),
            pltpu.VMEM((tm, RWKV_WIDTH), F32),
            pltpu.VMEM((tm, RWKV_WIDTH), F32),
            pltpu.VMEM((2, tm, RWKV_WIDTH), F32),
            pltpu.VMEM((2, tm, RWKV_WIDTH), F32),
            pltpu.VMEM((2, tm, RWKV_WIDTH), F32),
        ],
        compiler_params=_cparams(("parallel", "parallel")),
        name="wkvprep",
    )(mix, mix, mix, mu_prev, mu_next, w0, w_up, a0, a_up, k_k, k_a, r_k, seg)


def _wkvscan_kernel(qf_ref, qb_ref, of_ref, ob_ref, gf_ref, gb_ref, hf_ref, hb_ref, s0_ref,
                    outf_ref, outb_ref, sfin_ref, s_sc):
    c = pl.program_id(0)
    nb = qf_ref.shape[1]
    nc = gf_ref.shape[2]
    npair = HEAD_PAIRS
    pw = PAIR_WIDTH
    even_lanes = lax.broadcasted_iota(jnp.int32, (1, pw), 1) < RWKV_HEAD

    @pl.when(c == 0)
    def _():
        s_sc[...] = s0_ref[...].reshape(s_sc.shape)

    def bdot(a, s_parts):
        ab = a.astype(BF16)
        return sum(jnp.einsum('hij,hjk->hik', ab, p, preferred_element_type=F32) for p in s_parts)

    def block_diag(slab):
        zero = jnp.zeros_like(slab)
        return jnp.concatenate([jnp.where(even_lanes, slab, zero), jnp.where(even_lanes, zero, slab)], axis=1)

    def pairs(ref, rows):
        return [ref[0, bi, rows, p * pw:(p + 1) * pw] for bi in range(nb) for p in range(npair)]

    for j in range(nc):
        jb = nc - 1 - j
        rf = slice(j * CHUNK, (j + 1) * CHUNK)
        rb = slice(jb * CHUNK, (jb + 1) * CHUNK)
        s = _split_bf16(s_sc[...], 2)
        qe = jnp.stack(pairs(qf_ref, rf) + pairs(qb_ref, rb), axis=0)
        o0 = jnp.stack(pairs(of_ref, rf) + pairs(ob_ref, rb), axis=0)
        gm = jnp.concatenate([gf_ref[0, bi, j] for bi in range(nb)] + [gb_ref[0, bi, jb] for bi in range(nb)], axis=0)
        hm = jnp.concatenate([hf_ref[0, bi, j] for bi in range(nb)] + [hb_ref[0, bi, jb] for bi in range(nb)], axis=0)
        o = bdot(qe, s) + o0
        s_sc[...] = bdot(block_diag(gm), s) + block_diag(hm)
        for bi in range(nb):
            for p in range(npair):
                outf_ref[bi, rf, p * pw:(p + 1) * pw] = o[bi * npair + p]
                outb_ref[bi, rb, p * pw:(p + 1) * pw] = o[(nb + bi) * npair + p]

    @pl.when(c == pl.num_programs(0) - 1)
    def _():
        sfin_ref[...] = s_sc[...].reshape(sfin_ref.shape)


def _wkvscan(qeff, o0, g, h, s0):
    _, b, t, _ = qeff.shape
    tc = min(TOKEN_TILE, t)
    nc = tc // CHUNK
    nt = t // tc
    hd = (1, b, tc, RWKV_WIDTH)
    st = (1, b, nc) + PAIR_SLAB
    fwd_hd = pl.BlockSpec(hd, lambda c: (0, 0, c, 0))
    bwd_hd = pl.BlockSpec(hd, lambda c: (1, 0, nt - 1 - c, 0))
    fwd_st = pl.BlockSpec(st, lambda c: (0, 0, c, 0, 0, 0))
    bwd_st = pl.BlockSpec(st, lambda c: (1, 0, nt - 1 - c, 0, 0, 0))
    s_shape = (2, b, HEAD_PAIRS, PAIR_WIDTH, PAIR_WIDTH)
    s_spec = pl.BlockSpec(s_shape, lambda c: (0, 0, 0, 0, 0))
    return pl.pallas_call(
        _wkvscan_kernel,
        grid=(nt,),
        in_specs=[fwd_hd, bwd_hd, fwd_hd, bwd_hd, fwd_st, bwd_st, fwd_st, bwd_st, s_spec],
        out_specs=[
            pl.BlockSpec((b, tc, RWKV_WIDTH), lambda c: (0, c, 0)),
            pl.BlockSpec((b, tc, RWKV_WIDTH), lambda c: (0, nt - 1 - c, 0)),
            s_spec,
        ],
        out_shape=[
            jax.ShapeDtypeStruct((b, t, RWKV_WIDTH), F32),
            jax.ShapeDtypeStruct((b, t, RWKV_WIDTH), F32),
            jax.ShapeDtypeStruct(s_shape, F32),
        ],
        scratch_shapes=[pltpu.VMEM((2 * b * HEAD_PAIRS, PAIR_WIDTH, PAIR_WIDTH), F32)],
        compiler_params=_cparams(("arbitrary",)),
        name="wkvscan",
    )(qeff, qeff, o0, o0, g, g, h, h, s0)


def _merge_kernel(x_ref, mod_ref, ya_ref, ga_ref, of_ref, ob_ref, bonus_ref, gb_ref,
                  wpa_ref, wpb_ref, wmg_ref, bmg_ref, wo_ref, gnw_ref, gnb_ref, lng_ref, lnb_ref, seg_ref,
                  out_ref):
    x = x_ref[0]
    shift = mod_ref[0, 0:1, :]
    scale = mod_ref[0, 1:2, :]
    gate = mod_ref[0, 2:3, :]
    h = _layer_norm(x) * (1.0 + scale) + shift
    gates = jax.nn.sigmoid(jnp.dot(h.astype(BF16), wmg_ref[...], preferred_element_type=F32) + bmg_ref[...])
    ya = jnp.dot((ya_ref[0] * ga_ref[0].astype(F32)).astype(BF16), wpa_ref[...], preferred_element_type=F32)

    o = of_ref[0] + ob_ref[0]
    seg = seg_ref[...]
    mu = _dot_split_lhs(o, seg, SEG_SUM_PARTS) * (1.0 / RWKV_HEAD)
    oc = o - mu
    var = _dot_split_lhs(oc * oc, seg, SEG_SUM_PARTS) * (1.0 / RWKV_HEAD)
    y = oc * lax.rsqrt(var + GN_EPS) * gnw_ref[...] + gnb_ref[...] + bonus_ref[0]
    yb = jnp.dot((y * gb_ref[0].astype(F32)).astype(BF16), wpb_ref[...], preferred_element_type=F32)

    mixed = gates[:, 0:D_MODEL] * ya + gates[:, D_MODEL:2 * D_MODEL] * yb
    out = jnp.dot(mixed.astype(BF16), wo_ref[...], preferred_element_type=F32)
    out_ref[0] = _layer_norm(ALPHA * x + gate * out) * lng_ref[...] + lnb_ref[...]


def _merge(x, mod, y_att, ga, o_f, o_b, bonus, gb, w_pa, w_pb, w_mg, b_mg, w_o, gn_w, gn_b, ln_g, ln_b, seg):
    b, t, _ = x.shape
    tm = min(MERGE_TILE, t)
    assert t % tm == 0
    const = lambda shape: pl.BlockSpec(shape, lambda bi, i: (0,) * len(shape))
    tok = lambda width: pl.BlockSpec((1, tm, width), lambda bi, i: (bi, i, 0))
    return pl.pallas_call(
        _merge_kernel,
        grid=(b, t // tm),
        in_specs=[
            tok(D_MODEL),
            pl.BlockSpec((1, 3, D_MODEL), lambda bi, i: (bi, 0, 0)),
            tok(ATT_WIDTH),
            tok(ATT_WIDTH),
            tok(RWKV_WIDTH),
            tok(RWKV_WIDTH),
            tok(RWKV_WIDTH),
            tok(RWKV_WIDTH),
            const((ATT_WIDTH, D_MODEL)),
            const((RWKV_WIDTH, D_MODEL)),
            const((D_MODEL, 2 * D_MODEL)),
            const((1, 2 * D_MODEL)),
            const((D_MODEL, D_MODEL)),
            const((1, RWKV_WIDTH)),
            const((1, RWKV_WIDTH)),
            const((1, D_MODEL)),
            const((1, D_MODEL)),
            const((RWKV_WIDTH, RWKV_WIDTH)),
        ],
        out_specs=tok(D_MODEL),
        out_shape=jax.ShapeDtypeStruct((b, t, D_MODEL), F32),
        compiler_params=_cparams(("parallel", "parallel")),
        name="merge",
    )(x, mod, y_att, ga, o_f, o_b, bonus, gb, w_pa, w_pb, w_mg, b_mg, w_o, gn_w, gn_b, ln_g, ln_b, seg)


def _rope_tables_t(t):
    pos = jnp.arange(t, dtype=jnp.int32)
    row = (pos // GRID_W).astype(F32)
    col = (pos % GRID_W).astype(F32)
    inv = ROPE_BASE ** (-jnp.arange(ROPE_FREQS, dtype=F32) / ROPE_FREQS)
    ang_r = inv[:, None] * row[None, :]
    ang_c = inv[:, None] * col[None, :]
    cos_t = jnp.concatenate([jnp.cos(ang_r), jnp.cos(ang_r), jnp.cos(ang_c), jnp.cos(ang_c)], axis=0)
    sin_t = jnp.concatenate([-jnp.sin(ang_r), jnp.sin(ang_r), -jnp.sin(ang_c), jnp.sin(ang_c)], axis=0)
    return cos_t, sin_t


def kernel(x, c, ctx, c_ctx, w_ada, b_ada, w_in, q_norm, k_norm, mu_prev, mu_next, w0, w_up, a0, a_up,
           k_k, k_a, r_k, gn_w, gn_b, w_pa, w_pb, w_mg, b_mg, w_o, ln_g, ln_b):
    b, t, _ = x.shape
    tc = ctx.shape[1]
    l = 0
    seg = jnp.kron(jnp.eye(RWKV_HEADS, dtype=F32), jnp.ones((RWKV_HEAD, RWKV_HEAD), F32)).astype(BF16)

    rows = 8 * ((b + 1 + 7) // 8)
    cc = jnp.zeros((rows, D_MODEL), F32).at[0:b].set(c).at[b].set(c_ctx)
    ada = _ada(cc, w_ada[l], b_ada[l][None, :])
    mod_x = ada[0:b].reshape(b, 3, D_MODEL)
    mod_c = jnp.broadcast_to(ada[b].reshape(1, 3, D_MODEL), (b, 3, D_MODEL))

    w_t = w_in[l][:, 0:QKV_WIDTH].T.astype(BF16)
    w_n = w_in[l][:, QKV_WIDTH:].astype(BF16)
    q_gain = q_norm[l][:, None]
    k_gain = k_norm[l][:, None]
    cos_x, sin_x = _rope_tables_t(t)
    cos_c = jnp.ones((HEAD_DIM, tc), F32)
    sin_c = jnp.zeros((HEAD_DIM, tc), F32)

    q_x, k_x, q8_x, k8_x, v_x, ga_x, mix_x, gb_x = _inproj(x, mod_x, cos_x, sin_x, w_t, w_n, q_gain, k_gain)
    _, k_c, _, k8_c, v_c, _, mix_c, _ = _inproj(ctx, mod_c, cos_c, sin_c, w_t, w_n, q_gain, k_gain)

    v_all = jnp.concatenate([v_c, v_x], axis=2)
    fp8_ok = jnp.logical_and(
        jnp.max(jnp.abs(q_norm[l])) * (2.0 * HEAD_DIM ** 0.5 * HEAD_DIM ** -0.5 * math.log2(math.e) * FP8_Q_SCALE) < FP8_MAX,
        jnp.max(jnp.abs(k_norm[l])) * (2.0 * HEAD_DIM ** 0.5 / FP8_Q_SCALE) < FP8_MAX)
    y_att = lax.cond(
        fp8_ok,
        lambda: _attention(q8_x, jnp.concatenate([k8_c, k8_x], axis=2), v_all),
        lambda: _attention(q_x, jnp.concatenate([k_c, k_x], axis=2), v_all))

    prep_args = (mu_prev[l][None, :], mu_next[l][None, :], w0[l], w_up[l].astype(BF16), a0[l], a_up[l].astype(BF16),
                 k_k[l][None, :], k_a[l][None, :], r_k[l].reshape(2, RWKV_WIDTH), seg)
    _, qe_c, o0_c, g_c, h_c = _wkvprep(mix_c, *prep_args)
    zero_state = jnp.zeros((2, b, HEAD_PAIRS, PAIR_WIDTH, PAIR_WIDTH), F32)
    _, _, s_ctx = _wkvscan(qe_c, o0_c, g_c, h_c, zero_state)
    bonus, qe_x, o0_x, g_x, h_x = _wkvprep(mix_x, *prep_args)
    of_x, ob_x, _ = _wkvscan(qe_x, o0_x, g_x, h_x, s_ctx)

    return _merge(x, mod_x, y_att, ga_x, of_x, ob_x, bonus, gb_x,
                  w_pa[l].astype(BF16), w_pb[l].astype(BF16), w_mg[l].astype(BF16), b_mg[l][None, :],
                  w_o[l].astype(BF16), gn_w[l][None, :], gn_b[l][None, :], ln_g[l][None, :], ln_b[l][None, :], seg)
```

```python
import functools
import math

import jax
import jax.numpy as jnp
from jax import lax
from jax.experimental import pallas as pl
from jax.experimental.pallas import tpu as pltpu

F32 = jnp.float32
BF16 = jnp.bfloat16
FP8 = jnp.float8_e4m3fn
HIGHEST = lax.Precision.HIGHEST

D_MODEL = 1024
GRID_W = 64
HEAD_DIM = 64
N_Q_HEADS = 8
N_KV_HEADS = 2
GQA_GROUP = N_Q_HEADS // N_KV_HEADS
ATT_WIDTH = N_Q_HEADS * HEAD_DIM
KV_WIDTH = N_KV_HEADS * HEAD_DIM
QKV_WIDTH = ATT_WIDTH + 2 * KV_WIDTH
ROPE_FREQS = HEAD_DIM // 4
ROPE_BASE = 10000.0
RWKV_HEAD = 64
RWKV_HEADS = 8
RWKV_WIDTH = RWKV_HEADS * RWKV_HEAD
W_LORA = 64
A_LORA = 64
SHIFT_WIDTH = 3 * RWKV_WIDTH + 2 * W_LORA + 2 * A_LORA
IN_WIDTH = 2 * ATT_WIDTH + 2 * KV_WIDTH + SHIFT_WIDTH + RWKV_WIDTH
NAT_WIDTH = IN_WIDTH - QKV_WIDTH
LN_EPS = 1e-5
QK_EPS = 1e-6
GN_EPS = 64e-5
DEPTH = 1
ALPHA = (2.0 * DEPTH) ** 0.25

CHUNK = 64
KEY_BLOCK = 256
V_ROWS = HEAD_DIM + 8
FP8_DEPTH = 3 * HEAD_DIM
FP8_LO_SCALE = 16.0
FP8_Q_SCALE = 2.0
FP8_MAX = 448.0
TOKEN_TILE = 256
INPROJ_TILE = 512
MERGE_TILE = 512
Q_TILE = 512
ATTN_COLS = 256
ATTN_UNIT_BLOCKS = 4
ATTN_UNROLL = 4
WKV_GROUP = 2
SEG_SUM_PARTS = 1
HEAD_PAIRS = RWKV_HEADS // 2
PAIR_WIDTH = 2 * RWKV_HEAD
PAIR_SLAB = (HEAD_PAIRS, RWKV_HEAD, PAIR_WIDTH)
VMEM_LIMIT = 48 * 1024 * 1024
NEG_BIG = -1e30


def _cparams(sem):
    return pltpu.CompilerParams(dimension_semantics=sem, vmem_limit_bytes=VMEM_LIMIT)


def _silu(x):
    return x * jax.nn.sigmoid(x)


def _layer_norm(x):
    mu = jnp.mean(x, axis=-1, keepdims=True)
    xc = x - mu
    var = jnp.mean(xc * xc, axis=-1, keepdims=True)
    return xc * lax.rsqrt(var + LN_EPS)


def _split_bf16(x, parts):
    out = []
    rem = x
    for _ in range(parts):
        p = rem.astype(BF16)
        out.append(p)
        rem = rem - p.astype(F32)
    return out


def _dot_split_lhs(x, m_bf16, parts):
    acc = None
    for p in _split_bf16(x, parts):
        t = jnp.dot(p, m_bf16, preferred_element_type=F32)
        acc = t if acc is None else acc + t
    return acc


def _dot_split_rhs(m_bf16, x, parts):
    acc = None
    for p in _split_bf16(x, parts):
        t = jnp.dot(m_bf16, p, preferred_element_type=F32)
        acc = t if acc is None else acc + t
    return acc


def _ada_kernel(c_ref, w_ref, b_ref, o_ref):
    s = _silu(c_ref[...])
    o_ref[...] = jnp.dot(s, w_ref[...], precision=HIGHEST, preferred_element_type=F32) + b_ref[...]


def _ada(cc, w_ada, b_ada):
    rows = cc.shape[0]
    nblk = (3 * D_MODEL) // D_MODEL
    return pl.pallas_call(
        _ada_kernel,
        grid=(nblk,),
        in_specs=[
            pl.BlockSpec((rows, D_MODEL), lambda j: (0, 0)),
            pl.BlockSpec((D_MODEL, D_MODEL), lambda j: (0, j)),
            pl.BlockSpec((1, D_MODEL), lambda j: (0, j)),
        ],
        out_specs=pl.BlockSpec((rows, D_MODEL), lambda j: (0, j)),
        out_shape=jax.ShapeDtypeStruct((rows, 3 * D_MODEL), F32),
        compiler_params=_cparams(("arbitrary",)),
        name="ada",
    )(cc, w_ada, b_ada)


def _rope_t(x, cos, sin):
    xs = jnp.concatenate([x[:, 16:32], x[:, 0:16], x[:, 48:64], x[:, 32:48]], axis=1)
    return x * cos + xs * sin


def _fp8_pieces(x):
    hi = x.astype(FP8).astype(F32)
    lo16 = ((x - hi) * FP8_LO_SCALE).astype(FP8).astype(F32)
    hi16 = (hi * (1.0 / FP8_LO_SCALE)).astype(FP8).astype(F32)
    return hi, hi16, lo16


def _inproj_kernel(x_ref, mod_ref, cos_ref, sin_ref, wt_ref, wn_ref, qg_ref, kg_ref,
                   q_ref, k_ref, q8_ref, k8_ref, v_ref, ga_ref, mix_ref, gb_ref):
    tm = x_ref.shape[1]
    x = x_ref[0]
    shift = mod_ref[0, 0:1, :]
    scale = mod_ref[0, 1:2, :]
    h = _layer_norm(x) * (1.0 + scale) + shift
    hb = h.astype(BF16)
    zt = lax.dot_general(wt_ref[...], hb, (((1,), (1,)), ((), ())), preferred_element_type=F32)
    zn = jnp.dot(hb, wn_ref[...], preferred_element_type=F32)
    cos = cos_ref[...][None]
    sin = sin_ref[...][None]

    qz = zt[0:ATT_WIDTH].reshape(N_Q_HEADS, HEAD_DIM, tm)
    qn = qz * lax.rsqrt(jnp.mean(qz * qz, axis=1, keepdims=True) + QK_EPS) * qg_ref[...][None]
    qr = _rope_t(qn, cos, sin) * (HEAD_DIM ** -0.5 * math.log2(math.e))
    q_ref[0] = qr.astype(BF16)
    q_hi, q_hi16, q_lo16 = _fp8_pieces(qr * FP8_Q_SCALE)
    q8_ref[0] = jnp.concatenate([q_hi, q_hi16, q_lo16], axis=1).astype(FP8)

    kz = zt[ATT_WIDTH:ATT_WIDTH + KV_WIDTH].reshape(N_KV_HEADS, HEAD_DIM, tm)
    kn = kz * lax.rsqrt(jnp.mean(kz * kz, axis=1, keepdims=True) + QK_EPS) * kg_ref[...][None]
    kr = _rope_t(kn, cos, sin)
    k_hi, k_hi16, k_lo16 = _fp8_pieces(kr * (1.0 / FP8_Q_SCALE))
    k8 = jnp.concatenate([k_hi, k_lo16, k_hi16], axis=1)
    vz = zt[ATT_WIDTH + KV_WIDTH:QKV_WIDTH].reshape(N_KV_HEADS, HEAD_DIM, tm)
    pad_row = lax.broadcasted_iota(jnp.int32, (V_ROWS - HEAD_DIM, KEY_BLOCK), 0)
    ones_pad = jnp.where(pad_row == 0, 1.0, 0.0).astype(BF16)
    for j in range(N_KV_HEADS):
        for kb in range(tm // KEY_BLOCK):
            cols = slice(kb * KEY_BLOCK, (kb + 1) * KEY_BLOCK)
            k_ref[0, j, kb] = kr[j][:, cols].T.astype(BF16)
            k8_ref[0, j, kb] = k8[j][:, cols].T.astype(FP8)
            v_ref[0, j, kb, 0:HEAD_DIM, :] = vz[j][:, cols].astype(BF16)
            v_ref[0, j, kb, HEAD_DIM:V_ROWS, :] = ones_pad

    ga_ref[0] = _silu(zn[:, 0:ATT_WIDTH]).astype(BF16)
    mix_ref[0] = zn[:, ATT_WIDTH:ATT_WIDTH + SHIFT_WIDTH]
    gb_ref[0] = _silu(zn[:, ATT_WIDTH + SHIFT_WIDTH:NAT_WIDTH]).astype(BF16)


def _inproj(x, mod, cos_t, sin_t, w_t, w_n, q_gain, k_gain):
    b, t, _ = x.shape
    tm = min(INPROJ_TILE, t)
    assert tm % KEY_BLOCK == 0 and t % tm == 0
    nt = t // tm
    kbt = tm // KEY_BLOCK
    nkb = t // KEY_BLOCK
    const = lambda shape: pl.BlockSpec(shape, lambda bi, i: (0,) * len(shape))
    return pl.pallas_call(
        _inproj_kernel,
        grid=(b, nt),
        in_specs=[
            pl.BlockSpec((1, tm, D_MODEL), lambda bi, i: (bi, i, 0)),
            pl.BlockSpec((1, 3, D_MODEL), lambda bi, i: (bi, 0, 0)),
            pl.BlockSpec((HEAD_DIM, tm), lambda bi, i: (0, i)),
            pl.BlockSpec((HEAD_DIM, tm), lambda bi, i: (0, i)),
            const((QKV_WIDTH, D_MODEL)),
            const((D_MODEL, NAT_WIDTH)),
            const((HEAD_DIM, 1)),
            const((HEAD_DIM, 1)),
        ],
        out_specs=[
            pl.BlockSpec((1, N_Q_HEADS, HEAD_DIM, tm), lambda bi, i: (bi, 0, 0, i)),
            pl.BlockSpec((1, N_KV_HEADS, kbt, KEY_BLOCK, HEAD_DIM), lambda bi, i: (bi, 0, i, 0, 0)),
            pl.BlockSpec((1, N_Q_HEADS, FP8_DEPTH, tm), lambda bi, i: (bi, 0, 0, i)),
            pl.BlockSpec((1, N_KV_HEADS, kbt, KEY_BLOCK, FP8_DEPTH), lambda bi, i: (bi, 0, i, 0, 0)),
            pl.BlockSpec((1, N_KV_HEADS, kbt, V_ROWS, KEY_BLOCK), lambda bi, i: (bi, 0, i, 0, 0)),
            pl.BlockSpec((1, tm, ATT_WIDTH), lambda bi, i: (bi, i, 0)),
            pl.BlockSpec((1, tm, SHIFT_WIDTH), lambda bi, i: (bi, i, 0)),
            pl.BlockSpec((1, tm, RWKV_WIDTH), lambda bi, i: (bi, i, 0)),
        ],
        out_shape=[
            jax.ShapeDtypeStruct((b, N_Q_HEADS, HEAD_DIM, t), BF16),
            jax.ShapeDtypeStruct((b, N_KV_HEADS, nkb, KEY_BLOCK, HEAD_DIM), BF16),
            jax.ShapeDtypeStruct((b, N_Q_HEADS, FP8_DEPTH, t), FP8),
            jax.ShapeDtypeStruct((b, N_KV_HEADS, nkb, KEY_BLOCK, FP8_DEPTH), FP8),
            jax.ShapeDtypeStruct((b, N_KV_HEADS, nkb, V_ROWS, KEY_BLOCK), BF16),
            jax.ShapeDtypeStruct((b, t, ATT_WIDTH), BF16),
            jax.ShapeDtypeStruct((b, t, SHIFT_WIDTH), F32),
            jax.ShapeDtypeStruct((b, t, RWKV_WIDTH), BF16),
        ],
        compiler_params=_cparams(("parallel", "parallel")),
        name="inproj",
    )(x, mod, cos_t, sin_t, w_t, w_n, q_gain, k_gain)


def _attn_kernel(q_ref, k_ref, v_ref, o_ref, s_sc, cmax_sc, m_sc, acc_sc):
    nkb = k_ref.shape[2]
    tq = q_ref.shape[3]
    width = GQA_GROUP * tq
    nchunk = width // ATTN_COLS
    per_head = tq // ATTN_COLS
    kb = KEY_BLOCK
    kbt = ATTN_UNIT_BLOCKS
    m_sc[...] = jnp.full(m_sc.shape, NEG_BIG, F32)
    acc_sc[...] = jnp.zeros(acc_sc.shape, F32)
    first = nkb % kbt if nkb % kbt else kbt
    nunits = 1 + (nkb - first) // kbt

    def unit_start(u):
        return first + (u - 1) * kbt

    def tick(u, par, nb_acc, nb_score):
        if nb_acc:
            acc0 = 0 if (isinstance(u, int) and u == 1) else unit_start(u - 1)
            m_old = m_sc[...]
            m_new = jnp.maximum(m_old, cmax_sc[1 - par])
            alpha = jnp.exp2(m_old - m_new)
            m_sc[...] = m_new
        if nb_score:
            sc0 = 0 if (isinstance(u, int) and u == 0) else unit_start(u)
        for ch in range(nchunk):
            cols = slice(ch * ATTN_COLS, (ch + 1) * ATTN_COLS)
            if nb_acc:
                pv = None
                for i in range(nb_acc):
                    p = jnp.exp2(s_sc[1 - par, i * kb:(i + 1) * kb, cols] - m_new[:, cols]).astype(BF16)
                    t = jnp.dot(v_ref[0, 0, acc0 + i], p, preferred_element_type=F32)
                    pv = t if pv is None else pv + t
                acc_sc[:, cols] = acc_sc[:, cols] * alpha[:, cols] + pv
            if nb_score:
                g, h = divmod(ch, per_head)
                qc = q_ref[0, g, :, h * ATTN_COLS:(h + 1) * ATTN_COLS]
                cm = None
                for i in range(nb_score):
                    s = jnp.dot(k_ref[0, 0, sc0 + i], qc, preferred_element_type=F32)
                    s_sc[par, i * kb:(i + 1) * kb, cols] = s
                    c1 = jnp.max(s, axis=0, keepdims=True)
                    cm = c1 if cm is None else jnp.maximum(cm, c1)
                cmax_sc[par, :, cols] = cm

    tick(0, 0, 0, first)
    if nunits > 1:
        tick(1, 1, first, kbt)
    nsteady = max(nunits - 2, 0)
    npeel = nsteady % ATTN_UNROLL
    for u in range(2, 2 + npeel):
        tick(u, u % 2, kbt, kbt)
    base = 2 + npeel

    def group(j, carry):
        for i in range(ATTN_UNROLL):
            tick(base + ATTN_UNROLL * j + i, (base + i) % 2, kbt, kbt)
        return carry

    lax.fori_loop(0, (nsteady - npeel) // ATTN_UNROLL, group, 0)
    tick(nunits, nunits % 2, kbt if nunits > 1 else first, 0)
    for g in range(GQA_GROUP):
        acc = acc_sc[:, g * tq:(g + 1) * tq]
        o = acc[0:HEAD_DIM] / acc[HEAD_DIM:HEAD_DIM + 1]
        o_ref[0, :, g * HEAD_DIM:(g + 1) * HEAD_DIM] = o.T


def _attention(q_t, k_blk, v_blk):
    b, _, depth, t = q_t.shape
    nkb = k_blk.shape[2]
    tq = min(Q_TILE, t)
    assert t % tq == 0 and k_blk.shape[4] == depth
    width = GQA_GROUP * tq
    return pl.pallas_call(
        _attn_kernel,
        grid=(b, N_KV_HEADS, t // tq),
        in_specs=[
            pl.BlockSpec((1, GQA_GROUP, depth, tq), lambda bi, j, i: (bi, j, 0, i)),
            pl.BlockSpec((1, 1, nkb, KEY_BLOCK, depth), lambda bi, j, i: (bi, j, 0, 0, 0)),
            pl.BlockSpec((1, 1, nkb, V_ROWS, KEY_BLOCK), lambda bi, j, i: (bi, j, 0, 0, 0)),
        ],
        out_specs=pl.BlockSpec((1, tq, GQA_GROUP * HEAD_DIM), lambda bi, j, i: (bi, i, j)),
        out_shape=jax.ShapeDtypeStruct((b, t, ATT_WIDTH), F32),
        scratch_shapes=[
            pltpu.VMEM((2, ATTN_UNIT_BLOCKS * KEY_BLOCK, width), F32),
            pltpu.VMEM((2, 1, width), F32),
            pltpu.VMEM((1, width), F32),
            pltpu.VMEM((V_ROWS, width), F32),
        ],
        compiler_params=_cparams(("parallel", "parallel", "parallel")),
        name="attn",
    )(q_t, k_blk, v_blk)


def _bmm(a, b):
    return jnp.einsum('hij,hjk->hik', a.astype(BF16), b.astype(BF16), preferred_element_type=F32)


def _bmm_nt(a, b):
    return jnp.einsum('hik,hjk->hij', a.astype(BF16), b.astype(BF16), preferred_element_type=F32)


def _wkvprep_kernel(mix_ref, prev_ref, next_ref, mup_ref, mun_ref, w0_ref, wup_ref, a0_ref, aup_ref,
                    kk_ref, ka_ref, rk_ref, seg_ref,
                    bonus_ref, qeff_ref, o0_ref, g_ref, h_ref,
                    r_sc, v_sc, kn_sc, ld_sc, kd_sc, b_sc):
    tm = mix_ref.shape[1]
    i = pl.program_id(1)
    n = pl.num_programs(1)
    z = mix_ref[0]
    prev_row = jnp.where(i > 0, prev_ref[0, 7:8, :], 0.0)
    next_row = jnp.where(i < n - 1, next_ref[0, 0:1, :], 0.0)
    rows = lax.broadcasted_iota(jnp.int32, (tm, 1), 0)
    zp = jnp.where(rows == 0, prev_row, pltpu.roll(z, 1, 0))
    zn = jnp.where(rows == tm - 1, next_row, pltpu.roll(z, tm - 1, 0))
    zs = z + mup_ref[...] * (zp - z) + mun_ref[...] * (zn - z)

    w = RWKV_WIDTH
    r = zs[:, 0:w]
    k = zs[:, w:2 * w]
    v = zs[:, 2 * w:3 * w]
    seg = seg_ref[...]
    kkr = k * kk_ref[...]
    norm = jnp.sqrt(_dot_split_lhs(kkr * kkr, seg, SEG_SUM_PARTS))
    kn = kkr / jnp.maximum(norm, 1e-12)
    r_sc[...] = r
    v_sc[...] = v
    kn_sc[...] = kn
    coef = jnp.zeros((tm, w), F32)
    for d in range(2):
        wl = zs[:, 3 * w + d * W_LORA:3 * w + (d + 1) * W_LORA]
        al = zs[:, 3 * w + 2 * W_LORA + d * A_LORA:3 * w + 2 * W_LORA + (d + 1) * A_LORA]
        w_pre = w0_ref[d:d + 1, :] + jnp.dot(jnp.tanh(wl).astype(BF16), wup_ref[d], preferred_element_type=F32)
        ld_sc[d] = -math.exp(-0.5) * jax.nn.sigmoid(w_pre)
        a = jax.nn.sigmoid(a0_ref[d:d + 1, :] + jnp.dot(al.astype(BF16), aup_ref[d], preferred_element_type=F32))
        kd = k * (1.0 + (a - 1.0) * ka_ref[...])
        kd_sc[d] = kd
        b_sc[d] = kn * a
        coef = coef + _dot_split_lhs(r * kd * rk_ref[d:d + 1, :], seg, SEG_SUM_PARTS)
    bonus_ref[0] = coef * v

    ri = lax.broadcasted_iota(jnp.int32, (CHUNK, CHUNK), 0)
    ci = lax.broadcasted_iota(jnp.int32, (CHUNK, CHUNK), 1)
    eye_f = jnp.where(ri == ci, 1.0, 0.0)
    strict_f = [jnp.where(ci < ri, 1.0, 0.0), jnp.where(ci > ri, 1.0, 0.0)]
    incl_f = [s + eye_f for s in strict_f]
    npair = RWKV_HEADS // 2
    pw = 2 * RWKV_HEAD
    lane = lax.broadcasted_iota(jnp.int32, (1, pw), 1)
    even_lanes = lane < RWKV_HEAD
    m_even = jnp.where(even_lanes, 1.0, 0.0)
    m_odd = 1.0 - m_even
    even_lanes2 = jnp.concatenate([even_lanes, even_lanes], axis=1)
    r128 = lax.broadcasted_iota(jnp.int32, (pw, pw), 0)
    c128 = lax.broadcasted_iota(jnp.int32, (pw, pw), 1)
    eye_pw = jnp.where(r128 == c128, 1.0, 0.0)

    def _heads(x):
        return jnp.stack([x[:, p * pw:(p + 1) * pw] for p in range(npair)], axis=0)

    def group(gi, carry):
        combos = [(gi * WKV_GROUP + u, d) for u in range(WKV_GROUP) for d in range(2)]
        slices = []
        parts = {name: [] for name in ("rh", "kkh", "kt", "bt", "kv", "bv", "gam", "vh")}
        for c, d in combos:
            sl = pl.ds(pl.multiple_of(c * CHUNK, CHUNK), CHUNK)
            slices.append(sl)
            ld = ld_sc[d, sl, :]
            cum = _dot_split_rhs(incl_f[d].astype(BF16), ld, 3)
            tot = cum[CHUNK - 1:CHUNK, :] if d == 0 else cum[0:1, :]
            e_pos = jnp.exp(cum)
            e_prev = jnp.exp(cum - ld)
            e_neg = jnp.exp(-cum)
            e_rest = jnp.exp(tot - cum)
            kd_c = kd_sc[d, sl, :]
            b_c = b_sc[d, sl, :]
            parts["rh"].append(_heads(r_sc[sl, :] * e_pos))
            parts["kkh"].append(_heads(kn_sc[sl, :] * e_prev))
            parts["kt"].append(_heads(kd_c * e_neg))
            parts["bt"].append(_heads(b_c * e_neg))
            parts["kv"].append(_heads(kd_c * e_rest))
            parts["bv"].append(_heads(b_c * e_rest))
            parts["gam"].append(_heads(jnp.exp(tot)))
            parts["vh"].append(_heads(v_sc[sl, :]))
        cat = lambda xs: jnp.concatenate(xs, axis=0)
        rh, kkh, kt, bt, kv, bv, gam, vh = (cat(parts[n]) for n in ("rh", "kkh", "kt", "bt", "kv", "bv", "gam", "vh"))
        nbp = len(combos) * npair
        per = lambda ms: cat([jnp.broadcast_to(ms[d][None], (npair, CHUNK, CHUNK)) for _, d in combos])
        both = lambda x: jnp.concatenate([x, x], axis=0)
        masked = lambda x: jnp.concatenate([x * m_even, x * m_odd], axis=0)
        pick = lambda z, ev: jnp.where(ev, z[0:nbp], z[nbp:2 * nbp])
        strict = both(per(strict_f))
        incl = both(per(incl_f))

        kk_m = masked(kkh)
        r_m = masked(rh)
        kt2 = both(kt)
        bt2 = both(bt)
        v2 = both(vh)
        lk = _bmm_nt(kk_m, kt2) * strict
        lb = _bmm_nt(kk_m, bt2) * strict
        ark = _bmm_nt(r_m, kt2) * incl
        arb = _bmm_nt(r_m, bt2) * incl
        xp = -lb
        tinv = eye_f[None] + xp
        span = 2
        while span < CHUNK:
            xp = _bmm(xp, xp)
            tinv = tinv + _bmm(tinv, xp)
            span *= 2
        wu_h = _bmm(tinv, jnp.concatenate([both(kkh), _bmm(lk, v2)], axis=2))
        arb_wu = pick(_bmm(arb, wu_h), even_lanes2)
        wu = pick(wu_h, even_lanes2)
        qeff = rh - arb_wu[:, :, 0:pw]
        o0 = pick(_bmm(ark, v2), even_lanes) - arb_wu[:, :, pw:2 * pw]
        bvt_wu = _bmm(jnp.swapaxes(bv, 1, 2), wu)
        kvt_v = _bmm(jnp.swapaxes(kv, 1, 2), vh)
        gfull = eye_pw[None] * gam - bvt_wu[:, :, 0:pw]
        hfull = kvt_v - bvt_wu[:, :, pw:2 * pw]
        slab = lambda z: jnp.where(even_lanes, z[:, 0:RWKV_HEAD, :], z[:, RWKV_HEAD:pw, :])
        gmat = slab(gfull)
        hmat = slab(hfull)
        for idx, (c, d) in enumerate(combos):
            for p in range(npair):
                qeff_ref[d, 0, slices[idx], p * pw:(p + 1) * pw] = qeff[idx * npair + p].astype(BF16)
                o0_ref[d, 0, slices[idx], p * pw:(p + 1) * pw] = o0[idx * npair + p]
            g_ref[d, 0, c] = gmat[idx * npair:(idx + 1) * npair].astype(BF16)
            h_ref[d, 0, c] = hmat[idx * npair:(idx + 1) * npair]
        return carry

    lax.fori_loop(0, tm // (CHUNK * WKV_GROUP), group, 0)


def _wkvprep(mix, mu_prev, mu_next, w0, w_up, a0, a_up, k_k, k_a, r_k, seg):
    b, t, _ = mix.shape
    tm = min(TOKEN_TILE, t)
    assert t % tm == 0 and tm % CHUNK == 0
    nt = t // tm
    nc = tm // CHUNK
    hb = tm // 8
    last8 = t // 8 - 1
    const = lambda shape: pl.BlockSpec(shape, lambda bi, i: (0,) * len(shape))
    hd = (2, 1, tm, RWKV_WIDTH)
    st = (2, 1, nc) + PAIR_SLAB
    return pl.pallas_call(
        _wkvprep_kernel,
        grid=(b, nt),
        in_specs=[
            pl.BlockSpec((1, tm, SHIFT_WIDTH), lambda bi, i: (bi, i, 0)),
            pl.BlockSpec((1, 8, SHIFT_WIDTH), lambda bi, i: (bi, jnp.maximum(i * hb - 1, 0), 0)),
            pl.BlockSpec((1, 8, SHIFT_WIDTH), lambda bi, i: (bi, jnp.minimum((i + 1) * hb, last8), 0)),
            const((1, SHIFT_WIDTH)),
            const((1, SHIFT_WIDTH)),
            const((2, RWKV_WIDTH)),
            const((2, W_LORA, RWKV_WIDTH)),
            const((2, RWKV_WIDTH)),
            const((2, A_LORA, RWKV_WIDTH)),
            const((1, RWKV_WIDTH)),
            const((1, RWKV_WIDTH)),
            const((2, RWKV_WIDTH)),
            const((RWKV_WIDTH, RWKV_WIDTH)),
        ],
        out_specs=[
            pl.BlockSpec((1, tm, RWKV_WIDTH), lambda bi, i: (bi, i, 0)),
            pl.BlockSpec(hd, lambda bi, i: (0, bi, i, 0)),
            pl.BlockSpec(hd, lambda bi, i: (0, bi, i, 0)),
            pl.BlockSpec(st, lambda bi, i: (0, bi, i, 0, 0, 0)),
            pl.BlockSpec(st, lambda bi, i: (0, bi, i, 0, 0, 0)),
        ],
        out_shape=[
            jax.ShapeDtypeStruct((b, t, RWKV_WIDTH), F32),
            jax.ShapeDtypeStruct((2, b, t, RWKV_WIDTH), BF16),
            jax.ShapeDtypeStruct((2, b, t, RWKV_WIDTH), F32),
            jax.ShapeDtypeStruct((2, b, t // CHUNK) + PAIR_SLAB, BF16),
            jax.ShapeDtypeStruct((2, b, t // CHUNK) + PAIR_SLAB, F32),
        ],
        scratch_shapes=[
            pltpu.VMEM((tm, RWKV_WIDTH), F32),
            pltpu.VMEM((tm, RWKV_WIDTH), F32),
            pltpu.VMEM((tm, RWKV_WIDTH), F32),
            pltpu.VMEM((2, tm, RWKV_WIDTH), F32),
            pltpu.VMEM((2, tm, RWKV_WIDTH), F32),
            pltpu.VMEM((2, tm, RWKV_WIDTH), F32),
        ],
        compiler_params=_cparams(("parallel", "parallel")),
        name="wkvprep",
    )(mix, mix, mix, mu_prev, mu_next, w0, w_up, a0, a_up, k_k, k_a, r_k, seg)


def _wkvscan_kernel(qf_ref, qb_ref, of_ref, ob_ref, gf_ref, gb_ref, hf_ref, hb_ref, s0_ref,
                    outf_ref, outb_ref, sfin_ref, s_sc):
    c = pl.program_id(0)
    nb = qf_ref.shape[1]
    nc = gf_ref.shape[2]
    npair = HEAD_PAIRS
    pw = PAIR_WIDTH
    even_lanes = lax.broadcasted_iota(jnp.int32, (1, pw), 1) < RWKV_HEAD

    @pl.when(c == 0)
    def _():
        s_sc[...] = s0_ref[...].reshape(s_sc.shape)

    def bdot(a, s_parts):
        ab = a.astype(BF16)
        return sum(jnp.einsum('hij,hjk->hik', ab, p, preferred_element_type=F32) for p in s_parts)

    def block_diag(slab):
        zero = jnp.zeros_like(slab)
        return jnp.concatenate([jnp.where(even_lanes, slab, zero), jnp.where(even_lanes, zero, slab)], axis=1)

    def pairs(ref, rows):
        return [ref[0, bi, rows, p * pw:(p + 1) * pw] for bi in range(nb) for p in range(npair)]

    for j in range(nc):
        jb = nc - 1 - j
        rf = slice(j * CHUNK, (j + 1) * CHUNK)
        rb = slice(jb * CHUNK, (jb + 1) * CHUNK)
        s = _split_bf16(s_sc[...], 2)
        qe = jnp.stack(pairs(qf_ref, rf) + pairs(qb_ref, rb), axis=0)
        o0 = jnp.stack(pairs(of_ref, rf) + pairs(ob_ref, rb), axis=0)
        gm = jnp.concatenate([gf_ref[0, bi, j] for bi in range(nb)] + [gb_ref[0, bi, jb] for bi in range(nb)], axis=0)
        hm = jnp.concatenate([hf_ref[0, bi, j] for bi in range(nb)] + [hb_ref[0, bi, jb] for bi in range(nb)], axis=0)
        o = bdot(qe, s) + o0
        s_sc[...] = bdot(block_diag(gm), s) + block_diag(hm)
        for bi in range(nb):
            for p in range(npair):
                outf_ref[bi, rf, p * pw:(p + 1) * pw] = o[bi * npair + p]
                outb_ref[bi, rb, p * pw:(p + 1) * pw] = o[(nb + bi) * npair + p]

    @pl.when(c == pl.num_programs(0) - 1)
    def _():
        sfin_ref[...] = s_sc[...].reshape(sfin_ref.shape)


def _wkvscan(qeff, o0, g, h, s0):
    _, b, t, _ = qeff.shape
    tc = min(TOKEN_TILE, t)
    nc = tc // CHUNK
    nt = t // tc
    hd = (1, b, tc, RWKV_WIDTH)
    st = (1, b, nc) + PAIR_SLAB
    fwd_hd = pl.BlockSpec(hd, lambda c: (0, 0, c, 0))
    bwd_hd = pl.BlockSpec(hd, lambda c: (1, 0, nt - 1 - c, 0))
    fwd_st = pl.BlockSpec(st, lambda c: (0, 0, c, 0, 0, 0))
    bwd_st = pl.BlockSpec(st, lambda c: (1, 0, nt - 1 - c, 0, 0, 0))
    s_shape = (2, b, HEAD_PAIRS, PAIR_WIDTH, PAIR_WIDTH)
    s_spec = pl.BlockSpec(s_shape, lambda c: (0, 0, 0, 0, 0))
    return pl.pallas_call(
        _wkvscan_kernel,
        grid=(nt,),
        in_specs=[fwd_hd, bwd_hd, fwd_hd, bwd_hd, fwd_st, bwd_st, fwd_st, bwd_st, s_spec],
        out_specs=[
            pl.BlockSpec((b, tc, RWKV_WIDTH), lambda c: (0, c, 0)),
            pl.BlockSpec((b, tc, RWKV_WIDTH), lambda c: (0, nt - 1 - c, 0)),
            s_spec,
        ],
        out_shape=[
            jax.ShapeDtypeStruct((b, t, RWKV_WIDTH), F32),
            jax.ShapeDtypeStruct((b, t, RWKV_WIDTH), F32),
            jax.ShapeDtypeStruct(s_shape, F32),
        ],
        scratch_shapes=[pltpu.VMEM((2 * b * HEAD_PAIRS, PAIR_WIDTH, PAIR_WIDTH), F32)],
        compiler_params=_cparams(("arbitrary",)),
        name="wkvscan",
    )(qeff, qeff, o0, o0, g, g, h, h, s0)


def _merge_kernel(x_ref, mod_ref, ya_ref, ga_ref, of_ref, ob_ref, bonus_ref, gb_ref,
                  wpa_ref, wpb_ref, wmg_ref, bmg_ref, wo_ref, gnw_ref, gnb_ref, lng_ref, lnb_ref, seg_ref,
                  out_ref):
    x = x_ref[0]
    shift = mod_ref[0, 0:1, :]
    scale = mod_ref[0, 1:2, :]
    gate = mod_ref[0, 2:3, :]
    h = _layer_norm(x) * (1.0 + scale) + shift
    gates = jax.nn.sigmoid(jnp.dot(h.astype(BF16), wmg_ref[...], preferred_element_type=F32) + bmg_ref[...])
    ya = jnp.dot((ya_ref[0] * ga_ref[0].astype(F32)).astype(BF16), wpa_ref[...], preferred_element_type=F32)

    o = of_ref[0] + ob_ref[0]
    seg = seg_ref[...]
    mu = _dot_split_lhs(o, seg, SEG_SUM_PARTS) * (1.0 / RWKV_HEAD)
    oc = o - mu
    var = _dot_split_lhs(oc * oc, seg, SEG_SUM_PARTS) * (1.0 / RWKV_HEAD)
    y = oc * lax.rsqrt(var + GN_EPS) * gnw_ref[...] + gnb_ref[...] + bonus_ref[0]
    yb = jnp.dot((y * gb_ref[0].astype(F32)).astype(BF16), wpb_ref[...], preferred_element_type=F32)

    mixed = gates[:, 0:D_MODEL] * ya + gates[:, D_MODEL:2 * D_MODEL] * yb
    out = jnp.dot(mixed.astype(BF16), wo_ref[...], preferred_element_type=F32)
    out_ref[0] = _layer_norm(ALPHA * x + gate * out) * lng_ref[...] + lnb_ref[...]


def _merge(x, mod, y_att, ga, o_f, o_b, bonus, gb, w_pa, w_pb, w_mg, b_mg, w_o, gn_w, gn_b, ln_g, ln_b, seg):
    b, t, _ = x.shape
    tm = min(MERGE_TILE, t)
    assert t % tm == 0
    const = lambda shape: pl.BlockSpec(shape, lambda bi, i: (0,) * len(shape))
    tok = lambda width: pl.BlockSpec((1, tm, width), lambda bi, i: (bi, i, 0))
    return pl.pallas_call(
        _merge_kernel,
        grid=(b, t // tm),
        in_specs=[
            tok(D_MODEL),
            pl.BlockSpec((1, 3, D_MODEL), lambda bi, i: (bi, 0, 0)),
            tok(ATT_WIDTH),
            tok(ATT_WIDTH),
            tok(RWKV_WIDTH),
            tok(RWKV_WIDTH),
            tok(RWKV_WIDTH),
            tok(RWKV_WIDTH),
            const((ATT_WIDTH, D_MODEL)),
            const((RWKV_WIDTH, D_MODEL)),
            const((D_MODEL, 2 * D_MODEL)),
            const((1, 2 * D_MODEL)),
            const((D_MODEL, D_MODEL)),
            const((1, RWKV_WIDTH)),
            const((1, RWKV_WIDTH)),
            const((1, D_MODEL)),
            const((1, D_MODEL)),
            const((RWKV_WIDTH, RWKV_WIDTH)),
        ],
        out_specs=tok(D_MODEL),
        out_shape=jax.ShapeDtypeStruct((b, t, D_MODEL), F32),
        compiler_params=_cparams(("parallel", "parallel")),
        name="merge",
    )(x, mod, y_att, ga, o_f, o_b, bonus, gb, w_pa, w_pb, w_mg, b_mg, w_o, gn_w, gn_b, ln_g, ln_b, seg)


def _rope_tables_t(t):
    pos = jnp.arange(t, dtype=jnp.int32)
    row = (pos // GRID_W).astype(F32)
    col = (pos % GRID_W).astype(F32)
    inv = ROPE_BASE ** (-jnp.arange(ROPE_FREQS, dtype=F32) / ROPE_FREQS)
    ang_r = inv[:, None] * row[None, :]
    ang_c = inv[:, None] * col[None, :]
    cos_t = jnp.concatenate([jnp.cos(ang_r), jnp.cos(ang_r), jnp.cos(ang_c), jnp.cos(ang_c)], axis=0)
    sin_t = jnp.concatenate([-jnp.sin(ang_r), jnp.sin(ang_r), -jnp.sin(ang_c), jnp.sin(ang_c)], axis=0)
    return cos_t, sin_t


def kernel(x, c, ctx, c_ctx, w_ada, b_ada, w_in, q_norm, k_norm, mu_prev, mu_next, w0, w_up, a0, a_up,
           k_k, k_a, r_k, gn_w, gn_b, w_pa, w_pb, w_mg, b_mg, w_o, ln_g, ln_b):
    b, t, _ = x.shape
    tc = ctx.shape[1]
    l = 0
    seg = jnp.kron(jnp.eye(RWKV_HEADS, dtype=F32), jnp.ones((RWKV_HEAD, RWKV_HEAD), F32)).astype(BF16)

    rows = 8 * ((b + 1 + 7) // 8)
    cc = jnp.zeros((rows, D_MODEL), F32).at[0:b].set(c).at[b].set(c_ctx)
    ada = _ada(cc, w_ada[l], b_ada[l][None, :])
    mod_x = ada[0:b].reshape(b, 3, D_MODEL)
    mod_c = jnp.broadcast_to(ada[b].reshape(1, 3, D_MODEL), (b, 3, D_MODEL))

    w_t = w_in[l][:, 0:QKV_WIDTH].T.astype(BF16)
    w_n = w_in[l][:, QKV_WIDTH:].astype(BF16)
    q_gain = q_norm[l][:, None]
    k_gain = k_norm[l][:, None]
    cos_x, sin_x = _rope_tables_t(t)
    cos_c = jnp.ones((HEAD_DIM, tc), F32)
    sin_c = jnp.zeros((HEAD_DIM, tc), F32)

    q_x, k_x, q8_x, k8_x, v_x, ga_x, mix_x, gb_x = _inproj(x, mod_x, cos_x, sin_x, w_t, w_n, q_gain, k_gain)
    _, k_c, _, k8_c, v_c, _, mix_c, _ = _inproj(ctx, mod_c, cos_c, sin_c, w_t, w_n, q_gain, k_gain)

    v_all = jnp.concatenate([v_c, v_x], axis=2)
    fp8_ok = jnp.logical_and(
        jnp.max(jnp.abs(q_norm[l])) * (2.0 * HEAD_DIM ** 0.5 * HEAD_DIM ** -0.5 * math.log2(math.e) * FP8_Q_SCALE) < FP8_MAX,
        jnp.max(jnp.abs(k_norm[l])) * (2.0 * HEAD_DIM ** 0.5 / FP8_Q_SCALE) < FP8_MAX)
    y_att = lax.cond(
        fp8_ok,
        lambda: _attention(q8_x, jnp.concatenate([k8_c, k8_x], axis=2), v_all),
        lambda: _attention(q_x, jnp.concatenate([k_c, k_x], axis=2), v_all))

    prep_args = (mu_prev[l][None, :], mu_next[l][None, :], w0[l], w_up[l].astype(BF16), a0[l], a_up[l].astype(BF16),
                 k_k[l][None, :], k_a[l][None, :], r_k[l].reshape(2, RWKV_WIDTH), seg)
    _, qe_c, o0_c, g_c, h_c = _wkvprep(mix_c, *prep_args)
    zero_state = jnp.zeros((2, b, HEAD_PAIRS, PAIR_WIDTH, PAIR_WIDTH), F32)
    _, _, s_ctx = _wkvscan(qe_c, o0_c, g_c, h_c, zero_state)
    bonus, qe_x, o0_x, g_x, h_x = _wkvprep(mix_x, *prep_args)
    of_x, ob_x, _ = _wkvscan(qe_x, o0_x, g_x, h_x, s_ctx)

    return _merge(x, mod_x, y_att, ga_x, of_x, ob_x, bonus, gb_x,
                  w_pa[l].astype(BF16), w_pb[l].astype(BF16), w_mg[l].astype(BF16), b_mg[l][None, :],
                  w_o[l].astype(BF16), gn_w[l][None, :], gn_b[l][None, :], ln_g[l][None, :], ln_b[l][None, :], seg)
```

```python
import functools
import math

import jax
import jax.numpy as jnp
from jax import lax
from jax.experimental import pallas as pl
from jax.experimental.pallas import tpu as pltpu

F32 = jnp.float32
BF16 = jnp.bfloat16
FP8 = jnp.float8_e4m3fn
HIGHEST = lax.Precision.HIGHEST

D_MODEL = 1024
GRID_W = 64
HEAD_DIM = 64
N_Q_HEADS = 8
N_KV_HEADS = 2
GQA_GROUP = N_Q_HEADS // N_KV_HEADS
ATT_WIDTH = N_Q_HEADS * HEAD_DIM
KV_WIDTH = N_KV_HEADS * HEAD_DIM
QKV_WIDTH = ATT_WIDTH + 2 * KV_WIDTH
ROPE_FREQS = HEAD_DIM // 4
ROPE_BASE = 10000.0
RWKV_HEAD = 64
RWKV_HEADS = 8
RWKV_WIDTH = RWKV_HEADS * RWKV_HEAD
W_LORA = 64
A_LORA = 64
SHIFT_WIDTH = 3 * RWKV_WIDTH + 2 * W_LORA + 2 * A_LORA
IN_WIDTH = 2 * ATT_WIDTH + 2 * KV_WIDTH + SHIFT_WIDTH + RWKV_WIDTH
NAT_WIDTH = IN_WIDTH - QKV_WIDTH
LN_EPS = 1e-5
QK_EPS = 1e-6
GN_EPS = 64e-5
DEPTH = 1
ALPHA = (2.0 * DEPTH) ** 0.25

CHUNK = 64
KEY_BLOCK = 256
V_ROWS = HEAD_DIM + 8
FP8_DEPTH = 3 * HEAD_DIM
FP8_LO_SCALE = 16.0
FP8_Q_SCALE = 2.0
FP8_MAX = 448.0
TOKEN_TILE = 256
INPROJ_TILE = 512
MERGE_TILE = 512
Q_TILE = 512
ATTN_COLS = 256
ATTN_UNIT_BLOCKS = 4
ATTN_UNROLL = 4
WKV_GROUP = 2
INV_BLOCK = 16
SEG_SUM_PARTS = 1
HEAD_PAIRS = RWKV_HEADS // 2
PAIR_WIDTH = 2 * RWKV_HEAD
PAIR_SLAB = (HEAD_PAIRS, RWKV_HEAD, PAIR_WIDTH)
VMEM_LIMIT = 48 * 1024 * 1024
NEG_BIG = -1e30


def _cparams(sem):
    return pltpu.CompilerParams(dimension_semantics=sem, vmem_limit_bytes=VMEM_LIMIT)


def _silu(x):
    return x * jax.nn.sigmoid(x)


def _layer_norm(x):
    mu = jnp.mean(x, axis=-1, keepdims=True)
    xc = x - mu
    var = jnp.mean(xc * xc, axis=-1, keepdims=True)
    return xc * lax.rsqrt(var + LN_EPS)


def _split_bf16(x, parts):
    out = []
    rem = x
    for _ in range(parts):
        p = rem.astype(BF16)
        out.append(p)
        rem = rem - p.astype(F32)
    return out


def _dot_split_lhs(x, m_bf16, parts):
    acc = None
    for p in _split_bf16(x, parts):
        t = jnp.dot(p, m_bf16, preferred_element_type=F32)
        acc = t if acc is None else acc + t
    return acc


def _dot_split_rhs(m_bf16, x, parts):
    acc = None
    for p in _split_bf16(x, parts):
        t = jnp.dot(m_bf16, p, preferred_element_type=F32)
        acc = t if acc is None else acc + t
    return acc


def _ada_kernel(c_ref, w_ref, b_ref, o_ref):
    s = _silu(c_ref[...])
    o_ref[...] = jnp.dot(s, w_ref[...], precision=HIGHEST, preferred_element_type=F32) + b_ref[...]


def _ada(cc, w_ada, b_ada):
    rows = cc.shape[0]
    nblk = (3 * D_MODEL) // D_MODEL
    return pl.pallas_call(
        _ada_kernel,
        grid=(nblk,),
        in_specs=[
            pl.BlockSpec((rows, D_MODEL), lambda j: (0, 0)),
            pl.BlockSpec((D_MODEL, D_MODEL), lambda j: (0, j)),
            pl.BlockSpec((1, D_MODEL), lambda j: (0, j)),
        ],
        out_specs=pl.BlockSpec((rows, D_MODEL), lambda j: (0, j)),
        out_shape=jax.ShapeDtypeStruct((rows, 3 * D_MODEL), F32),
        compiler_params=_cparams(("arbitrary",)),
        name="ada",
    )(cc, w_ada, b_ada)


def _rope_t(x, cos, sin):
    xs = jnp.concatenate([x[:, 16:32], x[:, 0:16], x[:, 48:64], x[:, 32:48]], axis=1)
    return x * cos + xs * sin


def _fp8_pieces(x):
    hi = x.astype(FP8).astype(F32)
    lo16 = ((x - hi) * FP8_LO_SCALE).astype(FP8).astype(F32)
    hi16 = (hi * (1.0 / FP8_LO_SCALE)).astype(FP8).astype(F32)
    return hi, hi16, lo16


def _inproj_kernel(x_ref, mod_ref, cos_ref, sin_ref, wt_ref, wn_ref, qg_ref, kg_ref,
                   q_ref, k_ref, q8_ref, k8_ref, v_ref, ga_ref, mix_ref, gb_ref):
    tm = x_ref.shape[1]
    x = x_ref[0]
    shift = mod_ref[0, 0:1, :]
    scale = mod_ref[0, 1:2, :]
    h = _layer_norm(x) * (1.0 + scale) + shift
    hb = h.astype(BF16)
    zt = lax.dot_general(wt_ref[...], hb, (((1,), (1,)), ((), ())), preferred_element_type=F32)
    zn = jnp.dot(hb, wn_ref[...], preferred_element_type=F32)
    cos = cos_ref[...][None]
    sin = sin_ref[...][None]

    qz = zt[0:ATT_WIDTH].reshape(N_Q_HEADS, HEAD_DIM, tm)
    qn = qz * lax.rsqrt(jnp.mean(qz * qz, axis=1, keepdims=True) + QK_EPS) * qg_ref[...][None]
    qr = _rope_t(qn, cos, sin) * (HEAD_DIM ** -0.5 * math.log2(math.e))
    q_ref[0] = qr.astype(BF16)
    q_hi, q_hi16, q_lo16 = _fp8_pieces(qr * FP8_Q_SCALE)
    q8_ref[0] = jnp.concatenate([q_hi, q_hi16, q_lo16], axis=1).astype(FP8)

    kz = zt[ATT_WIDTH:ATT_WIDTH + KV_WIDTH].reshape(N_KV_HEADS, HEAD_DIM, tm)
    kn = kz * lax.rsqrt(jnp.mean(kz * kz, axis=1, keepdims=True) + QK_EPS) * kg_ref[...][None]
    kr = _rope_t(kn, cos, sin)
    k_hi, k_hi16, k_lo16 = _fp8_pieces(kr * (1.0 / FP8_Q_SCALE))
    k8 = jnp.concatenate([k_hi, k_lo16, k_hi16], axis=1)
    vz = zt[ATT_WIDTH + KV_WIDTH:QKV_WIDTH].reshape(N_KV_HEADS, HEAD_DIM, tm)
    pad_row = lax.broadcasted_iota(jnp.int32, (V_ROWS - HEAD_DIM, KEY_BLOCK), 0)
    ones_pad = jnp.where(pad_row == 0, 1.0, 0.0).astype(BF16)
    for j in range(N_KV_HEADS):
        for kb in range(tm // KEY_BLOCK):
            cols = slice(kb * KEY_BLOCK, (kb + 1) * KEY_BLOCK)
            k_ref[0, j, kb] = kr[j][:, cols].T.astype(BF16)
            k8_ref[0, j, kb] = k8[j][:, cols].T.astype(FP8)
            v_ref[0, j, kb, 0:HEAD_DIM, :] = vz[j][:, cols].astype(BF16)
            v_ref[0, j, kb, HEAD_DIM:V_ROWS, :] = ones_pad

    ga_ref[0] = _silu(zn[:, 0:ATT_WIDTH]).astype(BF16)
    mix_ref[0] = zn[:, ATT_WIDTH:ATT_WIDTH + SHIFT_WIDTH]
    gb_ref[0] = _silu(zn[:, ATT_WIDTH + SHIFT_WIDTH:NAT_WIDTH]).astype(BF16)


def _inproj(x, mod, cos_t, sin_t, w_t, w_n, q_gain, k_gain):
    b, t, _ = x.shape
    tm = min(INPROJ_TILE, t)
    assert tm % KEY_BLOCK == 0 and t % tm == 0
    nt = t // tm
    kbt = tm // KEY_BLOCK
    nkb = t // KEY_BLOCK
    const = lambda shape: pl.BlockSpec(shape, lambda bi, i: (0,) * len(shape))
    return pl.pallas_call(
        _inproj_kernel,
        grid=(b, nt),
        in_specs=[
            pl.BlockSpec((1, tm, D_MODEL), lambda bi, i: (bi, i, 0)),
            pl.BlockSpec((1, 3, D_MODEL), lambda bi, i: (bi, 0, 0)),
            pl.BlockSpec((HEAD_DIM, tm), lambda bi, i: (0, i)),
            pl.BlockSpec((HEAD_DIM, tm), lambda bi, i: (0, i)),
            const((QKV_WIDTH, D_MODEL)),
            const((D_MODEL, NAT_WIDTH)),
            const((HEAD_DIM, 1)),
            const((HEAD_DIM, 1)),
        ],
        out_specs=[
            pl.BlockSpec((1, N_Q_HEADS, HEAD_DIM, tm), lambda bi, i: (bi, 0, 0, i)),
            pl.BlockSpec((1, N_KV_HEADS, kbt, KEY_BLOCK, HEAD_DIM), lambda bi, i: (bi, 0, i, 0, 0)),
            pl.BlockSpec((1, N_Q_HEADS, FP8_DEPTH, tm), lambda bi, i: (bi, 0, 0, i)),
            pl.BlockSpec((1, N_KV_HEADS, kbt, KEY_BLOCK, FP8_DEPTH), lambda bi, i: (bi, 0, i, 0, 0)),
            pl.BlockSpec((1, N_KV_HEADS, kbt, V_ROWS, KEY_BLOCK), lambda bi, i: (bi, 0, i, 0, 0)),
            pl.BlockSpec((1, tm, ATT_WIDTH), lambda bi, i: (bi, i, 0)),
            pl.BlockSpec((1, tm, SHIFT_WIDTH), lambda bi, i: (bi, i, 0)),
            pl.BlockSpec((1, tm, RWKV_WIDTH), lambda bi, i: (bi, i, 0)),
        ],
        out_shape=[
            jax.ShapeDtypeStruct((b, N_Q_HEADS, HEAD_DIM, t), BF16),
            jax.ShapeDtypeStruct((b, N_KV_HEADS, nkb, KEY_BLOCK, HEAD_DIM), BF16),
            jax.ShapeDtypeStruct((b, N_Q_HEADS, FP8_DEPTH, t), FP8),
            jax.ShapeDtypeStruct((b, N_KV_HEADS, nkb, KEY_BLOCK, FP8_DEPTH), FP8),
            jax.ShapeDtypeStruct((b, N_KV_HEADS, nkb, V_ROWS, KEY_BLOCK), BF16),
            jax.ShapeDtypeStruct((b, t, ATT_WIDTH), BF16),
            jax.ShapeDtypeStruct((b, t, SHIFT_WIDTH), F32),
            jax.ShapeDtypeStruct((b, t, RWKV_WIDTH), BF16),
        ],
        compiler_params=_cparams(("parallel", "parallel")),
        name="inproj",
    )(x, mod, cos_t, sin_t, w_t, w_n, q_gain, k_gain)


def _attn_kernel(q_ref, k_ref, v_ref, o_ref, s_sc, cmax_sc, m_sc, acc_sc):
    nkb = k_ref.shape[2]
    tq = q_ref.shape[3]
    width = GQA_GROUP * tq
    nchunk = width // ATTN_COLS
    per_head = tq // ATTN_COLS
    kb = KEY_BLOCK
    kbt = ATTN_UNIT_BLOCKS
    m_sc[...] = jnp.full(m_sc.shape, NEG_BIG, F32)
    acc_sc[...] = jnp.zeros(acc_sc.shape, F32)
    first = nkb % kbt if nkb % kbt else kbt
    nunits = 1 + (nkb - first) // kbt

    def unit_start(u):
        return first + (u - 1) * kbt

    def tick(u, par, nb_acc, nb_score):
        if nb_acc:
            acc0 = 0 if (isinstance(u, int) and u == 1) else unit_start(u - 1)
            m_old = m_sc[...]
            m_new = jnp.maximum(m_old, cmax_sc[1 - par])
            alpha = jnp.exp2(m_old - m_new)
            m_sc[...] = m_new
        if nb_score:
            sc0 = 0 if (isinstance(u, int) and u == 0) else unit_start(u)
        for ch in range(nchunk):
            cols = slice(ch * ATTN_COLS, (ch + 1) * ATTN_COLS)
            if nb_acc:
                pv = None
                for i in range(nb_acc):
                    p = jnp.exp2(s_sc[1 - par, i * kb:(i + 1) * kb, cols] - m_new[:, cols]).astype(BF16)
                    t = jnp.dot(v_ref[0, 0, acc0 + i], p, preferred_element_type=F32)
                    pv = t if pv is None else pv + t
                acc_sc[:, cols] = acc_sc[:, cols] * alpha[:, cols] + pv
            if nb_score:
                g, h = divmod(ch, per_head)
                qc = q_ref[0, g, :, h * ATTN_COLS:(h + 1) * ATTN_COLS]
                cm = None
                for i in range(nb_score):
                    s = jnp.dot(k_ref[0, 0, sc0 + i], qc, preferred_element_type=F32)
                    s_sc[par, i * kb:(i + 1) * kb, cols] = s
                    c1 = jnp.max(s, axis=0, keepdims=True)
                    cm = c1 if cm is None else jnp.maximum(cm, c1)
                cmax_sc[par, :, cols] = cm

    tick(0, 0, 0, first)
    if nunits > 1:
        tick(1, 1, first, kbt)
    nsteady = max(nunits - 2, 0)
    npeel = nsteady % ATTN_UNROLL
    for u in range(2, 2 + npeel):
        tick(u, u % 2, kbt, kbt)
    base = 2 + npeel

    def group(j, carry):
        for i in range(ATTN_UNROLL):
            tick(base + ATTN_UNROLL * j + i, (base + i) % 2, kbt, kbt)
        return carry

    lax.fori_loop(0, (nsteady - npeel) // ATTN_UNROLL, group, 0)
    tick(nunits, nunits % 2, kbt if nunits > 1 else first, 0)
    for g in range(GQA_GROUP):
        acc = acc_sc[:, g * tq:(g + 1) * tq]
        o = acc[0:HEAD_DIM] / acc[HEAD_DIM:HEAD_DIM + 1]
        o_ref[0, :, g * HEAD_DIM:(g + 1) * HEAD_DIM] = o.T


def _attention(q_t, k_blk, v_blk):
    b, _, depth, t = q_t.shape
    nkb = k_blk.shape[2]
    tq = min(Q_TILE, t)
    assert t % tq == 0 and k_blk.shape[4] == depth
    width = GQA_GROUP * tq
    return pl.pallas_call(
        _attn_kernel,
        grid=(b, N_KV_HEADS, t // tq),
        in_specs=[
            pl.BlockSpec((1, GQA_GROUP, depth, tq), lambda bi, j, i: (bi, j, 0, i)),
            pl.BlockSpec((1, 1, nkb, KEY_BLOCK, depth), lambda bi, j, i: (bi, j, 0, 0, 0)),
            pl.BlockSpec((1, 1, nkb, V_ROWS, KEY_BLOCK), lambda bi, j, i: (bi, j, 0, 0, 0)),
        ],
        out_specs=pl.BlockSpec((1, tq, GQA_GROUP * HEAD_DIM), lambda bi, j, i: (bi, i, j)),
        out_shape=jax.ShapeDtypeStruct((b, t, ATT_WIDTH), F32),
        scratch_shapes=[
            pltpu.VMEM((2, ATTN_UNIT_BLOCKS * KEY_BLOCK, width), F32),
            pltpu.VMEM((2, 1, width), F32),
            pltpu.VMEM((1, width), F32),
            pltpu.VMEM((V_ROWS, width), F32),
        ],
        compiler_params=_cparams(("parallel", "parallel", "parallel")),
        name="attn",
    )(q_t, k_blk, v_blk)


def _bmm(a, b):
    return jnp.einsum('hij,hjk->hik', a.astype(BF16), b.astype(BF16), preferred_element_type=F32)


def _bmm_nt(a, b):
    return jnp.einsum('hik,hjk->hij', a.astype(BF16), b.astype(BF16), preferred_element_type=F32)


def _wkvprep_kernel(mix_ref, prev_ref, next_ref, mup_ref, mun_ref, w0_ref, wup_ref, a0_ref, aup_ref,
                    kk_ref, ka_ref, rk_ref, seg_ref,
                    bonus_ref, qeff_ref, o0_ref, g_ref, h_ref,
                    r_sc, v_sc, kn_sc, ld_sc, kd_sc, b_sc):
    tm = mix_ref.shape[1]
    i = pl.program_id(1)
    n = pl.num_programs(1)
    z = mix_ref[0]
    prev_row = jnp.where(i > 0, prev_ref[0, 7:8, :], 0.0)
    next_row = jnp.where(i < n - 1, next_ref[0, 0:1, :], 0.0)
    rows = lax.broadcasted_iota(jnp.int32, (tm, 1), 0)
    zp = jnp.where(rows == 0, prev_row, pltpu.roll(z, 1, 0))
    zn = jnp.where(rows == tm - 1, next_row, pltpu.roll(z, tm - 1, 0))
    zs = z + mup_ref[...] * (zp - z) + mun_ref[...] * (zn - z)

    w = RWKV_WIDTH
    r = zs[:, 0:w]
    k = zs[:, w:2 * w]
    v = zs[:, 2 * w:3 * w]
    seg = seg_ref[...]
    kkr = k * kk_ref[...]
    norm = jnp.sqrt(_dot_split_lhs(kkr * kkr, seg, SEG_SUM_PARTS))
    kn = kkr / jnp.maximum(norm, 1e-12)
    r_sc[...] = r
    v_sc[...] = v
    kn_sc[...] = kn
    coef = jnp.zeros((tm, w), F32)
    for d in range(2):
        wl = zs[:, 3 * w + d * W_LORA:3 * w + (d + 1) * W_LORA]
        al = zs[:, 3 * w + 2 * W_LORA + d * A_LORA:3 * w + 2 * W_LORA + (d + 1) * A_LORA]
        w_pre = w0_ref[d:d + 1, :] + jnp.dot(jnp.tanh(wl).astype(BF16), wup_ref[d], preferred_element_type=F32)
        ld_sc[d] = -math.exp(-0.5) * jax.nn.sigmoid(w_pre)
        a = jax.nn.sigmoid(a0_ref[d:d + 1, :] + jnp.dot(al.astype(BF16), aup_ref[d], preferred_element_type=F32))
        kd = k * (1.0 + (a - 1.0) * ka_ref[...])
        kd_sc[d] = kd
        b_sc[d] = kn * a
        coef = coef + _dot_split_lhs(r * kd * rk_ref[d:d + 1, :], seg, SEG_SUM_PARTS)
    bonus_ref[0] = coef * v

    ri = lax.broadcasted_iota(jnp.int32, (CHUNK, CHUNK), 0)
    ci = lax.broadcasted_iota(jnp.int32, (CHUNK, CHUNK), 1)
    eye_f = jnp.where(ri == ci, 1.0, 0.0)
    strict_f = [jnp.where(ci < ri, 1.0, 0.0), jnp.where(ci > ri, 1.0, 0.0)]
    incl_f = [s + eye_f for s in strict_f]
    npair = RWKV_HEADS // 2
    pw = 2 * RWKV_HEAD
    lane = lax.broadcasted_iota(jnp.int32, (1, pw), 1)
    even_lanes = lane < RWKV_HEAD
    m_even = jnp.where(even_lanes, 1.0, 0.0)
    m_odd = 1.0 - m_even
    even_lanes2 = jnp.concatenate([even_lanes, even_lanes], axis=1)
    r128 = lax.broadcasted_iota(jnp.int32, (pw, pw), 0)
    c128 = lax.broadcasted_iota(jnp.int32, (pw, pw), 1)
    eye_pw = jnp.where(r128 == c128, 1.0, 0.0)

    def _heads(x):
        return jnp.stack([x[:, p * pw:(p + 1) * pw] for p in range(npair)], axis=0)

    def group(gi, carry):
        combos = [(gi * WKV_GROUP + u, d) for u in range(WKV_GROUP) for d in range(2)]
        slices = []
        parts = {name: [] for name in ("rh", "kkh", "kt", "bt", "kv", "bv", "gam", "vh")}
        for c, d in combos:
            sl = pl.ds(pl.multiple_of(c * CHUNK, CHUNK), CHUNK)
            slices.append(sl)
            ld = ld_sc[d, sl, :]
            cum = _dot_split_rhs(incl_f[d].astype(BF16), ld, 3)
            tot = cum[CHUNK - 1:CHUNK, :] if d == 0 else cum[0:1, :]
            e_pos = jnp.exp(cum)
            e_prev = jnp.exp(cum - ld)
            e_neg = jnp.exp(-cum)
            e_rest = jnp.exp(tot - cum)
            kd_c = kd_sc[d, sl, :]
            b_c = b_sc[d, sl, :]
            parts["rh"].append(_heads(r_sc[sl, :] * e_pos))
            parts["kkh"].append(_heads(kn_sc[sl, :] * e_prev))
            parts["kt"].append(_heads(kd_c * e_neg))
            parts["bt"].append(_heads(b_c * e_neg))
            parts["kv"].append(_heads(kd_c * e_rest))
            parts["bv"].append(_heads(b_c * e_rest))
            parts["gam"].append(_heads(jnp.exp(tot)))
            parts["vh"].append(_heads(v_sc[sl, :]))
        cat = lambda xs: jnp.concatenate(xs, axis=0)
        rh, kkh, kt, bt, kv, bv, gam, vh = (cat(parts[n]) for n in ("rh", "kkh", "kt", "bt", "kv", "bv", "gam", "vh"))
        nbp = len(combos) * npair
        per = lambda ms: cat([jnp.broadcast_to(ms[d][None], (npair, CHUNK, CHUNK)) for _, d in combos])
        both = lambda x: jnp.concatenate([x, x], axis=0)
        masked = lambda x: jnp.concatenate([x * m_even, x * m_odd], axis=0)
        pick = lambda z, ev: jnp.where(ev, z[0:nbp], z[nbp:2 * nbp])
        strict = both(per(strict_f))
        incl = both(per(incl_f))

        kk_m = masked(kkh)
        r_m = masked(rh)
        kt2 = both(kt)
        bt2 = both(bt)
        v2 = both(vh)
        lk = _bmm_nt(kk_m, kt2) * strict
        lb = _bmm_nt(kk_m, bt2) * strict
        ark = _bmm_nt(r_m, kt2) * incl
        arb = _bmm_nt(r_m, bt2) * incl
        blk = lambda n: (ri // n) == (ci // n)
        xp = -(lb * jnp.where(blk(INV_BLOCK), 1.0, 0.0)[None])
        tinv = eye_f[None] + xp
        span = 2
        while span < INV_BLOCK:
            xp = _bmm(xp, xp)
            tinv = tinv + _bmm(tinv, xp)
            span *= 2
        size = INV_BLOCK
        while size < CHUNK:
            join = lb * jnp.where(blk(2 * size) & ~blk(size), 1.0, 0.0)[None]
            tinv = tinv - _bmm(tinv, _bmm(join, tinv))
            size *= 2
        wu_h = _bmm(tinv, jnp.concatenate([both(kkh), _bmm(lk, v2)], axis=2))
        arb_wu = pick(_bmm(arb, wu_h), even_lanes2)
        wu = pick(wu_h, even_lanes2)
        qeff = rh - arb_wu[:, :, 0:pw]
        o0 = pick(_bmm(ark, v2), even_lanes) - arb_wu[:, :, pw:2 * pw]
        bvt_wu = _bmm(jnp.swapaxes(bv, 1, 2), wu)
        kvt_v = _bmm(jnp.swapaxes(kv, 1, 2), vh)
        gfull = eye_pw[None] * gam - bvt_wu[:, :, 0:pw]
        hfull = kvt_v - bvt_wu[:, :, pw:2 * pw]
        slab = lambda z: jnp.where(even_lanes, z[:, 0:RWKV_HEAD, :], z[:, RWKV_HEAD:pw, :])
        gmat = slab(gfull)
        hmat = slab(hfull)
        for idx, (c, d) in enumerate(combos):
            for p in range(npair):
                qeff_ref[d, 0, slices[idx], p * pw:(p + 1) * pw] = qeff[idx * npair + p].astype(BF16)
                o0_ref[d, 0, slices[idx], p * pw:(p + 1) * pw] = o0[idx * npair + p]
            g_ref[d, 0, c] = gmat[idx * npair:(idx + 1) * npair].astype(BF16)
            h_ref[d, 0, c] = hmat[idx * npair:(idx + 1) * npair]
        return carry

    lax.fori_loop(0, tm // (CHUNK * WKV_GROUP), group, 0)


def _wkvprep(mix, mu_prev, mu_next, w0, w_up, a0, a_up, k_k, k_a, r_k, seg):
    b, t, _ = mix.shape
    tm = min(TOKEN_TILE, t)
    assert t % tm == 0 and tm % CHUNK == 0
    nt = t // tm
    nc = tm // CHUNK
    hb = tm // 8
    last8 = t // 8 - 1
    const = lambda shape: pl.BlockSpec(shape, lambda bi, i: (0,) * len(shape))
    hd = (2, 1, tm, RWKV_WIDTH)
    st = (2, 1, nc) + PAIR_SLAB
    return pl.pallas_call(
        _wkvprep_kernel,
        grid=(b, nt),
        in_specs=[
            pl.BlockSpec((1, tm, SHIFT_WIDTH), lambda bi, i: (bi, i, 0)),
            pl.BlockSpec((1, 8, SHIFT_WIDTH), lambda bi, i: (bi, jnp.maximum(i * hb - 1, 0), 0)),
            pl.BlockSpec((1, 8, SHIFT_WIDTH), lambda bi, i: (bi, jnp.minimum((i + 1) * hb, last8), 0)),
            const((1, SHIFT_WIDTH)),
            const((1, SHIFT_WIDTH)),
            const((2, RWKV_WIDTH)),
            const((2, W_LORA, RWKV_WIDTH)),
            const((2, RWKV_WIDTH)),
            const((2, A_LORA, RWKV_WIDTH)),
            const((1, RWKV_WIDTH)),
            const((1, RWKV_WIDTH)),
            const((2, RWKV_WIDTH)),
            const((RWKV_WIDTH, RWKV_WIDTH)),
        ],
        out_specs=[
            pl.BlockSpec((1, tm, RWKV_WIDTH), lambda bi, i: (bi, i, 0)),
            pl.BlockSpec(hd, lambda bi, i: (0, bi, i, 0)),
            pl.BlockSpec(hd, lambda bi, i: (0, bi, i, 0)),
            pl.BlockSpec(st, lambda bi, i: (0, bi, i, 0, 0, 0)),
            pl.BlockSpec(st, lambda bi, i: (0, bi, i, 0, 0, 0)),
        ],
        out_shape=[
            jax.ShapeDtypeStruct((b, t, RWKV_WIDTH), F32),
            jax.ShapeDtypeStruct((2, b, t, RWKV_WIDTH), BF16),
            jax.ShapeDtypeStruct((2, b, t, RWKV_WIDTH), F32),
            jax.ShapeDtypeStruct((2, b, t // CHUNK) + PAIR_SLAB, BF16),
            jax.ShapeDtypeStruct((2, b, t // CHUNK) + PAIR_SLAB, F32),
        ],
        scratch_shapes=[
            pltpu.VMEM((tm, RWKV_WIDTH), F32),
            pltpu.VMEM((tm, RWKV_WIDTH), F32),
            pltpu.VMEM((tm, RWKV_WIDTH), F32),
            pltpu.VMEM((2, tm, RWKV_WIDTH), F32),
            pltpu.VMEM((2, tm, RWKV_WIDTH), F32),
            pltpu.VMEM((2, tm, RWKV_WIDTH), F32),
        ],
        compiler_params=_cparams(("parallel", "parallel")),
        name="wkvprep",
    )(mix, mix, mix, mu_prev, mu_next, w0, w_up, a0, a_up, k_k, k_a, r_k, seg)


def _wkvscan_kernel(qf_ref, qb_ref, of_ref, ob_ref, gf_ref, gb_ref, hf_ref, hb_ref, s0_ref,
                    outf_ref, outb_ref, sfin_ref, s_sc):
    c = pl.program_id(0)
    nb = qf_ref.shape[1]
    nc = gf_ref.shape[2]
    npair = HEAD_PAIRS
    pw = PAIR_WIDTH
    even_lanes = lax.broadcasted_iota(jnp.int32, (1, pw), 1) < RWKV_HEAD

    @pl.when(c == 0)
    def _():
        s_sc[...] = s0_ref[...].reshape(s_sc.shape)

    def bdot(a, s_parts):
        ab = a.astype(BF16)
        return sum(jnp.einsum('hij,hjk->hik', ab, p, preferred_element_type=F32) for p in s_parts)

    def block_diag(slab):
        zero = jnp.zeros_like(slab)
        return jnp.concatenate([jnp.where(even_lanes, slab, zero), jnp.where(even_lanes, zero, slab)], axis=1)

    def pairs(ref, rows):
        return [ref[0, bi, rows, p * pw:(p + 1) * pw] for bi in range(nb) for p in range(npair)]

    for j in range(nc):
        jb = nc - 1 - j
        rf = slice(j * CHUNK, (j + 1) * CHUNK)
        rb = slice(jb * CHUNK, (jb + 1) * CHUNK)
        s = _split_bf16(s_sc[...], 2)
        qe = jnp.stack(pairs(qf_ref, rf) + pairs(qb_ref, rb), axis=0)
        o0 = jnp.stack(pairs(of_ref, rf) + pairs(ob_ref, rb), axis=0)
        gm = jnp.concatenate([gf_ref[0, bi, j] for bi in range(nb)] + [gb_ref[0, bi, jb] for bi in range(nb)], axis=0)
        hm = jnp.concatenate([hf_ref[0, bi, j] for bi in range(nb)] + [hb_ref[0, bi, jb] for bi in range(nb)], axis=0)
        o = bdot(qe, s) + o0
        s_sc[...] = bdot(block_diag(gm), s) + block_diag(hm)
        for bi in range(nb):
            for p in range(npair):
                outf_ref[bi, rf, p * pw:(p + 1) * pw] = o[bi * npair + p]
                outb_ref[bi, rb, p * pw:(p + 1) * pw] = o[(nb + bi) * npair + p]

    @pl.when(c == pl.num_programs(0) - 1)
    def _():
        sfin_ref[...] = s_sc[...].reshape(sfin_ref.shape)


def _wkvscan(qeff, o0, g, h, s0):
    _, b, t, _ = qeff.shape
    tc = min(TOKEN_TILE, t)
    nc = tc // CHUNK
    nt = t // tc
    hd = (1, b, tc, RWKV_WIDTH)
    st = (1, b, nc) + PAIR_SLAB
    fwd_hd = pl.BlockSpec(hd, lambda c: (0, 0, c, 0))
    bwd_hd = pl.BlockSpec(hd, lambda c: (1, 0, nt - 1 - c, 0))
    fwd_st = pl.BlockSpec(st, lambda c: (0, 0, c, 0, 0, 0))
    bwd_st = pl.BlockSpec(st, lambda c: (1, 0, nt - 1 - c, 0, 0, 0))
    s_shape = (2, b, HEAD_PAIRS, PAIR_WIDTH, PAIR_WIDTH)
    s_spec = pl.BlockSpec(s_shape, lambda c: (0, 0, 0, 0, 0))
    return pl.pallas_call(
        _wkvscan_kernel,
        grid=(nt,),
        in_specs=[fwd_hd, bwd_hd, fwd_hd, bwd_hd, fwd_st, bwd_st, fwd_st, bwd_st, s_spec],
        out_specs=[
            pl.BlockSpec((b, tc, RWKV_WIDTH), lambda c: (0, c, 0)),
            pl.BlockSpec((b, tc, RWKV_WIDTH), lambda c: (0, nt - 1 - c, 0)),
            s_spec,
        ],
        out_shape=[
            jax.ShapeDtypeStruct((b, t, RWKV_WIDTH), F32),
            jax.ShapeDtypeStruct((b, t, RWKV_WIDTH), F32),
            jax.ShapeDtypeStruct(s_shape, F32),
        ],
        scratch_shapes=[pltpu.VMEM((2 * b * HEAD_PAIRS, PAIR_WIDTH, PAIR_WIDTH), F32)],
        compiler_params=_cparams(("arbitrary",)),
        name="wkvscan",
    )(qeff, qeff, o0, o0, g, g, h, h, s0)


def _merge_kernel(x_ref, mod_ref, ya_ref, ga_ref, of_ref, ob_ref, bonus_ref, gb_ref,
                  wpa_ref, wpb_ref, wmg_ref, bmg_ref, wo_ref, gnw_ref, gnb_ref, lng_ref, lnb_ref, seg_ref,
                  out_ref):
    x = x_ref[0]
    shift = mod_ref[0, 0:1, :]
    scale = mod_ref[0, 1:2, :]
    gate = mod_ref[0, 2:3, :]
    h = _layer_norm(x) * (1.0 + scale) + shift
    gates = jax.nn.sigmoid(jnp.dot(h.astype(BF16), wmg_ref[...], preferred_element_type=F32) + bmg_ref[...])
    ya = jnp.dot((ya_ref[0] * ga_ref[0].astype(F32)).astype(BF16), wpa_ref[...], preferred_element_type=F32)

    o = of_ref[0] + ob_ref[0]
    seg = seg_ref[...]
    mu = _dot_split_lhs(o, seg, SEG_SUM_PARTS) * (1.0 / RWKV_HEAD)
    oc = o - mu
    var = _dot_split_lhs(oc * oc, seg, SEG_SUM_PARTS) * (1.0 / RWKV_HEAD)
    y = oc * lax.rsqrt(var + GN_EPS) * gnw_ref[...] + gnb_ref[...] + bonus_ref[0]
    yb = jnp.dot((y * gb_ref[0].astype(F32)).astype(BF16), wpb_ref[...], preferred_element_type=F32)

    mixed = gates[:, 0:D_MODEL] * ya + gates[:, D_MODEL:2 * D_MODEL] * yb
    out = jnp.dot(mixed.astype(BF16), wo_ref[...], preferred_element_type=F32)
    out_ref[0] = _layer_norm(ALPHA * x + gate * out) * lng_ref[...] + lnb_ref[...]


def _merge(x, mod, y_att, ga, o_f, o_b, bonus, gb, w_pa, w_pb, w_mg, b_mg, w_o, gn_w, gn_b, ln_g, ln_b, seg):
    b, t, _ = x.shape
    tm = min(MERGE_TILE, t)
    assert t % tm == 0
    const = lambda shape: pl.BlockSpec(shape, lambda bi, i: (0,) * len(shape))
    tok = lambda width: pl.BlockSpec((1, tm, width), lambda bi, i: (bi, i, 0))
    return pl.pallas_call(
        _merge_kernel,
        grid=(b, t // tm),
        in_specs=[
            tok(D_MODEL),
            pl.BlockSpec((1, 3, D_MODEL), lambda bi, i: (bi, 0, 0)),
            tok(ATT_WIDTH),
            tok(ATT_WIDTH),
            tok(RWKV_WIDTH),
            tok(RWKV_WIDTH),
            tok(RWKV_WIDTH),
            tok(RWKV_WIDTH),
            const((ATT_WIDTH, D_MODEL)),
            const((RWKV_WIDTH, D_MODEL)),
            const((D_MODEL, 2 * D_MODEL)),
            const((1, 2 * D_MODEL)),
            const((D_MODEL, D_MODEL)),
            const((1, RWKV_WIDTH)),
            const((1, RWKV_WIDTH)),
            const((1, D_MODEL)),
            const((1, D_MODEL)),
            const((RWKV_WIDTH, RWKV_WIDTH)),
        ],
        out_specs=tok(D_MODEL),
        out_shape=jax.ShapeDtypeStruct((b, t, D_MODEL), F32),
        compiler_params=_cparams(("parallel", "parallel")),
        name="merge",
    )(x, mod, y_att, ga, o_f, o_b, bonus, gb, w_pa, w_pb, w_mg, b_mg, w_o, gn_w, gn_b, ln_g, ln_b, seg)


def _rope_tables_t(t):
    pos = jnp.arange(t, dtype=jnp.int32)
    row = (pos // GRID_W).astype(F32)
    col = (pos % GRID_W).astype(F32)
    inv = ROPE_BASE ** (-jnp.arange(ROPE_FREQS, dtype=F32) / ROPE_FREQS)
    ang_r = inv[:, None] * row[None, :]
    ang_c = inv[:, None] * col[None, :]
    cos_t = jnp.concatenate([jnp.cos(ang_r), jnp.cos(ang_r), jnp.cos(ang_c), jnp.cos(ang_c)], axis=0)
    sin_t = jnp.concatenate([-jnp.sin(ang_r), jnp.sin(ang_r), -jnp.sin(ang_c), jnp.sin(ang_c)], axis=0)
    return cos_t, sin_t


def kernel(x, c, ctx, c_ctx, w_ada, b_ada, w_in, q_norm, k_norm, mu_prev, mu_next, w0, w_up, a0, a_up,
           k_k, k_a, r_k, gn_w, gn_b, w_pa, w_pb, w_mg, b_mg, w_o, ln_g, ln_b):
    b, t, _ = x.shape
    tc = ctx.shape[1]
    l = 0
    seg = jnp.kron(jnp.eye(RWKV_HEADS, dtype=F32), jnp.ones((RWKV_HEAD, RWKV_HEAD), F32)).astype(BF16)

    rows = 8 * ((b + 1 + 7) // 8)
    cc = jnp.zeros((rows, D_MODEL), F32).at[0:b].set(c).at[b].set(c_ctx)
    ada = _ada(cc, w_ada[l], b_ada[l][None, :])
    mod_x = ada[0:b].reshape(b, 3, D_MODEL)
    mod_c = jnp.broadcast_to(ada[b].reshape(1, 3, D_MODEL), (b, 3, D_MODEL))

    w_t = w_in[l][:, 0:QKV_WIDTH].T.astype(BF16)
    w_n = w_in[l][:, QKV_WIDTH:].astype(BF16)
    q_gain = q_norm[l][:, None]
    k_gain = k_norm[l][:, None]
    cos_x, sin_x = _rope_tables_t(t)
    cos_c = jnp.ones((HEAD_DIM, tc), F32)
    sin_c = jnp.zeros((HEAD_DIM, tc), F32)

    q_x, k_x, q8_x, k8_x, v_x, ga_x, mix_x, gb_x = _inproj(x, mod_x, cos_x, sin_x, w_t, w_n, q_gain, k_gain)
    _, k_c, _, k8_c, v_c, _, mix_c, _ = _inproj(ctx, mod_c, cos_c, sin_c, w_t, w_n, q_gain, k_gain)

    v_all = jnp.concatenate([v_c, v_x], axis=2)
    fp8_ok = jnp.logical_and(
        jnp.max(jnp.abs(q_norm[l])) * (2.0 * HEAD_DIM ** 0.5 * HEAD_DIM ** -0.5 * math.log2(math.e) * FP8_Q_SCALE) < FP8_MAX,
        jnp.max(jnp.abs(k_norm[l])) * (2.0 * HEAD_DIM ** 0.5 / FP8_Q_SCALE) < FP8_MAX)
    y_att = lax.cond(
        fp8_ok,
        lambda: _attention(q8_x, jnp.concatenate([k8_c, k8_x], axis=2), v_all),
        lambda: _attention(q_x, jnp.concatenate([k_c, k_x], axis=2), v_all))

    prep_args = (mu_prev[l][None, :], mu_next[l][None, :], w0[l], w_up[l].astype(BF16), a0[l], a_up[l].astype(BF16),
                 k_k[l][None, :], k_a[l][None, :], r_k[l].reshape(2, RWKV_WIDTH), seg)
    _, qe_c, o0_c, g_c, h_c = _wkvprep(mix_c, *prep_args)
    zero_state = jnp.zeros((2, b, HEAD_PAIRS, PAIR_WIDTH, PAIR_WIDTH), F32)
    _, _, s_ctx = _wkvscan(qe_c, o0_c, g_c, h_c, zero_state)
    bonus, qe_x, o0_x, g_x, h_x = _wkvprep(mix_x, *prep_args)
    of_x, ob_x, _ = _wkvscan(qe_x, o0_x, g_x, h_x, s_ctx)

    return _merge(x, mod_x, y_att, ga_x, of_x, ob_x, bonus, gb_x,
                  w_pa[l].astype(BF16), w_pb[l].astype(BF16), w_mg[l].astype(BF16), b_mg[l][None, :],
                  w_o[l].astype(BF16), gn_w[l][None, :], gn_b[l][None, :], ln_g[l][None, :], ln_b[l][None, :], seg)
```

```python
import math

import jax
import jax.numpy as jnp
from jax import lax
from jax.experimental import pallas as pl
from jax.experimental.pallas import tpu as pltpu

F32 = jnp.float32
BF16 = jnp.bfloat16
FP8 = jnp.float8_e4m3fn
HIGHEST = lax.Precision.HIGHEST

D_MODEL = 1024
GRID_W = 64
HEAD_DIM = 64
N_Q_HEADS = 8
N_KV_HEADS = 2
GQA_GROUP = N_Q_HEADS // N_KV_HEADS
ATT_WIDTH = N_Q_HEADS * HEAD_DIM
KV_WIDTH = N_KV_HEADS * HEAD_DIM
QKV_WIDTH = ATT_WIDTH + 2 * KV_WIDTH
ROPE_FREQS = HEAD_DIM // 4
ROPE_BASE = 10000.0
RWKV_HEAD = 64
RWKV_HEADS = 8
RWKV_WIDTH = RWKV_HEADS * RWKV_HEAD
W_LORA = 64
A_LORA = 64
SHIFT_WIDTH = 3 * RWKV_WIDTH + 2 * W_LORA + 2 * A_LORA
IN_WIDTH = 2 * ATT_WIDTH + 2 * KV_WIDTH + SHIFT_WIDTH + RWKV_WIDTH
NAT_WIDTH = IN_WIDTH - QKV_WIDTH
LN_EPS = 1e-5
QK_EPS = 1e-6
GN_EPS = 64e-5
DEPTH = 1
ALPHA = (2.0 * DEPTH) ** 0.25

CHUNK = 64
KEY_BLOCK = 256
V_ROWS = HEAD_DIM + 8
FP8_DEPTH = 3 * HEAD_DIM
FP8_LO_SCALE = 16.0
FP8_Q_SCALE = 2.0
FP8_MAX = 448.0
TOKEN_TILE = 256
INPROJ_TILE = 512
MERGE_TILE = 512
Q_TILE = 512
ATTN_COLS = 256
ATTN_UNIT_BLOCKS = 4
ATTN_UNROLL = 4
WKV_GROUP = 2
INV_BLOCK = 16
SEG_SUM_PARTS = 1
HEAD_PAIRS = RWKV_HEADS // 2
PAIR_WIDTH = 2 * RWKV_HEAD
PAIR_SLAB = (HEAD_PAIRS, RWKV_HEAD, PAIR_WIDTH)
SUBLANES = 8
V7X_VMEM_BYTES = 64 * 1024 * 1024
VMEM_LIMIT = V7X_VMEM_BYTES * 3 // 4
NEG_BIG = -1e30


def _cparams(sem):
    return pltpu.CompilerParams(dimension_semantics=sem, vmem_limit_bytes=VMEM_LIMIT)


def _silu(x):
    return x * jax.nn.sigmoid(x)


def _layer_norm(x):
    mu = jnp.mean(x, axis=-1, keepdims=True)
    xc = x - mu
    var = jnp.mean(xc * xc, axis=-1, keepdims=True)
    return xc * lax.rsqrt(var + LN_EPS)


def _split_bf16(x, parts):
    out = []
    rem = x
    for _ in range(parts):
        p = rem.astype(BF16)
        out.append(p)
        rem = rem - p.astype(F32)
    return out


def _dot_split_lhs(x, m_bf16, parts):
    acc = None
    for p in _split_bf16(x, parts):
        t = jnp.dot(p, m_bf16, preferred_element_type=F32)
        acc = t if acc is None else acc + t
    return acc


def _dot_split_rhs(m_bf16, x, parts):
    acc = None
    for p in _split_bf16(x, parts):
        t = jnp.dot(m_bf16, p, preferred_element_type=F32)
        acc = t if acc is None else acc + t
    return acc


def _ada_kernel(c_ref, w_ref, b_ref, o_ref):
    s = _silu(c_ref[...])
    o_ref[...] = jnp.dot(s, w_ref[...], precision=HIGHEST, preferred_element_type=F32) + b_ref[...]


def _ada(cc, w_ada, b_ada):
    rows = cc.shape[0]
    nblk = (3 * D_MODEL) // D_MODEL
    return pl.pallas_call(
        _ada_kernel,
        grid=(nblk,),
        in_specs=[
            pl.BlockSpec((rows, D_MODEL), lambda j: (0, 0)),
            pl.BlockSpec((D_MODEL, D_MODEL), lambda j: (0, j)),
            pl.BlockSpec((1, D_MODEL), lambda j: (0, j)),
        ],
        out_specs=pl.BlockSpec((rows, D_MODEL), lambda j: (0, j)),
        out_shape=jax.ShapeDtypeStruct((rows, 3 * D_MODEL), F32),
        compiler_params=_cparams(("arbitrary",)),
        name="ada",
    )(cc, w_ada, b_ada)


def _rope_t(x, cos, sin):
    xs = jnp.concatenate([x[:, 16:32], x[:, 0:16], x[:, 48:64], x[:, 32:48]], axis=1)
    return x * cos + xs * sin


def _fp8_pieces(x):
    hi = x.astype(FP8).astype(F32)
    lo16 = ((x - hi) * FP8_LO_SCALE).astype(FP8).astype(F32)
    hi16 = (hi * (1.0 / FP8_LO_SCALE)).astype(FP8).astype(F32)
    return hi, hi16, lo16


def _inproj_kernel(x_ref, mod_ref, cos_ref, sin_ref, wt_ref, wn_ref, qg_ref, kg_ref,
                   q_ref, k_ref, q8_ref, k8_ref, v_ref, ga_ref, mix_ref, gb_ref):
    tm = x_ref.shape[1]
    x = x_ref[0]
    shift = mod_ref[0, 0:1, :]
    scale = mod_ref[0, 1:2, :]
    h = _layer_norm(x) * (1.0 + scale) + shift
    hb = h.astype(BF16)
    zt = lax.dot_general(wt_ref[...], hb, (((1,), (1,)), ((), ())), preferred_element_type=F32)
    zn = jnp.dot(hb, wn_ref[...], preferred_element_type=F32)
    cos = cos_ref[...][None]
    sin = sin_ref[...][None]

    qz = zt[0:ATT_WIDTH].reshape(N_Q_HEADS, HEAD_DIM, tm)
    qn = qz * lax.rsqrt(jnp.mean(qz * qz, axis=1, keepdims=True) + QK_EPS) * qg_ref[...][None]
    qr = _rope_t(qn, cos, sin) * (HEAD_DIM ** -0.5 * math.log2(math.e))
    q_ref[0] = qr.astype(BF16)
    q_hi, q_hi16, q_lo16 = _fp8_pieces(qr * FP8_Q_SCALE)
    q8_ref[0] = jnp.concatenate([q_hi, q_hi16, q_lo16], axis=1).astype(FP8)

    kz = zt[ATT_WIDTH:ATT_WIDTH + KV_WIDTH].reshape(N_KV_HEADS, HEAD_DIM, tm)
    kn = kz * lax.rsqrt(jnp.mean(kz * kz, axis=1, keepdims=True) + QK_EPS) * kg_ref[...][None]
    kr = _rope_t(kn, cos, sin)
    k_hi, k_hi16, k_lo16 = _fp8_pieces(kr * (1.0 / FP8_Q_SCALE))
    k8 = jnp.concatenate([k_hi, k_lo16, k_hi16], axis=1)
    vz = zt[ATT_WIDTH + KV_WIDTH:QKV_WIDTH].reshape(N_KV_HEADS, HEAD_DIM, tm)
    pad_row = lax.broadcasted_iota(jnp.int32, (V_ROWS - HEAD_DIM, KEY_BLOCK), 0)
    ones_pad = jnp.where(pad_row == 0, 1.0, 0.0).astype(BF16)
    for j in range(N_KV_HEADS):
        for kb in range(tm // KEY_BLOCK):
            cols = slice(kb * KEY_BLOCK, (kb + 1) * KEY_BLOCK)
            k_ref[0, j, kb] = kr[j][:, cols].T.astype(BF16)
            k8_ref[0, j, kb] = k8[j][:, cols].T.astype(FP8)
            v_ref[0, j, kb, 0:HEAD_DIM, :] = vz[j][:, cols].astype(BF16)
            v_ref[0, j, kb, HEAD_DIM:V_ROWS, :] = ones_pad

    ga_ref[0] = _silu(zn[:, 0:ATT_WIDTH]).astype(BF16)
    mix_ref[0] = zn[:, ATT_WIDTH:ATT_WIDTH + SHIFT_WIDTH]
    gb_ref[0] = _silu(zn[:, ATT_WIDTH + SHIFT_WIDTH:NAT_WIDTH]).astype(BF16)


def _inproj(x, mod, cos_t, sin_t, w_t, w_n, q_gain, k_gain):
    b, t, _ = x.shape
    tm = min(INPROJ_TILE, t)
    assert tm % KEY_BLOCK == 0 and t % tm == 0
    nt = t // tm
    kbt = tm // KEY_BLOCK
    nkb = t // KEY_BLOCK
    const = lambda shape: pl.BlockSpec(shape, lambda bi, i: (0,) * len(shape))
    return pl.pallas_call(
        _inproj_kernel,
        grid=(b, nt),
        in_specs=[
            pl.BlockSpec((1, tm, D_MODEL), lambda bi, i: (bi, i, 0)),
            pl.BlockSpec((1, 3, D_MODEL), lambda bi, i: (bi, 0, 0)),
            pl.BlockSpec((HEAD_DIM, tm), lambda bi, i: (0, i)),
            pl.BlockSpec((HEAD_DIM, tm), lambda bi, i: (0, i)),
            const((QKV_WIDTH, D_MODEL)),
            const((D_MODEL, NAT_WIDTH)),
            const((HEAD_DIM, 1)),
            const((HEAD_DIM, 1)),
        ],
        out_specs=[
            pl.BlockSpec((1, N_Q_HEADS, HEAD_DIM, tm), lambda bi, i: (bi, 0, 0, i)),
            pl.BlockSpec((1, N_KV_HEADS, kbt, KEY_BLOCK, HEAD_DIM), lambda bi, i: (bi, 0, i, 0, 0)),
            pl.BlockSpec((1, N_Q_HEADS, FP8_DEPTH, tm), lambda bi, i: (bi, 0, 0, i)),
            pl.BlockSpec((1, N_KV_HEADS, kbt, KEY_BLOCK, FP8_DEPTH), lambda bi, i: (bi, 0, i, 0, 0)),
            pl.BlockSpec((1, N_KV_HEADS, kbt, V_ROWS, KEY_BLOCK), lambda bi, i: (bi, 0, i, 0, 0)),
            pl.BlockSpec((1, tm, ATT_WIDTH), lambda bi, i: (bi, i, 0)),
            pl.BlockSpec((1, tm, SHIFT_WIDTH), lambda bi, i: (bi, i, 0)),
            pl.BlockSpec((1, tm, RWKV_WIDTH), lambda bi, i: (bi, i, 0)),
        ],
        out_shape=[
            jax.ShapeDtypeStruct((b, N_Q_HEADS, HEAD_DIM, t), BF16),
            jax.ShapeDtypeStruct((b, N_KV_HEADS, nkb, KEY_BLOCK, HEAD_DIM), BF16),
            jax.ShapeDtypeStruct((b, N_Q_HEADS, FP8_DEPTH, t), FP8),
            jax.ShapeDtypeStruct((b, N_KV_HEADS, nkb, KEY_BLOCK, FP8_DEPTH), FP8),
            jax.ShapeDtypeStruct((b, N_KV_HEADS, nkb, V_ROWS, KEY_BLOCK), BF16),
            jax.ShapeDtypeStruct((b, t, ATT_WIDTH), BF16),
            jax.ShapeDtypeStruct((b, t, SHIFT_WIDTH), F32),
            jax.ShapeDtypeStruct((b, t, RWKV_WIDTH), BF16),
        ],
        compiler_params=_cparams(("parallel", "parallel")),
        name="inproj",
    )(x, mod, cos_t, sin_t, w_t, w_n, q_gain, k_gain)


def _attn_kernel(q_ref, k_ref, v_ref, o_ref, s_sc, cmax_sc, m_sc, acc_sc):
    nkb = k_ref.shape[2]
    tq = q_ref.shape[3]
    width = GQA_GROUP * tq
    nchunk = width // ATTN_COLS
    per_head = tq // ATTN_COLS
    kb = KEY_BLOCK
    kbt = ATTN_UNIT_BLOCKS
    m_sc[...] = jnp.full(m_sc.shape, NEG_BIG, F32)
    acc_sc[...] = jnp.zeros(acc_sc.shape, F32)
    first = nkb % kbt if nkb % kbt else kbt
    nunits = 1 + (nkb - first) // kbt

    def unit_start(u):
        return first + (u - 1) * kbt

    def tick(u, par, nb_acc, nb_score):
        if nb_acc:
            acc0 = 0 if (isinstance(u, int) and u == 1) else unit_start(u - 1)
            m_old = m_sc[...]
            m_new = jnp.maximum(m_old, cmax_sc[1 - par])
            alpha = jnp.exp2(m_old - m_new)
            m_sc[...] = m_new
        if nb_score:
            sc0 = 0 if (isinstance(u, int) and u == 0) else unit_start(u)
        for ch in range(nchunk):
            cols = slice(ch * ATTN_COLS, (ch + 1) * ATTN_COLS)
            if nb_acc:
                pv = None
                for i in range(nb_acc):
                    p = jnp.exp2(s_sc[1 - par, i * kb:(i + 1) * kb, cols] - m_new[:, cols]).astype(BF16)
                    t = jnp.dot(v_ref[0, 0, acc0 + i], p, preferred_element_type=F32)
                    pv = t if pv is None else pv + t
                acc_sc[:, cols] = acc_sc[:, cols] * alpha[:, cols] + pv
            if nb_score:
                g, h = divmod(ch, per_head)
                qc = q_ref[0, g, :, h * ATTN_COLS:(h + 1) * ATTN_COLS]
                cm = None
                for i in range(nb_score):
                    s = jnp.dot(k_ref[0, 0, sc0 + i], qc, preferred_element_type=F32)
                    s_sc[par, i * kb:(i + 1) * kb, cols] = s
                    c1 = jnp.max(s, axis=0, keepdims=True)
                    cm = c1 if cm is None else jnp.maximum(cm, c1)
                cmax_sc[par, :, cols] = cm

    tick(0, 0, 0, first)
    if nunits > 1:
        tick(1, 1, first, kbt)
    nsteady = max(nunits - 2, 0)
    npeel = nsteady % ATTN_UNROLL
    for u in range(2, 2 + npeel):
        tick(u, u % 2, kbt, kbt)
    base = 2 + npeel

    def group(j, carry):
        for i in range(ATTN_UNROLL):
            tick(base + ATTN_UNROLL * j + i, (base + i) % 2, kbt, kbt)
        return carry

    lax.fori_loop(0, (nsteady - npeel) // ATTN_UNROLL, group, 0)
    tick(nunits, nunits % 2, kbt if nunits > 1 else first, 0)
    for g in range(GQA_GROUP):
        acc = acc_sc[:, g * tq:(g + 1) * tq]
        o = acc[0:HEAD_DIM] / acc[HEAD_DIM:HEAD_DIM + 1]
        o_ref[0, :, g * HEAD_DIM:(g + 1) * HEAD_DIM] = o.T


def _attention(q_t, k_blk, v_blk):
    b, _, depth, t = q_t.shape
    nkb = k_blk.shape[2]
    tq = min(Q_TILE, t)
    assert t % tq == 0 and k_blk.shape[4] == depth
    width = GQA_GROUP * tq
    return pl.pallas_call(
        _attn_kernel,
        grid=(b, N_KV_HEADS, t // tq),
        in_specs=[
            pl.BlockSpec((1, GQA_GROUP, depth, tq), lambda bi, j, i: (bi, j, 0, i)),
            pl.BlockSpec((1, 1, nkb, KEY_BLOCK, depth), lambda bi, j, i: (bi, j, 0, 0, 0)),
            pl.BlockSpec((1, 1, nkb, V_ROWS, KEY_BLOCK), lambda bi, j, i: (bi, j, 0, 0, 0)),
        ],
        out_specs=pl.BlockSpec((1, tq, GQA_GROUP * HEAD_DIM), lambda bi, j, i: (bi, i, j)),
        out_shape=jax.ShapeDtypeStruct((b, t, ATT_WIDTH), F32),
        scratch_shapes=[
            pltpu.VMEM((2, ATTN_UNIT_BLOCKS * KEY_BLOCK, width), F32),
            pltpu.VMEM((2, 1, width), F32),
            pltpu.VMEM((1, width), F32),
            pltpu.VMEM((V_ROWS, width), F32),
        ],
        compiler_params=_cparams(("parallel", "parallel", "parallel")),
        name="attn",
    )(q_t, k_blk, v_blk)


def _bmm(a, b):
    return jnp.einsum('hij,hjk->hik', a.astype(BF16), b.astype(BF16), preferred_element_type=F32)


def _bmm_nt(a, b):
    return jnp.einsum('hik,hjk->hij', a.astype(BF16), b.astype(BF16), preferred_element_type=F32)


def _wkvprep_kernel(mix_ref, prev_ref, next_ref, mup_ref, mun_ref, w0_ref, wup_ref, a0_ref, aup_ref,
                    kk_ref, ka_ref, rk_ref, seg_ref,
                    bonus_ref, qeff_ref, o0_ref, g_ref, h_ref,
                    r_sc, v_sc, kn_sc, ld_sc, kd_sc, b_sc):
    tm = mix_ref.shape[1]
    i = pl.program_id(1)
    n = pl.num_programs(1)
    z = mix_ref[0]
    prev_row = jnp.where(i > 0, prev_ref[0, SUBLANES - 1:SUBLANES, :], 0.0)
    next_row = jnp.where(i < n - 1, next_ref[0, 0:1, :], 0.0)
    rows = lax.broadcasted_iota(jnp.int32, (tm, 1), 0)
    zp = jnp.where(rows == 0, prev_row, pltpu.roll(z, 1, 0))
    zn = jnp.where(rows == tm - 1, next_row, pltpu.roll(z, tm - 1, 0))
    zs = z + mup_ref[...] * (zp - z) + mun_ref[...] * (zn - z)

    w = RWKV_WIDTH
    r = zs[:, 0:w]
    k = zs[:, w:2 * w]
    v = zs[:, 2 * w:3 * w]
    seg = seg_ref[...]
    kkr = k * kk_ref[...]
    norm = jnp.sqrt(_dot_split_lhs(kkr * kkr, seg, SEG_SUM_PARTS))
    kn = kkr / jnp.maximum(norm, 1e-12)
    r_sc[...] = r
    v_sc[...] = v
    kn_sc[...] = kn
    coef = jnp.zeros((tm, w), F32)
    for d in range(2):
        wl = zs[:, 3 * w + d * W_LORA:3 * w + (d + 1) * W_LORA]
        al = zs[:, 3 * w + 2 * W_LORA + d * A_LORA:3 * w + 2 * W_LORA + (d + 1) * A_LORA]
        w_pre = w0_ref[d:d + 1, :] + jnp.dot(jnp.tanh(wl).astype(BF16), wup_ref[d], preferred_element_type=F32)
        ld_sc[d] = -math.exp(-0.5) * jax.nn.sigmoid(w_pre)
        a = jax.nn.sigmoid(a0_ref[d:d + 1, :] + jnp.dot(al.astype(BF16), aup_ref[d], preferred_element_type=F32))
        kd = k * (1.0 + (a - 1.0) * ka_ref[...])
        kd_sc[d] = kd
        b_sc[d] = kn * a
        coef = coef + _dot_split_lhs(r * kd * rk_ref[d:d + 1, :], seg, SEG_SUM_PARTS)
    bonus_ref[0] = coef * v

    ri = lax.broadcasted_iota(jnp.int32, (CHUNK, CHUNK), 0)
    ci = lax.broadcasted_iota(jnp.int32, (CHUNK, CHUNK), 1)
    eye_f = jnp.where(ri == ci, 1.0, 0.0)
    strict_f = [jnp.where(ci < ri, 1.0, 0.0), jnp.where(ci > ri, 1.0, 0.0)]
    incl_f = [s + eye_f for s in strict_f]
    npair = RWKV_HEADS // 2
    pw = 2 * RWKV_HEAD
    lane = lax.broadcasted_iota(jnp.int32, (1, pw), 1)
    even_lanes = lane < RWKV_HEAD
    m_even = jnp.where(even_lanes, 1.0, 0.0)
    m_odd = 1.0 - m_even
    even_lanes2 = jnp.concatenate([even_lanes, even_lanes], axis=1)
    r128 = lax.broadcasted_iota(jnp.int32, (pw, pw), 0)
    c128 = lax.broadcasted_iota(jnp.int32, (pw, pw), 1)
    eye_pw = jnp.where(r128 == c128, 1.0, 0.0)

    def _heads(x):
        return jnp.stack([x[:, p * pw:(p + 1) * pw] for p in range(npair)], axis=0)

    def group(gi, carry):
        combos = [(gi * WKV_GROUP + u, d) for u in range(WKV_GROUP) for d in range(2)]
        slices = []
        parts = {name: [] for name in ("rh", "kkh", "kt", "bt", "kv", "bv", "gam", "vh")}
        for c, d in combos:
            sl = pl.ds(pl.multiple_of(c * CHUNK, CHUNK), CHUNK)
            slices.append(sl)
            ld = ld_sc[d, sl, :]
            cum = _dot_split_rhs(incl_f[d].astype(BF16), ld, 3)
            tot = cum[CHUNK - 1:CHUNK, :] if d == 0 else cum[0:1, :]
            e_pos = jnp.exp(cum)
            e_prev = jnp.exp(cum - ld)
            e_neg = jnp.exp(-cum)
            e_rest = jnp.exp(tot - cum)
            kd_c = kd_sc[d, sl, :]
            b_c = b_sc[d, sl, :]
            parts["rh"].append(_heads(r_sc[sl, :] * e_pos))
            parts["kkh"].append(_heads(kn_sc[sl, :] * e_prev))
            parts["kt"].append(_heads(kd_c * e_neg))
            parts["bt"].append(_heads(b_c * e_neg))
            parts["kv"].append(_heads(kd_c * e_rest))
            parts["bv"].append(_heads(b_c * e_rest))
            parts["gam"].append(_heads(jnp.exp(tot)))
            parts["vh"].append(_heads(v_sc[sl, :]))
        cat = lambda xs: jnp.concatenate(xs, axis=0)
        rh, kkh, kt, bt, kv, bv, gam, vh = (cat(parts[n]) for n in ("rh", "kkh", "kt", "bt", "kv", "bv", "gam", "vh"))
        nbp = len(combos) * npair
        per = lambda ms: cat([jnp.broadcast_to(ms[d][None], (npair, CHUNK, CHUNK)) for _, d in combos])
        both = lambda x: jnp.concatenate([x, x], axis=0)
        masked = lambda x: jnp.concatenate([x * m_even, x * m_odd], axis=0)
        pick = lambda z, ev: jnp.where(ev, z[0:nbp], z[nbp:2 * nbp])
        strict = both(per(strict_f))
        incl = both(per(incl_f))

        kk_m = masked(kkh)
        r_m = masked(rh)
        kt2 = both(kt)
        bt2 = both(bt)
        v2 = both(vh)
        lk = _bmm_nt(kk_m, kt2) * strict
        lb = _bmm_nt(kk_m, bt2) * strict
        ark = _bmm_nt(r_m, kt2) * incl
        arb = _bmm_nt(r_m, bt2) * incl
        blk = lambda n: (ri // n) == (ci // n)
        xp = -(lb * jnp.where(blk(INV_BLOCK), 1.0, 0.0)[None])
        tinv = eye_f[None] + xp
        span = 2
        while span < INV_BLOCK:
            xp = _bmm(xp, xp)
            tinv = tinv + _bmm(tinv, xp)
            span *= 2
        size = INV_BLOCK
        while size < CHUNK:
            join = lb * jnp.where(blk(2 * size) & ~blk(size), 1.0, 0.0)[None]
            tinv = tinv - _bmm(tinv, _bmm(join, tinv))
            size *= 2
        wu_h = _bmm(tinv, jnp.concatenate([both(kkh), _bmm(lk, v2)], axis=2))
        arb_wu = pick(_bmm(arb, wu_h), even_lanes2)
        wu = pick(wu_h, even_lanes2)
        qeff = rh - arb_wu[:, :, 0:pw]
        o0 = pick(_bmm(ark, v2), even_lanes) - arb_wu[:, :, pw:2 * pw]
        bvt_wu = _bmm(jnp.swapaxes(bv, 1, 2), wu)
        kvt_v = _bmm(jnp.swapaxes(kv, 1, 2), vh)
        gfull = eye_pw[None] * gam - bvt_wu[:, :, 0:pw]
        hfull = kvt_v - bvt_wu[:, :, pw:2 * pw]
        slab = lambda z: jnp.where(even_lanes, z[:, 0:RWKV_HEAD, :], z[:, RWKV_HEAD:pw, :])
        gmat = slab(gfull)
        hmat = slab(hfull)
        for idx, (c, d) in enumerate(combos):
            for p in range(npair):
                qeff_ref[d, 0, slices[idx], p * pw:(p + 1) * pw] = qeff[idx * npair + p].astype(BF16)
                o0_ref[d, 0, slices[idx], p * pw:(p + 1) * pw] = o0[idx * npair + p]
            g_ref[d, 0, c] = gmat[idx * npair:(idx + 1) * npair].astype(BF16)
            h_ref[d, 0, c] = hmat[idx * npair:(idx + 1) * npair]
        return carry

    lax.fori_loop(0, tm // (CHUNK * WKV_GROUP), group, 0)


def _wkvprep(mix, mu_prev, mu_next, w0, w_up, a0, a_up, k_k, k_a, r_k, seg):
    b, t, _ = mix.shape
    tm = min(TOKEN_TILE, t)
    assert t % tm == 0 and tm % CHUNK == 0
    nt = t // tm
    nc = tm // CHUNK
    hb = tm // SUBLANES
    last8 = t // SUBLANES - 1
    const = lambda shape: pl.BlockSpec(shape, lambda bi, i: (0,) * len(shape))
    hd = (2, 1, tm, RWKV_WIDTH)
    st = (2, 1, nc) + PAIR_SLAB
    return pl.pallas_call(
        _wkvprep_kernel,
        grid=(b, nt),
        in_specs=[
            pl.BlockSpec((1, tm, SHIFT_WIDTH), lambda bi, i: (bi, i, 0)),
            pl.BlockSpec((1, SUBLANES, SHIFT_WIDTH), lambda bi, i: (bi, jnp.maximum(i * hb - 1, 0), 0)),
            pl.BlockSpec((1, SUBLANES, SHIFT_WIDTH), lambda bi, i: (bi, jnp.minimum((i + 1) * hb, last8), 0)),
            const((1, SHIFT_WIDTH)),
            const((1, SHIFT_WIDTH)),
            const((2, RWKV_WIDTH)),
            const((2, W_LORA, RWKV_WIDTH)),
            const((2, RWKV_WIDTH)),
            const((2, A_LORA, RWKV_WIDTH)),
            const((1, RWKV_WIDTH)),
            const((1, RWKV_WIDTH)),
            const((2, RWKV_WIDTH)),
            const((RWKV_WIDTH, RWKV_WIDTH)),
        ],
        out_specs=[
            pl.BlockSpec((1, tm, RWKV_WIDTH), lambda bi, i: (bi, i, 0)),
            pl.BlockSpec(hd, lambda bi, i: (0, bi, i, 0)),
            pl.BlockSpec(hd, lambda bi, i: (0, bi, i, 0)),
            pl.BlockSpec(st, lambda bi, i: (0, bi, i, 0, 0, 0)),
            pl.BlockSpec(st, lambda bi, i: (0, bi, i, 0, 0, 0)),
        ],
        out_shape=[
            jax.ShapeDtypeStruct((b, t, RWKV_WIDTH), F32),
            jax.ShapeDtypeStruct((2, b, t, RWKV_WIDTH), BF16),
            jax.ShapeDtypeStruct((2, b, t, RWKV_WIDTH), F32),
            jax.ShapeDtypeStruct((2, b, t // CHUNK) + PAIR_SLAB, BF16),
            jax.ShapeDtypeStruct((2, b, t // CHUNK) + PAIR_SLAB, F32),
        ],
        scratch_shapes=[
            pltpu.VMEM((tm, RWKV_WIDTH), F32),
            pltpu.VMEM((tm, RWKV_WIDTH), F32),
            pltpu.VMEM((tm, RWKV_WIDTH), F32),
            pltpu.VMEM((2, tm, RWKV_WIDTH), F32),
            pltpu.VMEM((2, tm, RWKV_WIDTH), F32),
            pltpu.VMEM((2, tm, RWKV_WIDTH), F32),
        ],
        compiler_params=_cparams(("parallel", "parallel")),
        name="wkvprep",
    )(mix, mix, mix, mu_prev, mu_next, w0, w_up, a0, a_up, k_k, k_a, r_k, seg)


def _wkvscan_kernel(qf_ref, qb_ref, of_ref, ob_ref, gf_ref, gb_ref, hf_ref, hb_ref, s0_ref,
                    outf_ref, outb_ref, sfin_ref, s_sc):
    c = pl.program_id(0)
    nb = qf_ref.shape[1]
    nc = gf_ref.shape[2]
    npair = HEAD_PAIRS
    pw = PAIR_WIDTH
    even_lanes = lax.broadcasted_iota(jnp.int32, (1, pw), 1) < RWKV_HEAD

    @pl.when(c == 0)
    def _():
        s_sc[...] = s0_ref[...].reshape(s_sc.shape)

    def bdot(a, s_parts):
        ab = a.astype(BF16)
        return sum(jnp.einsum('hij,hjk->hik', ab, p, preferred_element_type=F32) for p in s_parts)

    def block_diag(slab):
        zero = jnp.zeros_like(slab)
        return jnp.concatenate([jnp.where(even_lanes, slab, zero), jnp.where(even_lanes, zero, slab)], axis=1)

    def pairs(ref, rows):
        return [ref[0, bi, rows, p * pw:(p + 1) * pw] for bi in range(nb) for p in range(npair)]

    for j in range(nc):
        jb = nc - 1 - j
        rf = slice(j * CHUNK, (j + 1) * CHUNK)
        rb = slice(jb * CHUNK, (jb + 1) * CHUNK)
        s = _split_bf16(s_sc[...], 2)
        qe = jnp.stack(pairs(qf_ref, rf) + pairs(qb_ref, rb), axis=0)
        o0 = jnp.stack(pairs(of_ref, rf) + pairs(ob_ref, rb), axis=0)
        gm = jnp.concatenate([gf_ref[0, bi, j] for bi in range(nb)] + [gb_ref[0, bi, jb] for bi in range(nb)], axis=0)
        hm = jnp.concatenate([hf_ref[0, bi, j] for bi in range(nb)] + [hb_ref[0, bi, jb] for bi in range(nb)], axis=0)
        o = bdot(qe, s) + o0
        s_sc[...] = bdot(block_diag(gm), s) + block_diag(hm)
        for bi in range(nb):
            for p in range(npair):
                outf_ref[bi, rf, p * pw:(p + 1) * pw] = o[bi * npair + p]
                outb_ref[bi, rb, p * pw:(p + 1) * pw] = o[(nb + bi) * npair + p]

    @pl.when(c == pl.num_programs(0) - 1)
    def _():
        sfin_ref[...] = s_sc[...].reshape(sfin_ref.shape)


def _wkvscan(qeff, o0, g, h, s0):
    _, b, t, _ = qeff.shape
    tc = min(TOKEN_TILE, t)
    nc = tc // CHUNK
    nt = t // tc
    hd = (1, b, tc, RWKV_WIDTH)
    st = (1, b, nc) + PAIR_SLAB
    fwd_hd = pl.BlockSpec(hd, lambda c: (0, 0, c, 0))
    bwd_hd = pl.BlockSpec(hd, lambda c: (1, 0, nt - 1 - c, 0))
    fwd_st = pl.BlockSpec(st, lambda c: (0, 0, c, 0, 0, 0))
    bwd_st = pl.BlockSpec(st, lambda c: (1, 0, nt - 1 - c, 0, 0, 0))
    s_shape = (2, b, HEAD_PAIRS, PAIR_WIDTH, PAIR_WIDTH)
    s_spec = pl.BlockSpec(s_shape, lambda c: (0, 0, 0, 0, 0))
    return pl.pallas_call(
        _wkvscan_kernel,
        grid=(nt,),
        in_specs=[fwd_hd, bwd_hd, fwd_hd, bwd_hd, fwd_st, bwd_st, fwd_st, bwd_st, s_spec],
        out_specs=[
            pl.BlockSpec((b, tc, RWKV_WIDTH), lambda c: (0, c, 0)),
            pl.BlockSpec((b, tc, RWKV_WIDTH), lambda c: (0, nt - 1 - c, 0)),
            s_spec,
        ],
        out_shape=[
            jax.ShapeDtypeStruct((b, t, RWKV_WIDTH), F32),
            jax.ShapeDtypeStruct((b, t, RWKV_WIDTH), F32),
            jax.ShapeDtypeStruct(s_shape, F32),
        ],
        scratch_shapes=[pltpu.VMEM((2 * b * HEAD_PAIRS, PAIR_WIDTH, PAIR_WIDTH), F32)],
        compiler_params=_cparams(("arbitrary",)),
        name="wkvscan",
    )(qeff, qeff, o0, o0, g, g, h, h, s0)


def _merge_kernel(x_ref, mod_ref, ya_ref, ga_ref, of_ref, ob_ref, bonus_ref, gb_ref,
                  wpa_ref, wpb_ref, wmg_ref, bmg_ref, wo_ref, gnw_ref, gnb_ref, lng_ref, lnb_ref, seg_ref,
                  out_ref):
    x = x_ref[0]
    shift = mod_ref[0, 0:1, :]
    scale = mod_ref[0, 1:2, :]
    gate = mod_ref[0, 2:3, :]
    h = _layer_norm(x) * (1.0 + scale) + shift
    gates = jax.nn.sigmoid(jnp.dot(h.astype(BF16), wmg_ref[...], preferred_element_type=F32) + bmg_ref[...])
    ya = jnp.dot((ya_ref[0] * ga_ref[0].astype(F32)).astype(BF16), wpa_ref[...], preferred_element_type=F32)

    o = of_ref[0] + ob_ref[0]
    seg = seg_ref[...]
    mu = _dot_split_lhs(o, seg, SEG_SUM_PARTS) * (1.0 / RWKV_HEAD)
    oc = o - mu
    var = _dot_split_lhs(oc * oc, seg, SEG_SUM_PARTS) * (1.0 / RWKV_HEAD)
    y = oc * lax.rsqrt(var + GN_EPS) * gnw_ref[...] + gnb_ref[...] + bonus_ref[0]
    yb = jnp.dot((y * gb_ref[0].astype(F32)).astype(BF16), wpb_ref[...], preferred_element_type=F32)

    mixed = gates[:, 0:D_MODEL] * ya + gates[:, D_MODEL:2 * D_MODEL] * yb
    out = jnp.dot(mixed.astype(BF16), wo_ref[...], preferred_element_type=F32)
    out_ref[0] = _layer_norm(ALPHA * x + gate * out) * lng_ref[...] + lnb_ref[...]


def _merge(x, mod, y_att, ga, o_f, o_b, bonus, gb, w_pa, w_pb, w_mg, b_mg, w_o, gn_w, gn_b, ln_g, ln_b, seg):
    b, t, _ = x.shape
    tm = min(MERGE_TILE, t)
    assert t % tm == 0
    const = lambda shape: pl.BlockSpec(shape, lambda bi, i: (0,) * len(shape))
    tok = lambda width: pl.BlockSpec((1, tm, width), lambda bi, i: (bi, i, 0))
    return pl.pallas_call(
        _merge_kernel,
        grid=(b, t // tm),
        in_specs=[
            tok(D_MODEL),
            pl.BlockSpec((1, 3, D_MODEL), lambda bi, i: (bi, 0, 0)),
            tok(ATT_WIDTH),
            tok(ATT_WIDTH),
            tok(RWKV_WIDTH),
            tok(RWKV_WIDTH),
            tok(RWKV_WIDTH),
            tok(RWKV_WIDTH),
            const((ATT_WIDTH, D_MODEL)),
            const((RWKV_WIDTH, D_MODEL)),
            const((D_MODEL, 2 * D_MODEL)),
            const((1, 2 * D_MODEL)),
            const((D_MODEL, D_MODEL)),
            const((1, RWKV_WIDTH)),
            const((1, RWKV_WIDTH)),
            const((1, D_MODEL)),
            const((1, D_MODEL)),
            const((RWKV_WIDTH, RWKV_WIDTH)),
        ],
        out_specs=tok(D_MODEL),
        out_shape=jax.ShapeDtypeStruct((b, t, D_MODEL), F32),
        compiler_params=_cparams(("parallel", "parallel")),
        name="merge",
    )(x, mod, y_att, ga, o_f, o_b, bonus, gb, w_pa, w_pb, w_mg, b_mg, w_o, gn_w, gn_b, ln_g, ln_b, seg)


def _rope_tables_t(t):
    pos = jnp.arange(t, dtype=jnp.int32)
    row = (pos // GRID_W).astype(F32)
    col = (pos % GRID_W).astype(F32)
    inv = ROPE_BASE ** (-jnp.arange(ROPE_FREQS, dtype=F32) / ROPE_FREQS)
    ang_r = inv[:, None] * row[None, :]
    ang_c = inv[:, None] * col[None, :]
    cos_t = jnp.concatenate([jnp.cos(ang_r), jnp.cos(ang_r), jnp.cos(ang_c), jnp.cos(ang_c)], axis=0)
    sin_t = jnp.concatenate([-jnp.sin(ang_r), jnp.sin(ang_r), -jnp.sin(ang_c), jnp.sin(ang_c)], axis=0)
    return cos_t, sin_t


def kernel(x, c, ctx, c_ctx, w_ada, b_ada, w_in, q_norm, k_norm, mu_prev, mu_next, w0, w_up, a0, a_up,
           k_k, k_a, r_k, gn_w, gn_b, w_pa, w_pb, w_mg, b_mg, w_o, ln_g, ln_b):
    b, t, _ = x.shape
    tc = ctx.shape[1]
    l = 0
    seg = jnp.kron(jnp.eye(RWKV_HEADS, dtype=F32), jnp.ones((RWKV_HEAD, RWKV_HEAD), F32)).astype(BF16)

    rows = SUBLANES * pl.cdiv(b + 1, SUBLANES)
    cc = jnp.zeros((rows, D_MODEL), F32).at[0:b].set(c).at[b].set(c_ctx)
    ada = _ada(cc, w_ada[l], b_ada[l][None, :])
    mod_x = ada[0:b].reshape(b, 3, D_MODEL)
    mod_c = jnp.broadcast_to(ada[b].reshape(1, 3, D_MODEL), (b, 3, D_MODEL))

    w_t = w_in[l][:, 0:QKV_WIDTH].T.astype(BF16)
    w_n = w_in[l][:, QKV_WIDTH:].astype(BF16)
    q_gain = q_norm[l][:, None]
    k_gain = k_norm[l][:, None]
    cos_x, sin_x = _rope_tables_t(t)
    cos_c = jnp.ones((HEAD_DIM, tc), F32)
    sin_c = jnp.zeros((HEAD_DIM, tc), F32)

    q_x, k_x, q8_x, k8_x, v_x, ga_x, mix_x, gb_x = _inproj(x, mod_x, cos_x, sin_x, w_t, w_n, q_gain, k_gain)
    _, k_c, _, k8_c, v_c, _, mix_c, _ = _inproj(ctx, mod_c, cos_c, sin_c, w_t, w_n, q_gain, k_gain)

    v_all = jnp.concatenate([v_c, v_x], axis=2)
    fp8_ok = jnp.logical_and(
        jnp.max(jnp.abs(q_norm[l])) * (2.0 * HEAD_DIM ** 0.5 * HEAD_DIM ** -0.5 * math.log2(math.e) * FP8_Q_SCALE) < FP8_MAX,
        jnp.max(jnp.abs(k_norm[l])) * (2.0 * HEAD_DIM ** 0.5 / FP8_Q_SCALE) < FP8_MAX)
    y_att = lax.cond(
        fp8_ok,
        lambda: _attention(q8_x, jnp.concatenate([k8_c, k8_x], axis=2), v_all),
        lambda: _attention(q_x, jnp.concatenate([k_c, k_x], axis=2), v_all))

    prep_args = (mu_prev[l][None, :], mu_next[l][None, :], w0[l], w_up[l].astype(BF16), a0[l], a_up[l].astype(BF16),
                 k_k[l][None, :], k_a[l][None, :], r_k[l].reshape(2, RWKV_WIDTH), seg)
    _, qe_c, o0_c, g_c, h_c = _wkvprep(mix_c, *prep_args)
    zero_state = jnp.zeros((2, b, HEAD_PAIRS, PAIR_WIDTH, PAIR_WIDTH), F32)
    _, _, s_ctx = _wkvscan(qe_c, o0_c, g_c, h_c, zero_state)
    bonus, qe_x, o0_x, g_x, h_x = _wkvprep(mix_x, *prep_args)
    of_x, ob_x, _ = _wkvscan(qe_x, o0_x, g_x, h_x, s_ctx)

    return _merge(x, mod_x, y_att, ga_x, of_x, ob_x, bonus, gb_x,
                  w_pa[l].astype(BF16), w_pb[l].astype(BF16), w_mg[l].astype(BF16), b_mg[l][None, :],
                  w_o[l].astype(BF16), gn_w[l][None, :], gn_b[l][None, :], ln_g[l][None, :], ln_b[l][None, :], seg)
```
